```python
import math
import jax, jax.numpy as jnp
from jax import lax
import numpy as np

D_MODEL = 1024
BATCH = 32
SEQ = 2048
DEPTH = 4
DEC_BATCH = 2
DEC_SEQ = 16384
PAST_LEN = 128

N_MIXERS = 3
EXPAND = 2
BRANCH_WIDTH = EXPAND * D_MODEL
NORM_EPS = 1e-6

A_HEAD_DIM = 64
A_Q_HEADS = BRANCH_WIDTH // A_HEAD_DIM
A_KV_HEADS = max(1, A_Q_HEADS // 8)
A_GROUP = A_Q_HEADS // A_KV_HEADS
A_ROT_DIM = A_HEAD_DIM // 4
A_ROPE_THETA = 500000.0
A_WINDOW = 128
A_BLOCK = 128
A_KEY_SPAN = A_BLOCK + 2 * A_WINDOW
A_IN = A_Q_HEADS * A_HEAD_DIM + 2 * A_KV_HEADS * A_HEAD_DIM + BRANCH_WIDTH

B_QK_DIM = 256
B_HEADS = D_MODEL // B_QK_DIM
B_V_DIM = BRANCH_WIDTH // B_HEADS
B_CHUNK = 128
B_ROPE_THETA = 10000.0
B_IN = 2 * B_HEADS * B_QK_DIM + 2 * BRANCH_WIDTH
B_DECAY_HI = 1.0 / 32.0
B_DECAY_LO = 1.0 / 512.0

C_GROUPS = 4
C_GROUP_DIM = BRANCH_WIDTH // C_GROUPS
C_IN = 2 * BRANCH_WIDTH

kernel_name = "hybrid_bidir_swa_retention_fnet_encoder"


def rms_norm(x, g):
    x32 = x.astype(jnp.float32)
    y = x32 * lax.rsqrt(jnp.mean(x32 * x32, axis=-1, keepdims=True) + NORM_EPS)
    return (y * g.astype(jnp.float32)).astype(x.dtype)


def apply_rope(x, rot_dim, theta):
    seq = x.shape[1]
    half = rot_dim // 2
    inv_freq = jnp.exp(-(jnp.arange(half, dtype=jnp.float32) * (2.0 / rot_dim)) * math.log(theta))
    ang = jnp.arange(seq, dtype=jnp.float32)[:, None] * inv_freq[None, :]
    cos = jnp.cos(ang)[None, :, None, :]
    sin = jnp.sin(ang)[None, :, None, :]
    xr = x[..., :rot_dim].astype(jnp.float32)
    x1, x2 = xr[..., :half], xr[..., half:]
    rot = jnp.concatenate([x1 * cos - x2 * sin, x2 * cos + x1 * sin], axis=-1).astype(x.dtype)
    if rot_dim == x.shape[-1]:
        return rot
    return jnp.concatenate([rot, x[..., rot_dim:]], axis=-1)


def window_attention(h, w_in, sink, w_out):
    bsz, seq, _ = h.shape
    proj = h @ w_in
    nq = A_Q_HEADS * A_HEAD_DIM
    nkv = A_KV_HEADS * A_HEAD_DIM
    q = proj[..., :nq].reshape(bsz, seq, A_Q_HEADS, A_HEAD_DIM)
    k = proj[..., nq:nq + nkv].reshape(bsz, seq, A_KV_HEADS, A_HEAD_DIM)
    v = proj[..., nq + nkv:nq + 2 * nkv].reshape(bsz, seq, A_KV_HEADS, A_HEAD_DIM)
    gate = proj[..., nq + 2 * nkv:]
    q = apply_rope(q, A_ROT_DIM, A_ROPE_THETA) * (A_HEAD_DIM ** -0.5)
    k = apply_rope(k, A_ROT_DIM, A_ROPE_THETA)
    nb = seq // A_BLOCK
    q_blocks = q.reshape(bsz, nb, A_BLOCK, A_KV_HEADS, A_GROUP, A_HEAD_DIM).transpose(1, 0, 2, 3, 4, 5)
    kp = jnp.pad(k, ((0, 0), (A_WINDOW, A_WINDOW), (0, 0), (0, 0)))
    vp = jnp.pad(v, ((0, 0), (A_WINDOW, A_WINDOW), (0, 0), (0, 0)))
    kvalid = jnp.pad(jnp.ones((seq,), dtype=bool), (A_WINDOW, A_WINDOW))
    s_idx = jnp.arange(A_BLOCK)[:, None]
    t_idx = jnp.arange(A_KEY_SPAN)[None, :]
    band = jnp.abs(s_idx + A_WINDOW - t_idx) <= A_WINDOW
    sink_l = sink.astype(jnp.float32).reshape(A_KV_HEADS, A_GROUP)[None, :, :, None, None]

    def one_block(args):
        qb, start = args
        kb = lax.dynamic_slice_in_dim(kp, start, A_KEY_SPAN, axis=1)
        vb = lax.dynamic_slice_in_dim(vp, start, A_KEY_SPAN, axis=1)
        mb = lax.dynamic_slice_in_dim(kvalid, start, A_KEY_SPAN, axis=0)
        scores = jnp.einsum('bqkgd,btkd->bkgqt', qb, kb).astype(jnp.float32)
        mask = (band & mb[None, :])[None, None, None]
        scores = jnp.where(mask, scores, -1e30)
        mx = jnp.maximum(jnp.max(scores, axis=-1, keepdims=True), sink_l)
        p = jnp.exp(scores - mx)
        p = p / (jnp.sum(p, axis=-1, keepdims=True) + jnp.exp(sink_l - mx))
        out = jnp.einsum('bkgqt,btkd->bqkgd', p.astype(vb.dtype), vb)
        return out.reshape(bsz, A_BLOCK, A_Q_HEADS * A_HEAD_DIM)

    starts = jnp.arange(nb, dtype=jnp.int32) * A_BLOCK
    out = lax.map(one_block, (q_blocks, starts))
    out = out.transpose(1, 0, 2, 3).reshape(bsz, seq, A_Q_HEADS * A_HEAD_DIM)
    return (out * jax.nn.silu(gate)) @ w_out


def retention_scan(q, k, v, log_g, include_diag):
    bsz, seq, nh, dk = q.shape
    dv = v.shape[-1]
    nc = seq // B_CHUNK
    idx = jnp.arange(B_CHUNK, dtype=jnp.float32)
    diff = idx[:, None] - idx[None, :]
    causal = (diff >= 0) if include_diag else (diff > 0)
    intra = jnp.where(causal[None], jnp.exp(jnp.where(causal, diff, 0.0)[None] * log_g[:, None, None]), 0.0)
    q_decay = jnp.exp((idx + 1.0)[:, None] * log_g[None, :])[None, :, :, None]
    k_decay = jnp.exp((B_CHUNK - 1.0 - idx)[:, None] * log_g[None, :])[None, :, :, None]
    chunk_decay = jnp.exp(B_CHUNK * log_g)[None, :, None, None]

    def to_chunks(t):
        return t.reshape(bsz, nc, B_CHUNK, nh, t.shape[-1]).transpose(1, 0, 2, 3, 4)

    def step(state, inp):
        qc, kc, vc = inp
        scores = jnp.einsum('bihd,bjhd->bhij', qc, kc) * intra[None]
        inner = jnp.einsum('bhij,bjhe->bihe', scores, vc)
        cross = jnp.einsum('bihd,bhde->bihe', qc, state) * q_decay
        new_state = state * chunk_decay + jnp.einsum('bjhd,bjhe->bhde', kc * k_decay, vc)
        return new_state, inner + cross

    state0 = jnp.zeros((bsz, nh, dk, dv), jnp.float32)
    _, out = lax.scan(step, state0, (to_chunks(q), to_chunks(k), to_chunks(v)))
    return out.transpose(1, 0, 2, 3, 4).reshape(bsz, seq, nh, dv)


def retention(h, w_in, decay_logit, w_out):
    bsz, seq, _ = h.shape
    proj = h @ w_in
    nqk = B_HEADS * B_QK_DIM
    q = proj[..., :nqk].reshape(bsz, seq, B_HEADS, B_QK_DIM)
    k = proj[..., nqk:2 * nqk].reshape(bsz, seq, B_HEADS, B_QK_DIM)
    v = proj[..., 2 * nqk:2 * nqk + BRANCH_WIDTH].reshape(bsz, seq, B_HEADS, B_V_DIM)
    gate = proj[..., 2 * nqk + BRANCH_WIDTH:]
    q = apply_rope(q, B_QK_DIM, B_ROPE_THETA).astype(jnp.float32)
    k = apply_rope(k, B_QK_DIM, B_ROPE_THETA).astype(jnp.float32) * (B_QK_DIM ** -0.5)
    v = v.astype(jnp.float32)
    log_g = jax.nn.log_sigmoid(decay_logit.astype(jnp.float32))
    fwd = retention_scan(q, k, v, log_g[0], True)
    bwd = jnp.flip(retention_scan(jnp.flip(q, 1), jnp.flip(k, 1), jnp.flip(v, 1), log_g[1], False), 1)
    o = fwd + bwd
    o = o * lax.rsqrt(jnp.mean(o * o, axis=-1, keepdims=True) + NORM_EPS)
    o = o.reshape(bsz, seq, BRANCH_WIDTH).astype(h.dtype)
    return (o * jax.nn.silu(gate)) @ w_out


def fourier_mix(h, w_in, w_out):
    bsz, seq, _ = h.shape
    proj = h @ w_in
    u = proj[..., :BRANCH_WIDTH].reshape(bsz, seq, C_GROUPS, C_GROUP_DIM).astype(jnp.float32)
    gate = proj[..., BRANCH_WIDTH:]
    mixed = jnp.real(jnp.fft.fftn(u, axes=(1, 3), norm='ortho'))
    mixed = mixed.reshape(bsz, seq, BRANCH_WIDTH).astype(h.dtype)
    return (mixed * jax.nn.silu(gate)) @ w_out


def trunk(x, norm_g, final_norm_g, a_w_in, a_sink, a_w_out, b_w_in, b_decay, b_w_out, c_w_in, c_w_out):
    counts = [0, 0, 0]
    for layer in range(DEPTH):
        kind = layer % N_MIXERS
        j = counts[kind]
        counts[kind] += 1
        h = rms_norm(x, norm_g[layer])
        if kind == 0:
            d = window_attention(h, a_w_in[j], a_sink[j], a_w_out[j])
        elif kind == 1:
            d = retention(h, b_w_in[j], b_decay[j], b_w_out[j])
        else:
            d = fourier_mix(h, c_w_in[j], c_w_out[j])
        x = x + d.astype(x.dtype)
    return rms_norm(x, final_norm_g)


def setup_inputs(seed: int = 0) -> dict:
    key = jax.random.key(seed)
    ks = jax.random.split(key, 16)
    n_a = sum(1 for l in range(DEPTH) if l % N_MIXERS == 0)
    n_b = sum(1 for l in range(DEPTH) if l % N_MIXERS == 1)
    n_c = sum(1 for l in range(DEPTH) if l % N_MIXERS == 2)
    f32 = jnp.float32
    x_prompt = jax.random.normal(ks[0], (BATCH, SEQ, D_MODEL), f32)
    x_sample = jax.random.normal(ks[1], (DEC_BATCH, DEC_SEQ, D_MODEL), f32)
    norm_g = 1.0 + 0.02 * jax.random.normal(ks[2], (DEPTH, D_MODEL), f32)
    final_norm_g = 1.0 + 0.02 * jax.random.normal(ks[3], (D_MODEL,), f32)
    a_w_in = jax.random.normal(ks[4], (n_a, D_MODEL, A_IN), f32) * D_MODEL ** -0.5
    a_sink = jax.random.normal(ks[5], (n_a, A_Q_HEADS), f32)
    a_w_out = jax.random.normal(ks[6], (n_a, A_Q_HEADS * A_HEAD_DIM, D_MODEL), f32) * (A_Q_HEADS * A_HEAD_DIM) ** -0.5
    b_w_in = jax.random.normal(ks[7], (n_b, D_MODEL, B_IN), f32) * D_MODEL ** -0.5
    lin = jnp.linspace(math.log(B_DECAY_HI), math.log(B_DECAY_LO), B_HEADS, dtype=f32)
    base_logit = jnp.log1p(-jnp.exp(lin)) - lin
    b_decay = base_logit[None, None, :] + 0.01 * jax.random.normal(ks[8], (n_b, 2, B_HEADS), f32)
    b_w_out = jax.random.normal(ks[9], (n_b, BRANCH_WIDTH, D_MODEL), f32) * BRANCH_WIDTH ** -0.5
    c_w_in = jax.random.normal(ks[10], (n_c, D_MODEL, C_IN), f32) * D_MODEL ** -0.5
    c_w_out = jax.random.normal(ks[11], (n_c, BRANCH_WIDTH, D_MODEL), f32) * BRANCH_WIDTH ** -0.5
    return {"x_prompt": x_prompt, "x_sample": x_sample, "norm_g": norm_g, "final_norm_g": final_norm_g,
            "a_w_in": a_w_in, "a_sink": a_sink, "a_w_out": a_w_out,
            "b_w_in": b_w_in, "b_decay": b_decay, "b_w_out": b_w_out,
            "c_w_in": c_w_in, "c_w_out": c_w_out}


def reference(x_prompt, x_sample, norm_g, final_norm_g, a_w_in, a_sink, a_w_out,
              b_w_in, b_decay, b_w_out, c_w_in, c_w_out):
    y_prompt = trunk(x_prompt, norm_g, final_norm_g, a_w_in, a_sink, a_w_out,
                     b_w_in, b_decay, b_w_out, c_w_in, c_w_out)
    y_sample = trunk(x_sample, norm_g, final_norm_g, a_w_in, a_sink, a_w_out,
                     b_w_in, b_decay, b_w_out, c_w_in, c_w_out)
    return (y_prompt, y_sample)
```

```python
import functools
import math

import jax
import jax.numpy as jnp
from jax import lax
from jax.experimental import pallas as pl
from jax.experimental.pallas import tpu as pltpu

D_MODEL = 1024
BRANCH = 2048
NORM_EPS = 1e-6

A_HEAD = 64
A_QH = 32
A_KVH = 4
A_ROT = 16
A_THETA = 500000.0
A_WIN = 128
A_QBLK = 256
A_SPAN = A_QBLK + 2 * A_WIN
A_PAIRS = A_QH // 2

B_QK = 256
B_HEADS = 4
B_V = 512
B_CHUNK = 128
B_THETA = 10000.0

C_GROUPS = 4
C_GDIM = 512
C_RADIX = 128
C_DIRECT_MAX = 2048

LANES = 128
VMEM_LIMIT = 56 * 1024 * 1024

BF16 = jnp.bfloat16
F32 = jnp.float32


def _params(sem):
    return pltpu.CompilerParams(dimension_semantics=sem, vmem_limit_bytes=VMEM_LIMIT)


def _silu(x):
    return x * jax.nn.sigmoid(x)


def _rms(x, g):
    ms = jnp.mean(x * x, axis=-1, keepdims=True)
    return (x * lax.rsqrt(ms + NORM_EPS)) * g


IN_TM = 512
IN_TN = 512


def _inproj_a_kernel(x_ref, g_ref, w_ref, tab_ref, q_ref, gate_ref, kk_ref, vt_ref):
    h = _rms(x_ref[...], g_ref[...]).astype(BF16)

    def rope(a, t0):
        return (a * tab_ref[t0] + pltpu.roll(a, LANES - A_ROT // 2, 1) * tab_ref[t0 + 1]
                + pltpu.roll(a, A_ROT // 2, 1) * tab_ref[t0 + 2])

    for c in range(4):
        acc = jnp.dot(h, w_ref[:, c * IN_TN:(c + 1) * IN_TN], preferred_element_type=F32)
        for s in range(4):
            q_ref[4 * c + s] = rope(acc[:, s * LANES:(s + 1) * LANES], 0).astype(BF16)
    for c in range(4):
        acc = jnp.dot(h, w_ref[:, BRANCH + c * IN_TN:BRANCH + (c + 1) * IN_TN],
                      preferred_element_type=F32)
        for s in range(4):
            gate_ref[4 * c + s] = acc[:, s * LANES:(s + 1) * LANES].astype(BF16)
    acc = jnp.dot(h, w_ref[:, 2 * BRANCH:2 * BRANCH + IN_TN], preferred_element_type=F32)
    for s in range(A_KVH):
        kk_ref[s] = rope(acc[:, s * LANES:(s + 1) * LANES], 3).astype(BF16)
    acc = jnp.dot(h, w_ref[:, 2 * BRANCH + IN_TN:], preferred_element_type=F32)
    vt_ref[0] = acc.T.astype(BF16)


def _inproj_a(x2d, g, w, tabs, bsz, seq):
    t = x2d.shape[0]
    nblk = seq // IN_TM
    n = w.shape[1]
    return pl.pallas_call(
        _inproj_a_kernel,
        grid=(t // IN_TM,),
        in_specs=[
            pl.BlockSpec((IN_TM, D_MODEL), lambda i: (i, 0)),
            pl.BlockSpec((1, D_MODEL), lambda i: (0, 0)),
            pl.BlockSpec((D_MODEL, n), lambda i: (0, 0)),
            pl.BlockSpec((6, IN_TM, LANES), lambda i: (0, i % nblk, 0)),
        ],
        out_specs=[
            pl.BlockSpec((A_PAIRS, IN_TM, LANES), lambda i: (0, i, 0)),
            pl.BlockSpec((A_PAIRS, IN_TM, LANES), lambda i: (0, i, 0)),
            pl.BlockSpec((A_KVH, IN_TM, LANES), lambda i: (0, i, 0)),
            pl.BlockSpec((1, A_KVH * A_HEAD, IN_TM), lambda i: (i // nblk, 0, i % nblk)),
        ],
        out_shape=[
            jax.ShapeDtypeStruct((A_PAIRS, t, LANES), BF16),
            jax.ShapeDtypeStruct((A_PAIRS, t, LANES), BF16),
            jax.ShapeDtypeStruct((A_KVH, t, LANES), BF16),
            jax.ShapeDtypeStruct((bsz, A_KVH * A_HEAD, seq), BF16),
        ],
        compiler_params=_params(("parallel",)),
        name="inproj_a",
    )(x2d, g, w, tabs)


def _inproj_kernel(x_ref, g_ref, w_ref, *rest, rope_heads):
    if rope_heads:
        tab_ref, o_ref = rest
    else:
        (o_ref,) = rest
    h = _rms(x_ref[...], g_ref[...]).astype(BF16)
    n = w_ref.shape[1]
    for c in range(n // IN_TN):
        acc = jnp.dot(h, w_ref[:, c * IN_TN:(c + 1) * IN_TN], preferred_element_type=F32)
        for s in range(IN_TN // B_QK):
            head = c * (IN_TN // B_QK) + s
            lo = s * B_QK
            if head < rope_heads:
                t0 = 0 if head < rope_heads // 2 else 2
                cos, sin = tab_ref[t0], tab_ref[t0 + 1]
                x1 = acc[:, lo:lo + LANES]
                x2 = acc[:, lo + LANES:lo + B_QK]
                o_ref[:, c * IN_TN + lo:c * IN_TN + lo + LANES] = (x1 * cos - x2 * sin).astype(BF16)
                o_ref[:, c * IN_TN + lo + LANES:c * IN_TN + lo + B_QK] = (x2 * cos + x1 * sin).astype(BF16)
            else:
                o_ref[:, c * IN_TN + lo:c * IN_TN + lo + B_QK] = acc[:, lo:lo + B_QK].astype(BF16)


def _inproj(x2d, g, w, tabs, seq, rope_heads):
    t = x2d.shape[0]
    n = w.shape[1]
    nblk = seq // IN_TM
    in_specs = [
        pl.BlockSpec((IN_TM, D_MODEL), lambda i: (i, 0)),
        pl.BlockSpec((1, D_MODEL), lambda i: (0, 0)),
        pl.BlockSpec((D_MODEL, n), lambda i: (0, 0)),
    ]
    args = [x2d, g, w]
    if rope_heads:
        in_specs.append(pl.BlockSpec((4, IN_TM, LANES), lambda i: (0, i % nblk, 0)))
        args.append(tabs)
    return pl.pallas_call(
        functools.partial(_inproj_kernel, rope_heads=rope_heads),
        grid=(t // IN_TM,),
        in_specs=in_specs,
        out_specs=pl.BlockSpec((IN_TM, n), lambda i: (i, 0)),
        out_shape=jax.ShapeDtypeStruct((t, n), BF16),
        compiler_params=_params(("parallel",)),
        name="inproj",
    )(*args)


def _attn_kernel(sink_ref, q_ref, gate_ref, kp_ref, kc_ref, kn_ref, vp_ref, vc_ref, vn_ref,
                 x_ref, w_ref, fg_ref, o_ref, ka_ref, kb_ref, v_ref, bias_ref, g_ref, *, final):
    i = pl.program_id(1)
    last = pl.num_programs(1) - 1
    rows = A_SPAN - A_WIN

    lane = lax.broadcasted_iota(jnp.int32, (A_SPAN, LANES), 1)
    for h in range(A_KVH):
        span = jnp.concatenate([kp_ref[h], kc_ref[h], kn_ref[h]], axis=0)
        ka_ref[h] = jnp.where(lane < A_HEAD, span, jnp.zeros_like(span))
        kb_ref[h] = jnp.where(lane >= A_HEAD, span, jnp.zeros_like(span))
        v_ref[h] = jnp.concatenate([vp_ref[0, h], vc_ref[0, h], vn_ref[0, h]], axis=1)

    r = lax.broadcasted_iota(jnp.int32, (rows, LANES), 0)
    c = lax.broadcasted_iota(jnp.int32, (rows, LANES), 1)
    band = (r - c >= 0) & (r - c <= 2 * A_WIN)
    lo = jnp.where(i == 0, A_WIN, 0)
    hi = jnp.where(i == last, rows - A_WIN, rows)
    bias_ref[0] = jnp.where(band & (r >= lo), 0.0, -1e30).astype(F32)
    bias_ref[1] = jnp.where(band & (r < hi), 0.0, -1e30).astype(F32)

    def pair(j, carry):
        h = j // (A_PAIRS // A_KVH)
        qp = q_ref[j]
        vh = v_ref[h]
        outs = []
        for a, k_ref in enumerate((ka_ref, kb_ref)):
            s_t = lax.dot_general(k_ref[h], qp, (((1,), (1,)), ((), ())),
                                  preferred_element_type=F32)
            sink = sink_ref[2 * j + a]
            halves = []
            for jj in range(2):
                s = s_t[jj * A_WIN:jj * A_WIN + rows, jj * LANES:(jj + 1) * LANES] + bias_ref[jj]
                mx = jnp.maximum(jnp.max(s, axis=0, keepdims=True), sink)
                p = jnp.exp(s - mx)
                den = jnp.sum(p, axis=0, keepdims=True) + jnp.exp(sink - mx)
                o_t = jnp.dot(vh[:, jj * A_WIN:jj * A_WIN + rows], p.astype(BF16),
                              preferred_element_type=F32)
                halves.append(o_t / den)
            outs.append(jnp.concatenate(halves, axis=1))
        o_pair = jnp.concatenate(outs, axis=0).T
        gt = gate_ref[j].astype(F32)
        g_ref[j] = (o_pair * _silu(gt)).astype(BF16)
        return carry

    lax.fori_loop(0, A_PAIRS, pair, 0)

    gated = jnp.concatenate([g_ref[j] for j in range(A_PAIRS)], axis=1)
    y = x_ref[...] + jnp.dot(gated, w_ref[...], preferred_element_type=F32)
    if final:
        y = _rms(y, fg_ref[...])
    o_ref[...] = y


def _attention(x2d, q3, gate3, kk, vt, sink, w_out, fin_g, bsz, seq, final):
    t = x2d.shape[0]
    nqb = seq // A_QBLK
    nkb = seq // A_WIN
    vt4 = vt.reshape(bsz, A_KVH, A_HEAD, seq)
    grid_spec = pltpu.PrefetchScalarGridSpec(
        num_scalar_prefetch=1,
        grid=(bsz, nqb),
        in_specs=[
            pl.BlockSpec((A_PAIRS, A_QBLK, LANES), lambda b, i, s: (0, b * nqb + i, 0)),
            pl.BlockSpec((A_PAIRS, A_QBLK, LANES), lambda b, i, s: (0, b * nqb + i, 0)),
            pl.BlockSpec((A_KVH, A_WIN, LANES),
                         lambda b, i, s: (0, b * nkb + jnp.maximum(2 * i - 1, 0), 0)),
            pl.BlockSpec((A_KVH, A_QBLK, LANES), lambda b, i, s: (0, b * nqb + i, 0)),
            pl.BlockSpec((A_KVH, A_WIN, LANES),
                         lambda b, i, s: (0, b * nkb + jnp.minimum(2 * i + 2, nkb - 1), 0)),
            pl.BlockSpec((1, A_KVH, A_HEAD, A_WIN),
                         lambda b, i, s: (b, 0, 0, jnp.maximum(2 * i - 1, 0))),
            pl.BlockSpec((1, A_KVH, A_HEAD, A_QBLK), lambda b, i, s: (b, 0, 0, i)),
            pl.BlockSpec((1, A_KVH, A_HEAD, A_WIN),
                         lambda b, i, s: (b, 0, 0, jnp.minimum(2 * i + 2, nkb - 1))),
            pl.BlockSpec((A_QBLK, D_MODEL), lambda b, i, s: (b * nqb + i, 0)),
            pl.BlockSpec((BRANCH, D_MODEL), lambda b, i, s: (0, 0)),
            pl.BlockSpec((1, D_MODEL), lambda b, i, s: (0, 0)),
        ],
        out_specs=pl.BlockSpec((A_QBLK, D_MODEL), lambda b, i, s: (b * nqb + i, 0)),
        scratch_shapes=[
            pltpu.VMEM((A_KVH, A_SPAN, LANES), BF16),
            pltpu.VMEM((A_KVH, A_SPAN, LANES), BF16),
            pltpu.VMEM((A_KVH, A_HEAD, A_SPAN), BF16),
            pltpu.VMEM((2, A_SPAN - A_WIN, LANES), F32),
            pltpu.VMEM((A_PAIRS, A_QBLK, LANES), BF16),
        ],
    )
    return pl.pallas_call(
        functools.partial(_attn_kernel, final=final),
        grid_spec=grid_spec,
        out_shape=jax.ShapeDtypeStruct((t, D_MODEL), F32),
        compiler_params=_params(("parallel", "parallel")),
        name="attention",
    )(sink, q3, gate3, kk, kk, kk, vt4, vt4, vt4, x2d, w_out, fin_g)


def _retention_kernel(lg_ref, qf_ref, kf_ref, vf_ref, qb_ref, kb_ref, vb_ref,
                      of_ref, ob_ref, st_ref):
    t = pl.program_id(1)

    @pl.when(t == 0)
    def _():
        st_ref[...] = jnp.zeros_like(st_ref)

    n = B_CHUNK
    ii = lax.broadcasted_iota(jnp.int32, (n, n), 0)
    jj = lax.broadcasted_iota(jnp.int32, (n, n), 1)
    col = lax.broadcasted_iota(jnp.int32, (n, 1), 0).astype(F32)
    refs = ((qf_ref, kf_ref, vf_ref, of_ref), (qb_ref, kb_ref, vb_ref, ob_ref))
    for d, (q_ref, k_ref, v_ref, o_ref) in enumerate(refs):
        for h in range(B_HEADS):
            lg = lg_ref[d, h]
            if d == 0:
                mask = ii >= jj
                dist = (ii - jj).astype(F32)
                q_dec = jnp.exp((col + 1.0) * lg)
                k_dec = jnp.exp((n - 1.0 - col) * lg)
            else:
                mask = jj > ii
                dist = (jj - ii).astype(F32)
                q_dec = jnp.exp((n - col) * lg)
                k_dec = jnp.exp(col * lg)
            intra = jnp.where(mask, jnp.exp(jnp.where(mask, dist, 0.0) * lg), 0.0)
            c_dec = jnp.exp(jnp.full((1, B_V), n * lg, F32))

            q = q_ref[0, :, h * B_QK:(h + 1) * B_QK]
            k = k_ref[0, :, h * B_QK:(h + 1) * B_QK]
            v = v_ref[0, :, h * B_V:(h + 1) * B_V]
            st = st_ref[d, h]
            sc = lax.dot_general(q, k, (((1,), (1,)), ((), ())), preferred_element_type=F32)
            inner = jnp.dot((sc * intra).astype(BF16), v, preferred_element_type=F32)
            cross = jnp.dot(q, st.astype(BF16), preferred_element_type=F32) * q_dec
            o_ref[0, :, h * B_V:(h + 1) * B_V] = inner + cross
            kd_t = (k.astype(F32) * k_dec).T.astype(BF16)
            st_ref[d, h] = st * c_dec + jnp.dot(kd_t, v, preferred_element_type=F32)


def _retention(proj3, log_g):
    bsz, seq, _ = proj3.shape
    nc = seq // B_CHUNK
    nqk = B_HEADS * B_QK
    grid_spec = pltpu.PrefetchScalarGridSpec(
        num_scalar_prefetch=0,
        grid=(bsz, nc),
        in_specs=[
            pl.BlockSpec(memory_space=pltpu.SMEM),
            pl.BlockSpec((1, B_CHUNK, nqk), lambda b, t: (b, t, 0)),
            pl.BlockSpec((1, B_CHUNK, nqk), lambda b, t: (b, t, 1)),
            pl.BlockSpec((1, B_CHUNK, BRANCH), lambda b, t: (b, t, 1)),
            pl.BlockSpec((1, B_CHUNK, nqk), lambda b, t: (b, nc - 1 - t, 0)),
            pl.BlockSpec((1, B_CHUNK, nqk), lambda b, t: (b, nc - 1 - t, 1)),
            pl.BlockSpec((1, B_CHUNK, BRANCH), lambda b, t: (b, nc - 1 - t, 1)),
        ],
        out_specs=[
            pl.BlockSpec((1, B_CHUNK, BRANCH), lambda b, t: (b, t, 0)),
            pl.BlockSpec((1, B_CHUNK, BRANCH), lambda b, t: (b, nc - 1 - t, 0)),
        ],
        scratch_shapes=[pltpu.VMEM((2, B_HEADS, B_QK, B_V), F32)],
    )
    return pl.pallas_call(
        _retention_kernel,
        grid_spec=grid_spec,
        out_shape=[jax.ShapeDtypeStruct((bsz, seq, BRANCH), F32)] * 2,
        compiler_params=_params(("parallel", "arbitrary")),
        name="retention",
    )(log_g, proj3, proj3, proj3, proj3, proj3, proj3)


B_TM = 256


def _retention_out_kernel(of_ref, ob_ref, gate_ref, x_ref, w_ref, o_ref):
    o = of_ref[...] + ob_ref[...]
    parts = []
    for h in range(B_HEADS):
        oh = o[:, h * B_V:(h + 1) * B_V]
        ms = jnp.mean(oh * oh, axis=-1, keepdims=True)
        on = oh * lax.rsqrt(ms + NORM_EPS)
        gt = gate_ref[:, h * B_V:(h + 1) * B_V].astype(F32)
        parts.append((on * _silu(gt)).astype(BF16))
    gated = jnp.concatenate(parts, axis=1)
    o_ref[...] = x_ref[...] + jnp.dot(gated, w_ref[...], preferred_element_type=F32)


def _retention_out(o_f, o_b, proj2d, x2d, w_out):
    t = x2d.shape[0]
    return pl.pallas_call(
        _retention_out_kernel,
        grid=(t // B_TM,),
        in_specs=[
            pl.BlockSpec((B_TM, BRANCH), lambda i: (i, 0)),
            pl.BlockSpec((B_TM, BRANCH), lambda i: (i, 0)),
            pl.BlockSpec((B_TM, BRANCH), lambda i: (i, 2)),
            pl.BlockSpec((B_TM, D_MODEL), lambda i: (i, 0)),
            pl.BlockSpec((BRANCH, D_MODEL), lambda i: (0, 0)),
        ],
        out_specs=pl.BlockSpec((B_TM, D_MODEL), lambda i: (i, 0)),
        out_shape=jax.ShapeDtypeStruct((t, D_MODEL), F32),
        compiler_params=_params(("parallel",)),
        name="retention_out",
    )(o_f, o_b, proj2d, x2d, w_out)


def _fold_kernel(w_ref, cs_ref, a_ref, b_ref):
    r = jnp.dot(w_ref[...], cs_ref[...], preferred_element_type=F32,
                precision=lax.Precision.HIGHEST)
    a_ref[...] = r[:, :C_GDIM].astype(BF16)
    b_ref[...] = r[:, C_GDIM:].astype(BF16)


def _fold_channel_dft(w_u, cs):
    spec = pl.BlockSpec((D_MODEL, C_GDIM), lambda g: (0, g))
    return pl.pallas_call(
        _fold_kernel,
        grid=(C_GROUPS,),
        in_specs=[spec, pl.BlockSpec((C_GDIM, 2 * C_GDIM), lambda g: (0, 0))],
        out_specs=[spec, spec],
        out_shape=[jax.ShapeDtypeStruct((D_MODEL, BRANCH), BF16)] * 2,
        compiler_params=_params(("parallel",)),
        name="fold_channel_dft",
    )(w_u, cs)


DFT_ROWS = 512


def _dft_mix_kernel(m1_ref, m2_ref, p_ref, q_ref, gate_ref, o_ref):
    n = m1_ref.shape[0]
    rc = min(n, DFT_ROWS)
    p = p_ref[0]
    q = q_ref[0]
    for r in range(n // rc):
        rows = slice(r * rc, (r + 1) * rc)
        acc = jnp.dot(m1_ref[rows, :], p, preferred_element_type=F32)
        acc = acc + jnp.dot(m2_ref[rows, :], q, preferred_element_type=F32)
        gt = gate_ref[0, rows, :].astype(F32)
        o_ref[0, rows, :] = (acc * _silu(gt)).astype(BF16)


def _dft_mix(m1, m2, src_p, src_gate, out_shape, n, tc, p_idx, q_idx, g_idx, o_idx, grid):
    mspec = pl.BlockSpec((n, n), lambda b, c: (0, 0))

    def dspec(idx):
        return pl.BlockSpec((1, n, tc), lambda b, c: (b, 0, idx(c)))

    return pl.pallas_call(
        _dft_mix_kernel,
        grid=grid,
        in_specs=[mspec, mspec, dspec(p_idx), dspec(q_idx), dspec(g_idx)],
        out_specs=dspec(o_idx),
        out_shape=jax.ShapeDtypeStruct(out_shape, BF16),
        compiler_params=_params(("parallel", "parallel")),
        name="dft_mix",
    )(m1, m2, src_p, src_p, src_gate)


def _dft_stage1_kernel(d_ref, a_ref, b_ref, y_ref):
    dc = d_ref[0, :C_RADIX, :]
    ds = d_ref[0, C_RADIX:, :]
    a = a_ref[0]
    b = b_ref[0]
    yr = jnp.dot(dc, a, preferred_element_type=F32) - jnp.dot(ds, b, preferred_element_type=F32)
    yi = jnp.dot(dc, b, preferred_element_type=F32) + jnp.dot(ds, a, preferred_element_type=F32)
    y_ref[0, 0, :, :BRANCH] = yr.astype(BF16)
    y_ref[0, 0, :, BRANCH:] = (-yi).astype(BF16)


def _dft_stage1(projv, dtab, bsz, n1):
    return pl.pallas_call(
        _dft_stage1_kernel,
        grid=(bsz, n1),
        in_specs=[
            pl.BlockSpec((1, 2 * C_RADIX, C_RADIX), lambda b, m: (m, 0, 0)),
            pl.BlockSpec((1, C_RADIX, BRANCH), lambda b, m: (b, 0, 3 * m)),
            pl.BlockSpec((1, C_RADIX, BRANCH), lambda b, m: (b, 0, 3 * m + 1)),
        ],
        out_specs=pl.BlockSpec((1, 1, C_RADIX, 2 * BRANCH), lambda b, m: (b, m, 0, 0)),
        out_shape=jax.ShapeDtypeStruct((bsz, n1, C_RADIX, 2 * BRANCH), BF16),
        compiler_params=_params(("parallel", "parallel")),
        name="dft_stage1",
    )(dtab, projv, projv)


def _trig(m, period, scale):
    ang = (2.0 * math.pi / period) * (m % period).astype(F32)
    return jnp.cos(ang) * scale, jnp.sin(ang) * scale


def _fourier_mix(proj3):
    bsz, seq, _ = proj3.shape
    if seq <= C_DIRECT_MAX:
        idx = jnp.arange(seq, dtype=jnp.int32)
        cm, sm = _trig(idx[:, None] * idx[None, :], seq, seq ** -0.5)
        tc = 512
        nct = BRANCH // tc
        return _dft_mix(cm.astype(BF16), (-sm).astype(BF16), proj3, proj3, (bsz, seq, BRANCH),
                        seq, tc, lambda c: c, lambda c: nct + c, lambda c: 2 * nct + c,
                        lambda c: c, (bsz, nct))
    n2 = C_RADIX
    n1 = seq // n2
    k2 = jnp.arange(n2, dtype=jnp.int32)
    tok = jnp.arange(n1, dtype=jnp.int32)[:, None, None] + n1 * jnp.arange(n2, dtype=jnp.int32)[None, None, :]
    dc, ds = _trig(k2[None, :, None] * tok, seq, n2 ** -0.5)
    dtab = jnp.concatenate([dc, ds], axis=1).astype(BF16)
    y = _dft_stage1(proj3.reshape(bsz, n2, n1 * 3 * BRANCH), dtab, bsz, n1)
    i1 = jnp.arange(n1, dtype=jnp.int32)
    c1, s1 = _trig(i1[:, None] * i1[None, :], n1, n1 ** -0.5)
    out = _dft_mix(c1.astype(BF16), s1.astype(BF16), y.reshape(bsz, n1, n2 * 2 * BRANCH),
                   proj3.reshape(bsz, n1, n2 * 3 * BRANCH), (bsz, n1, n2 * BRANCH),
                   n1, BRANCH, lambda c: 2 * c, lambda c: 2 * c + 1, lambda c: 3 * c + 2,
                   lambda c: c, (bsz, n2))
    return out.reshape(bsz, seq, BRANCH)


OUT_TM = 512


def _outproj_kernel(m_ref, x_ref, w_ref, o_ref):
    o_ref[...] = x_ref[...] + jnp.dot(m_ref[...], w_ref[...], preferred_element_type=F32)


def _outproj(mixed2d, x2d, w_out):
    t = x2d.shape[0]
    return pl.pallas_call(
        _outproj_kernel,
        grid=(t // OUT_TM,),
        in_specs=[
            pl.BlockSpec((OUT_TM, BRANCH), lambda i: (i, 0)),
            pl.BlockSpec((OUT_TM, D_MODEL), lambda i: (i, 0)),
            pl.BlockSpec((BRANCH, D_MODEL), lambda i: (0, 0)),
        ],
        out_specs=pl.BlockSpec((OUT_TM, D_MODEL), lambda i: (i, 0)),
        out_shape=jax.ShapeDtypeStruct((t, D_MODEL), F32),
        compiler_params=_params(("parallel",)),
        name="outproj",
    )(mixed2d, x2d, w_out)


def _rope_tables_a(seq):
    half = A_ROT // 2
    inv_freq = jnp.exp(-(jnp.arange(half, dtype=F32) * (2.0 / A_ROT)) * math.log(A_THETA))
    ang = jnp.arange(seq, dtype=F32)[:, None] * inv_freq[None, :]
    cos, sin = jnp.cos(ang), jnp.sin(ang)
    d = jnp.arange(LANES) % A_HEAD
    f = d % half
    cm = jnp.where(d[None, :] < A_ROT, cos[:, f], 1.0)
    s1 = jnp.where(d[None, :] < half, -sin[:, f], 0.0)
    s2 = jnp.where((d[None, :] >= half) & (d[None, :] < A_ROT), sin[:, f], 0.0)
    qs = A_HEAD ** -0.5
    return jnp.stack([cm * qs, s1 * qs, s2 * qs, cm, s1, s2]).astype(F32)


def _rope_tables_b(seq):
    half = B_QK // 2
    inv_freq = jnp.exp(-(jnp.arange(half, dtype=F32) * (2.0 / B_QK)) * math.log(B_THETA))
    ang = jnp.arange(seq, dtype=F32)[:, None] * inv_freq[None, :]
    cos, sin = jnp.cos(ang), jnp.sin(ang)
    ks = B_QK ** -0.5
    return jnp.stack([cos, sin, cos * ks, sin * ks]).astype(F32)


def _weights_a(w_in):
    nq = A_QH * A_HEAD
    nkv = A_KVH * A_HEAD
    q = w_in[:, :nq]
    k = w_in[:, nq:nq + nkv].reshape(D_MODEL, A_KVH, 1, A_HEAD)
    v = w_in[:, nq + nkv:nq + 2 * nkv]
    gate = w_in[:, nq + 2 * nkv:]
    kk = jnp.broadcast_to(k, (D_MODEL, A_KVH, 2, A_HEAD)).reshape(D_MODEL, 2 * nkv)
    return jnp.concatenate([q, gate, kk, v], axis=1).astype(BF16)


def _weights_c(w_in):
    m = jnp.arange(C_GDIM, dtype=jnp.int32)
    cm, sm = _trig(m[:, None] * m[None, :], C_GDIM, C_GDIM ** -0.5)
    w_a, w_b = _fold_channel_dft(w_in[:, :BRANCH], jnp.concatenate([cm, sm], axis=1))
    return jnp.concatenate([w_a, w_b, w_in[:, BRANCH:].astype(BF16)], axis=1)


def _trunk(x, norm_g, fin_g, wa, a_sink, a_w_out, wb, log_g, b_w_out, wc, c_w_out):
    bsz, seq, _ = x.shape
    t = bsz * seq
    x2d = x.reshape(t, D_MODEL)
    tabs_a = _rope_tables_a(seq)
    tabs_b = _rope_tables_b(seq)
    fin = fin_g.reshape(1, D_MODEL)

    def layer_a(x2d, layer, j, final):
        q3, gate3, kk, vt = _inproj_a(x2d, norm_g[layer].reshape(1, D_MODEL), wa[j], tabs_a, bsz, seq)
        return _attention(x2d, q3, gate3, kk, vt, a_sink[j], a_w_out[j], fin, bsz, seq, final)

    x2d = layer_a(x2d, 0, 0, False)

    proj = _inproj(x2d, norm_g[1].reshape(1, D_MODEL), wb, tabs_b, seq, 2 * B_HEADS)
    o_f, o_b = _retention(proj.reshape(bsz, seq, -1), log_g)
    x2d = _retention_out(o_f.reshape(t, BRANCH), o_b.reshape(t, BRANCH), proj, x2d, b_w_out)

    proj = _inproj(x2d, norm_g[2].reshape(1, D_MODEL), wc, None, seq, 0)
    mixed = _fourier_mix(proj.reshape(bsz, seq, -1))
    x2d = _outproj(mixed.reshape(t, BRANCH), x2d, c_w_out)

    x2d = layer_a(x2d, 3, 1, True)
    return x2d.reshape(bsz, seq, D_MODEL)


def kernel(x_prompt, x_sample, norm_g, final_norm_g, a_w_in, a_sink, a_w_out, b_w_in, b_decay,
           b_w_out, c_w_in, c_w_out):
    wa = [_weights_a(a_w_in[j]) for j in range(a_w_in.shape[0])]
    a_out = [a_w_out[j].astype(BF16) for j in range(a_w_out.shape[0])]
    wb = b_w_in[0].astype(BF16)
    log_g = jax.nn.log_sigmoid(b_decay[0].astype(F32))
    wc = _weights_c(c_w_in[0])
    args = (norm_g, final_norm_g, wa, a_sink, a_out, wb, log_g, b_w_out[0].astype(BF16),
            wc, c_w_out[0].astype(BF16))
    return (_trunk(x_prompt, *args), _trunk(x_sample, *args))
```

```python
import functools
import math

import jax
import jax.numpy as jnp
from jax import lax
from jax.experimental import pallas as pl
from jax.experimental.pallas import tpu as pltpu

D_MODEL = 1024
BRANCH = 2048
NORM_EPS = 1e-6

A_HEAD = 64
A_QH = 32
A_KVH = 4
A_ROT = 16
A_THETA = 500000.0
A_WIN = 128
A_QBLK = 256
A_SPAN = A_QBLK + 2 * A_WIN
A_PAIRS = A_QH // 2
A_LOOKAHEAD = 3
A_VPAD = 16
LOG2E = math.log2(math.e)

B_QK = 256
B_HEADS = 4
B_V = 512
B_CHUNK = 256
B_THETA = 10000.0

C_GROUPS = 4
C_GDIM = 512
C_RADIX = 128
C_DIRECT_MAX = 2048

LANES = 128
VMEM_LIMIT = 56 * 1024 * 1024

BF16 = jnp.bfloat16
F32 = jnp.float32


def _params(sem):
    return pltpu.CompilerParams(dimension_semantics=sem, vmem_limit_bytes=VMEM_LIMIT)


def _silu(x):
    return x * jax.nn.sigmoid(x)


def _rms(x, g):
    ms = jnp.mean(x * x, axis=-1, keepdims=True)
    return (x * lax.rsqrt(ms + NORM_EPS)) * g


IN_TM = 512
IN_TN = 512


def _inproj_a_kernel(x_ref, g_ref, w_ref, tab_ref, q_ref, gate_ref, kk_ref, vt_ref):
    h = _rms(x_ref[...], g_ref[...]).astype(BF16)

    def rope(a, t0):
        return (a * tab_ref[t0] + pltpu.roll(a, LANES - A_ROT // 2, 1) * tab_ref[t0 + 1]
                + pltpu.roll(a, A_ROT // 2, 1) * tab_ref[t0 + 2])

    for c in range(4):
        acc = jnp.dot(h, w_ref[:, c * IN_TN:(c + 1) * IN_TN], preferred_element_type=F32)
        for s in range(4):
            q_ref[4 * c + s] = rope(acc[:, s * LANES:(s + 1) * LANES], 0).astype(BF16)
    for c in range(4):
        acc = jnp.dot(h, w_ref[:, BRANCH + c * IN_TN:BRANCH + (c + 1) * IN_TN],
                      preferred_element_type=F32)
        for s in range(4):
            gate_ref[4 * c + s] = acc[:, s * LANES:(s + 1) * LANES].astype(BF16)
    acc = jnp.dot(h, w_ref[:, 2 * BRANCH:2 * BRANCH + IN_TN], preferred_element_type=F32)
    for s in range(A_KVH):
        kk_ref[s] = rope(acc[:, s * LANES:(s + 1) * LANES], 3).astype(BF16)
    acc = jnp.dot(h, w_ref[:, 2 * BRANCH + IN_TN:], preferred_element_type=F32)
    vt_ref[0] = acc.T.astype(BF16)


def _inproj_a(x2d, g, w, tabs, bsz, seq):
    t = x2d.shape[0]
    nblk = seq // IN_TM
    n = w.shape[1]
    return pl.pallas_call(
        _inproj_a_kernel,
        grid=(t // IN_TM,),
        in_specs=[
            pl.BlockSpec((IN_TM, D_MODEL), lambda i: (i, 0)),
            pl.BlockSpec((1, D_MODEL), lambda i: (0, 0)),
            pl.BlockSpec((D_MODEL, n), lambda i: (0, 0)),
            pl.BlockSpec((6, IN_TM, LANES), lambda i: (0, i % nblk, 0)),
        ],
        out_specs=[
            pl.BlockSpec((A_PAIRS, IN_TM, LANES), lambda i: (0, i, 0)),
            pl.BlockSpec((A_PAIRS, IN_TM, LANES), lambda i: (0, i, 0)),
            pl.BlockSpec((A_KVH, IN_TM, LANES), lambda i: (0, i, 0)),
            pl.BlockSpec((1, A_KVH * A_HEAD, IN_TM), lambda i: (i // nblk, 0, i % nblk)),
        ],
        out_shape=[
            jax.ShapeDtypeStruct((A_PAIRS, t, LANES), BF16),
            jax.ShapeDtypeStruct((A_PAIRS, t, LANES), BF16),
            jax.ShapeDtypeStruct((A_KVH, t, LANES), BF16),
            jax.ShapeDtypeStruct((bsz, A_KVH * A_HEAD, seq), BF16),
        ],
        compiler_params=_params(("parallel",)),
        name="inproj_a",
    )(x2d, g, w, tabs)


def _inproj_kernel(x_ref, g_ref, w_ref, *rest, rope_heads):
    if rope_heads:
        tab_ref, o_ref = rest
    else:
        (o_ref,) = rest
    h = _rms(x_ref[...], g_ref[...]).astype(BF16)
    n = w_ref.shape[1]
    for c in range(n // IN_TN):
        acc = jnp.dot(h, w_ref[:, c * IN_TN:(c + 1) * IN_TN], preferred_element_type=F32)
        for s in range(IN_TN // B_QK):
            head = c * (IN_TN // B_QK) + s
            lo = s * B_QK
            if head < rope_heads:
                t0 = 0 if head < rope_heads // 2 else 2
                cos, sin = tab_ref[t0], tab_ref[t0 + 1]
                x1 = acc[:, lo:lo + LANES]
                x2 = acc[:, lo + LANES:lo + B_QK]
                o_ref[:, c * IN_TN + lo:c * IN_TN + lo + LANES] = (x1 * cos - x2 * sin).astype(BF16)
                o_ref[:, c * IN_TN + lo + LANES:c * IN_TN + lo + B_QK] = (x2 * cos + x1 * sin).astype(BF16)
            else:
                o_ref[:, c * IN_TN + lo:c * IN_TN + lo + B_QK] = acc[:, lo:lo + B_QK].astype(BF16)


def _inproj(x2d, g, w, tabs, seq, rope_heads):
    t = x2d.shape[0]
    n = w.shape[1]
    nblk = seq // IN_TM
    in_specs = [
        pl.BlockSpec((IN_TM, D_MODEL), lambda i: (i, 0)),
        pl.BlockSpec((1, D_MODEL), lambda i: (0, 0)),
        pl.BlockSpec((D_MODEL, n), lambda i: (0, 0)),
    ]
    args = [x2d, g, w]
    if rope_heads:
        in_specs.append(pl.BlockSpec((4, IN_TM, LANES), lambda i: (0, i % nblk, 0)))
        args.append(tabs)
    return pl.pallas_call(
        functools.partial(_inproj_kernel, rope_heads=rope_heads),
        grid=(t // IN_TM,),
        in_specs=in_specs,
        out_specs=pl.BlockSpec((IN_TM, n), lambda i: (i, 0)),
        out_shape=jax.ShapeDtypeStruct((t, n), BF16),
        compiler_params=_params(("parallel",)),
        name="inproj",
    )(*args)


def _attn_kernel(sink_ref, q_ref, gate_ref, kp_ref, kc_ref, kn_ref, vp_ref, vc_ref, vn_ref,
                 x_ref, w_ref, fg_ref, o_ref, ka_ref, kb_ref, v_ref, bias_ref, *, final):
    i = pl.program_id(1)
    last = pl.num_programs(1) - 1
    rows = A_SPAN - A_WIN
    per_kv = A_PAIRS // A_KVH

    lane = lax.broadcasted_iota(jnp.int32, (A_SPAN, LANES), 1)
    ones_row = (lax.broadcasted_iota(jnp.int32, (A_VPAD, A_SPAN), 0) == 0).astype(BF16)
    for h in range(A_KVH):
        span = jnp.concatenate([kp_ref[h], kc_ref[h], kn_ref[h]], axis=0)
        ka_ref[h] = jnp.where(lane < A_HEAD, span, jnp.zeros_like(span))
        kb_ref[h] = jnp.where(lane >= A_HEAD, span, jnp.zeros_like(span))
        v_ref[h, :A_HEAD, :] = jnp.concatenate([vp_ref[0, h], vc_ref[0, h], vn_ref[0, h]], axis=1)
        v_ref[h, A_HEAD:, :] = ones_row

    r = lax.broadcasted_iota(jnp.int32, (A_WIN, LANES), 0)
    c = lax.broadcasted_iota(jnp.int32, (A_WIN, LANES), 1)
    neg = jnp.full((A_WIN, LANES), -1e30, F32)
    top = jnp.where(r >= c, 0.0, neg)
    bot = jnp.where(r <= c, 0.0, neg)
    bias_ref[0] = jnp.where(i == 0, neg, top)
    bias_ref[1] = bot
    bias_ref[2] = top
    bias_ref[3] = jnp.where(i == last, neg, bot)
    pad = jnp.zeros((A_WIN, LANES), BF16)

    units = [(g, jl, a) for g in range(A_KVH) for jl in range(per_kv) for a in range(2)]

    def scores(n):
        g, jl, a = units[n]
        k = (ka_ref, kb_ref)[a][g]
        return lax.dot_general(k, q_ref[g * per_kv + jl], (((1,), (1,)), ((), ())),
                               preferred_element_type=F32)

    def softmax_pv(s_t, sink, g):
        ps, extra = [], []
        for jj in range(2):
            blk = s_t[jj * A_WIN:jj * A_WIN + rows, jj * LANES:(jj + 1) * LANES]
            parts = [blk[:A_WIN] + bias_ref[2 * jj], blk[A_WIN:2 * A_WIN],
                     blk[2 * A_WIN:] + bias_ref[2 * jj + 1]]
            mx = jnp.full((1, LANES), sink, F32)
            for part in parts:
                mx = jnp.maximum(mx, jnp.max(part, axis=0, keepdims=True))
            es = [jnp.exp2(part - mx).astype(BF16) for part in parts]
            ps.append(jnp.concatenate(es + [pad] if jj == 0 else [pad] + es, axis=0))
            extra.append(jnp.exp2(sink - mx))
        o_ext = jnp.dot(v_ref[g], jnp.concatenate(ps, axis=1), preferred_element_type=F32)
        den = o_ext[A_HEAD:A_HEAD + 1, :] + jnp.concatenate(extra, axis=1)
        return o_ext[:A_HEAD, :] / den

    pending = [scores(n) for n in range(A_LOOKAHEAD)]
    outs, gated = [], []
    y = x_ref[...]
    for n, (g, jl, a) in enumerate(units):
        if n + A_LOOKAHEAD < len(units):
            pending.append(scores(n + A_LOOKAHEAD))
        j = g * per_kv + jl
        outs.append(softmax_pv(pending.pop(0), sink_ref[2 * j + a] * LOG2E, g))
        if a == 1:
            o_pair = jnp.concatenate(outs[-2:], axis=0).T
            gt = gate_ref[j].astype(F32)
            gated.append((o_pair * _silu(gt)).astype(BF16))
            if jl == per_kv - 1:
                wg = w_ref[g * per_kv * LANES:(g + 1) * per_kv * LANES, :]
                y = y + jnp.dot(jnp.concatenate(gated[-per_kv:], axis=1), wg,
                                preferred_element_type=F32)
    if final:
        y = _rms(y, fg_ref[...])
    o_ref[...] = y


def _attention(x2d, q3, gate3, kk, vt, sink, w_out, fin_g, bsz, seq, final):
    t = x2d.shape[0]
    nqb = seq // A_QBLK
    nkb = seq // A_WIN
    vt4 = vt.reshape(bsz, A_KVH, A_HEAD, seq)
    grid_spec = pltpu.PrefetchScalarGridSpec(
        num_scalar_prefetch=1,
        grid=(bsz, nqb),
        in_specs=[
            pl.BlockSpec((A_PAIRS, A_QBLK, LANES), lambda b, i, s: (0, b * nqb + i, 0)),
            pl.BlockSpec((A_PAIRS, A_QBLK, LANES), lambda b, i, s: (0, b * nqb + i, 0)),
            pl.BlockSpec((A_KVH, A_WIN, LANES),
                         lambda b, i, s: (0, b * nkb + jnp.maximum(2 * i - 1, 0), 0)),
            pl.BlockSpec((A_KVH, A_QBLK, LANES), lambda b, i, s: (0, b * nqb + i, 0)),
            pl.BlockSpec((A_KVH, A_WIN, LANES),
                         lambda b, i, s: (0, b * nkb + jnp.minimum(2 * i + 2, nkb - 1), 0)),
            pl.BlockSpec((1, A_KVH, A_HEAD, A_WIN),
                         lambda b, i, s: (b, 0, 0, jnp.maximum(2 * i - 1, 0))),
            pl.BlockSpec((1, A_KVH, A_HEAD, A_QBLK), lambda b, i, s: (b, 0, 0, i)),
            pl.BlockSpec((1, A_KVH, A_HEAD, A_WIN),
                         lambda b, i, s: (b, 0, 0, jnp.minimum(2 * i + 2, nkb - 1))),
            pl.BlockSpec((A_QBLK, D_MODEL), lambda b, i, s: (b * nqb + i, 0)),
            pl.BlockSpec((BRANCH, D_MODEL), lambda b, i, s: (0, 0)),
            pl.BlockSpec((1, D_MODEL), lambda b, i, s: (0, 0)),
        ],
        out_specs=pl.BlockSpec((A_QBLK, D_MODEL), lambda b, i, s: (b * nqb + i, 0)),
        scratch_shapes=[
            pltpu.VMEM((A_KVH, A_SPAN, LANES), BF16),
            pltpu.VMEM((A_KVH, A_SPAN, LANES), BF16),
            pltpu.VMEM((A_KVH, A_HEAD + A_VPAD, A_SPAN), BF16),
            pltpu.VMEM((4, A_WIN, LANES), F32),
        ],
    )
    return pl.pallas_call(
        functools.partial(_attn_kernel, final=final),
        grid_spec=grid_spec,
        out_shape=jax.ShapeDtypeStruct((t, D_MODEL), F32),
        compiler_params=_params(("parallel", "parallel")),
        name="attention",
    )(sink, q3, gate3, kk, kk, kk, vt4, vt4, vt4, x2d, w_out, fin_g)


def _retention_heads(lg_ref, q_ref, k_ref, v_ref, st_ref, backward):
    n = B_CHUNK
    ii = lax.broadcasted_iota(jnp.int32, (n, n), 0)
    jj = lax.broadcasted_iota(jnp.int32, (n, n), 1)
    col = lax.broadcasted_iota(jnp.int32, (n, 1), 0).astype(F32)
    d = 1 if backward else 0
    heads = range(B_HEADS)
    qs = [q_ref[0, :, h * B_QK:(h + 1) * B_QK] for h in heads]
    ks = [k_ref[0, :, h * B_QK:(h + 1) * B_QK] for h in heads]
    vs = [v_ref[0, :, h * B_V:(h + 1) * B_V] for h in heads]
    sc = [lax.dot_general(qs[h], ks[h], (((1,), (1,)), ((), ())), preferred_element_type=F32)
          for h in heads]
    cross = [jnp.dot(qs[h], st_ref[h].astype(BF16), preferred_element_type=F32) for h in heads]
    outs = []
    for h in heads:
        lg = lg_ref[d, h]
        if backward:
            mask = jj > ii
            dist = (jj - ii).astype(F32)
            q_dec = jnp.exp((n - col) * lg)
            k_dec = jnp.exp(col * lg)
        else:
            mask = ii >= jj
            dist = (ii - jj).astype(F32)
            q_dec = jnp.exp((col + 1.0) * lg)
            k_dec = jnp.exp((n - 1.0 - col) * lg)
        intra = jnp.where(mask, jnp.exp(jnp.where(mask, dist, 0.0) * lg), 0.0)
        c_dec = jnp.exp(jnp.full((1, B_V), n * lg, F32))
        kd_t = (ks[h].astype(F32) * k_dec).T.astype(BF16)
        st_ref[h] = st_ref[h] * c_dec + jnp.dot(kd_t, vs[h], preferred_element_type=F32)
        inner = jnp.dot((sc[h] * intra).astype(BF16), vs[h], preferred_element_type=F32)
        outs.append(inner + cross[h] * q_dec)
    return outs


def _retention_bwd_kernel(lg_ref, q_ref, k_ref, v_ref, o_ref, st_ref):
    @pl.when(pl.program_id(1) == 0)
    def _():
        st_ref[...] = jnp.zeros_like(st_ref)

    outs = _retention_heads(lg_ref, q_ref, k_ref, v_ref, st_ref, True)
    for h, o in enumerate(outs):
        o_ref[0, :, h * B_V:(h + 1) * B_V] = o.astype(BF16)


def _retention_fwd_kernel(lg_ref, q_ref, k_ref, v_ref, gate_ref, ob_ref, x_ref, w_ref, o_ref, st_ref):
    @pl.when(pl.program_id(1) == 0)
    def _():
        st_ref[...] = jnp.zeros_like(st_ref)

    outs = _retention_heads(lg_ref, q_ref, k_ref, v_ref, st_ref, False)
    parts = []
    for h, o_f in enumerate(outs):
        sl = slice(h * B_V, (h + 1) * B_V)
        o = o_f + ob_ref[0, :, sl].astype(F32)
        on = o * lax.rsqrt(jnp.mean(o * o, axis=-1, keepdims=True) + NORM_EPS)
        parts.append((on * _silu(gate_ref[0, :, sl].astype(F32))).astype(BF16))
    gated = jnp.concatenate(parts, axis=1)
    o_ref[0] = x_ref[0] + jnp.dot(gated, w_ref[...], preferred_element_type=F32)


def _retention(proj3, log_g, x3, w_out):
    bsz, seq, _ = proj3.shape
    nc = seq // B_CHUNK
    nqk = B_HEADS * B_QK
    smem = pl.BlockSpec(memory_space=pltpu.SMEM)
    state = [pltpu.VMEM((B_HEADS, B_QK, B_V), F32)]

    def specs(chunk):
        return [
            pl.BlockSpec((1, B_CHUNK, nqk), lambda b, t: (b, chunk(t), 0)),
            pl.BlockSpec((1, B_CHUNK, nqk), lambda b, t: (b, chunk(t), 1)),
            pl.BlockSpec((1, B_CHUNK, BRANCH), lambda b, t: (b, chunk(t), 1)),
        ]

    def rev(t):
        return nc - 1 - t

    def fwd(t):
        return t

    o_b = pl.pallas_call(
        _retention_bwd_kernel,
        grid=(bsz, nc),
        in_specs=[smem] + specs(rev),
        out_specs=pl.BlockSpec((1, B_CHUNK, BRANCH), lambda b, t: (b, rev(t), 0)),
        out_shape=jax.ShapeDtypeStruct((bsz, seq, BRANCH), BF16),
        scratch_shapes=state,
        compiler_params=_params(("parallel", "arbitrary")),
        name="retention_bwd",
    )(log_g, proj3, proj3, proj3)
    return pl.pallas_call(
        _retention_fwd_kernel,
        grid=(bsz, nc),
        in_specs=[smem] + specs(fwd) + [
            pl.BlockSpec((1, B_CHUNK, BRANCH), lambda b, t: (b, t, 2)),
            pl.BlockSpec((1, B_CHUNK, BRANCH), lambda b, t: (b, t, 0)),
            pl.BlockSpec((1, B_CHUNK, D_MODEL), lambda b, t: (b, t, 0)),
            pl.BlockSpec((BRANCH, D_MODEL), lambda b, t: (0, 0)),
        ],
        out_specs=pl.BlockSpec((1, B_CHUNK, D_MODEL), lambda b, t: (b, t, 0)),
        out_shape=jax.ShapeDtypeStruct((bsz, seq, D_MODEL), F32),
        scratch_shapes=state,
        compiler_params=_params(("parallel", "arbitrary")),
        name="retention_fwd",
    )(log_g, proj3, proj3, proj3, proj3, o_b, x3, w_out)


def _fold_kernel(w_ref, cs_ref, a_ref, b_ref):
    r = jnp.dot(w_ref[...], cs_ref[...], preferred_element_type=F32,
                precision=lax.Precision.HIGHEST)
    a_ref[...] = r[:, :C_GDIM].astype(BF16)
    b_ref[...] = r[:, C_GDIM:].astype(BF16)


def _fold_channel_dft(w_u, cs):
    spec = pl.BlockSpec((D_MODEL, C_GDIM), lambda g: (0, g))
    return pl.pallas_call(
        _fold_kernel,
        grid=(C_GROUPS,),
        in_specs=[spec, pl.BlockSpec((C_GDIM, 2 * C_GDIM), lambda g: (0, 0))],
        out_specs=[spec, spec],
        out_shape=[jax.ShapeDtypeStruct((D_MODEL, BRANCH), BF16)] * 2,
        compiler_params=_params(("parallel",)),
        name="fold_channel_dft",
    )(w_u, cs)


DFT_ROWS = 512


def _dft_mix_kernel(m1_ref, m2_ref, p_ref, q_ref, gate_ref, o_ref):
    n = m1_ref.shape[0]
    rc = min(n, DFT_ROWS)
    p = p_ref[0]
    q = q_ref[0]
    for r in range(n // rc):
        rows = slice(r * rc, (r + 1) * rc)
        acc = jnp.dot(m1_ref[rows, :], p, preferred_element_type=F32)
        acc = acc + jnp.dot(m2_ref[rows, :], q, preferred_element_type=F32)
        gt = gate_ref[0, rows, :].astype(F32)
        o_ref[0, rows, :] = (acc * _silu(gt)).astype(BF16)


def _dft_mix(m1, m2, src_p, src_gate, out_shape, n, tc, p_idx, q_idx, g_idx, o_idx, grid):
    mspec = pl.BlockSpec((n, n), lambda b, c: (0, 0))

    def dspec(idx):
        return pl.BlockSpec((1, n, tc), lambda b, c: (b, 0, idx(c)))

    return pl.pallas_call(
        _dft_mix_kernel,
        grid=grid,
        in_specs=[mspec, mspec, dspec(p_idx), dspec(q_idx), dspec(g_idx)],
        out_specs=dspec(o_idx),
        out_shape=jax.ShapeDtypeStruct(out_shape, BF16),
        compiler_params=_params(("parallel", "parallel")),
        name="dft_mix",
    )(m1, m2, src_p, src_p, src_gate)


def _dft_stage1_kernel(d_ref, a_ref, b_ref, y_ref):
    dc = d_ref[0, :C_RADIX, :]
    ds = d_ref[0, C_RADIX:, :]
    a = a_ref[0]
    b = b_ref[0]
    yr = jnp.dot(dc, a, preferred_element_type=F32) - jnp.dot(ds, b, preferred_element_type=F32)
    yi = jnp.dot(dc, b, preferred_element_type=F32) + jnp.dot(ds, a, preferred_element_type=F32)
    y_ref[0, 0, :, :BRANCH] = yr.astype(BF16)
    y_ref[0, 0, :, BRANCH:] = (-yi).astype(BF16)


def _dft_stage1(projv, dtab, bsz, n1):
    return pl.pallas_call(
        _dft_stage1_kernel,
        grid=(bsz, n1),
        in_specs=[
            pl.BlockSpec((1, 2 * C_RADIX, C_RADIX), lambda b, m: (m, 0, 0)),
            pl.BlockSpec((1, C_RADIX, BRANCH), lambda b, m: (b, 0, 3 * m)),
            pl.BlockSpec((1, C_RADIX, BRANCH), lambda b, m: (b, 0, 3 * m + 1)),
        ],
        out_specs=pl.BlockSpec((1, 1, C_RADIX, 2 * BRANCH), lambda b, m: (b, m, 0, 0)),
        out_shape=jax.ShapeDtypeStruct((bsz, n1, C_RADIX, 2 * BRANCH), BF16),
        compiler_params=_params(("parallel", "parallel")),
        name="dft_stage1",
    )(dtab, projv, projv)


def _trig(m, period, scale):
    ang = (2.0 * math.pi / period) * (m % period).astype(F32)
    return jnp.cos(ang) * scale, jnp.sin(ang) * scale


def _fourier_mix(proj3):
    bsz, seq, _ = proj3.shape
    if seq <= C_DIRECT_MAX:
        idx = jnp.arange(seq, dtype=jnp.int32)
        cm, sm = _trig(idx[:, None] * idx[None, :], seq, seq ** -0.5)
        tc = 512
        nct = BRANCH // tc
        return _dft_mix(cm.astype(BF16), (-sm).astype(BF16), proj3, proj3, (bsz, seq, BRANCH),
                        seq, tc, lambda c: c, lambda c: nct + c, lambda c: 2 * nct + c,
                        lambda c: c, (bsz, nct))
    n2 = C_RADIX
    n1 = seq // n2
    k2 = jnp.arange(n2, dtype=jnp.int32)
    tok = jnp.arange(n1, dtype=jnp.int32)[:, None, None] + n1 * jnp.arange(n2, dtype=jnp.int32)[None, None, :]
    dc, ds = _trig(k2[None, :, None] * tok, seq, n2 ** -0.5)
    dtab = jnp.concatenate([dc, ds], axis=1).astype(BF16)
    y = _dft_stage1(proj3.reshape(bsz, n2, n1 * 3 * BRANCH), dtab, bsz, n1)
    i1 = jnp.arange(n1, dtype=jnp.int32)
    c1, s1 = _trig(i1[:, None] * i1[None, :], n1, n1 ** -0.5)
    out = _dft_mix(c1.astype(BF16), s1.astype(BF16), y.reshape(bsz, n1, n2 * 2 * BRANCH),
                   proj3.reshape(bsz, n1, n2 * 3 * BRANCH), (bsz, n1, n2 * BRANCH),
                   n1, BRANCH, lambda c: 2 * c, lambda c: 2 * c + 1, lambda c: 3 * c + 2,
                   lambda c: c, (bsz, n2))
    return out.reshape(bsz, seq, BRANCH)


OUT_TM = 512


def _outproj_kernel(m_ref, x_ref, w_ref, o_ref):
    o_ref[...] = x_ref[...] + jnp.dot(m_ref[...], w_ref[...], preferred_element_type=F32)


def _outproj(mixed2d, x2d, w_out):
    t = x2d.shape[0]
    return pl.pallas_call(
        _outproj_kernel,
        grid=(t // OUT_TM,),
        in_specs=[
            pl.BlockSpec((OUT_TM, BRANCH), lambda i: (i, 0)),
            pl.BlockSpec((OUT_TM, D_MODEL), lambda i: (i, 0)),
            pl.BlockSpec((BRANCH, D_MODEL), lambda i: (0, 0)),
        ],
        out_specs=pl.BlockSpec((OUT_TM, D_MODEL), lambda i: (i, 0)),
        out_shape=jax.ShapeDtypeStruct((t, D_MODEL), F32),
        compiler_params=_params(("parallel",)),
        name="outproj",
    )(mixed2d, x2d, w_out)


def _rope_tables_a(seq):
    half = A_ROT // 2
    inv_freq = jnp.exp(-(jnp.arange(half, dtype=F32) * (2.0 / A_ROT)) * math.log(A_THETA))
    ang = jnp.arange(seq, dtype=F32)[:, None] * inv_freq[None, :]
    cos, sin = jnp.cos(ang), jnp.sin(ang)
    d = jnp.arange(LANES) % A_HEAD
    f = d % half
    cm = jnp.where(d[None, :] < A_ROT, cos[:, f], 1.0)
    s1 = jnp.where(d[None, :] < half, -sin[:, f], 0.0)
    s2 = jnp.where((d[None, :] >= half) & (d[None, :] < A_ROT), sin[:, f], 0.0)
    qs = A_HEAD ** -0.5 * LOG2E
    return jnp.stack([cm * qs, s1 * qs, s2 * qs, cm, s1, s2]).astype(F32)


def _rope_tables_b(seq):
    half = B_QK // 2
    inv_freq = jnp.exp(-(jnp.arange(half, dtype=F32) * (2.0 / B_QK)) * math.log(B_THETA))
    ang = jnp.arange(seq, dtype=F32)[:, None] * inv_freq[None, :]
    cos, sin = jnp.cos(ang), jnp.sin(ang)
    ks = B_QK ** -0.5
    return jnp.stack([cos, sin, cos * ks, sin * ks]).astype(F32)


def _weights_a(w_in):
    nq = A_QH * A_HEAD
    nkv = A_KVH * A_HEAD
    q = w_in[:, :nq]
    k = w_in[:, nq:nq + nkv].reshape(D_MODEL, A_KVH, 1, A_HEAD)
    v = w_in[:, nq + nkv:nq + 2 * nkv]
    gate = w_in[:, nq + 2 * nkv:]
    kk = jnp.broadcast_to(k, (D_MODEL, A_KVH, 2, A_HEAD)).reshape(D_MODEL, 2 * nkv)
    return jnp.concatenate([q, gate, kk, v], axis=1).astype(BF16)


def _weights_c(w_in):
    m = jnp.arange(C_GDIM, dtype=jnp.int32)
    cm, sm = _trig(m[:, None] * m[None, :], C_GDIM, C_GDIM ** -0.5)
    w_a, w_b = _fold_channel_dft(w_in[:, :BRANCH], jnp.concatenate([cm, sm], axis=1))
    return jnp.concatenate([w_a, w_b, w_in[:, BRANCH:].astype(BF16)], axis=1)


def _trunk(x, norm_g, fin_g, wa, a_sink, a_w_out, wb, log_g, b_w_out, wc, c_w_out):
    bsz, seq, _ = x.shape
    t = bsz * seq
    x2d = x.reshape(t, D_MODEL)
    tabs_a = _rope_tables_a(seq)
    tabs_b = _rope_tables_b(seq)
    fin = fin_g.reshape(1, D_MODEL)

    def layer_a(x2d, layer, j, final):
        q3, gate3, kk, vt = _inproj_a(x2d, norm_g[layer].reshape(1, D_MODEL), wa[j], tabs_a, bsz, seq)
        return _attention(x2d, q3, gate3, kk, vt, a_sink[j], a_w_out[j], fin, bsz, seq, final)

    x2d = layer_a(x2d, 0, 0, False)

    proj = _inproj(x2d, norm_g[1].reshape(1, D_MODEL), wb, tabs_b, seq, 2 * B_HEADS)
    x2d = _retention(proj.reshape(bsz, seq, -1), log_g, x2d.reshape(bsz, seq, D_MODEL),
                     b_w_out).reshape(t, D_MODEL)

    proj = _inproj(x2d, norm_g[2].reshape(1, D_MODEL), wc, None, seq, 0)
    mixed = _fourier_mix(proj.reshape(bsz, seq, -1))
    x2d = _outproj(mixed.reshape(t, BRANCH), x2d, c_w_out)

    x2d = layer_a(x2d, 3, 1, True)
    return x2d.reshape(bsz, seq, D_MODEL)


def kernel(x_prompt, x_sample, norm_g, final_norm_g, a_w_in, a_sink, a_w_out, b_w_in, b_decay,
           b_w_out, c_w_in, c_w_out):
    wa = [_weights_a(a_w_in[j]) for j in range(a_w_in.shape[0])]
    a_out = [a_w_out[j].astype(BF16) for j in range(a_w_out.shape[0])]
    wb = b_w_in[0].astype(BF16)
    log_g = jax.nn.log_sigmoid(b_decay[0].astype(F32))
    wc = _weights_c(c_w_in[0])
    args = (norm_g, final_norm_g, wa, a_sink, a_out, wb, log_g, b_w_out[0].astype(BF16),
            wc, c_w_out[0].astype(BF16))
    return (_trunk(x_prompt, *args), _trunk(x_sample, *args))
```

```python
import functools
import math

import jax
import jax.numpy as jnp
from jax import lax
from jax.experimental import pallas as pl
from jax.experimental.pallas import tpu as pltpu

D_MODEL = 1024
BRANCH = 2048
NORM_EPS = 1e-6

A_HEAD = 64
A_QH = 32
A_KVH = 4
A_ROT = 16
A_THETA = 500000.0
A_WIN = 128
A_QBLK = 256
A_SPAN = A_QBLK + 2 * A_WIN
A_PAIRS = A_QH // 2
A_LOOKAHEAD = 3
A_VPAD = 16
LOG2E = math.log2(math.e)

B_QK = 256
B_HEADS = 4
B_V = 512
B_CHUNK = 256
B_THETA = 10000.0

C_GROUPS = 4
C_GDIM = 512
C_DIRECT_MAX = 2048

LANES = 128
VMEM_LIMIT = 56 * 1024 * 1024

BF16 = jnp.bfloat16
F32 = jnp.float32


def _params(sem):
    return pltpu.CompilerParams(dimension_semantics=sem, vmem_limit_bytes=VMEM_LIMIT)


def _silu(x):
    return x * jax.nn.sigmoid(x)


def _rms(x, g):
    ms = jnp.mean(x * x, axis=-1, keepdims=True)
    return (x * lax.rsqrt(ms + NORM_EPS)) * g


IN_TM = 512
IN_TN = 512


def _inproj_a_kernel(x_ref, g_ref, w_ref, tab_ref, q_ref, gate_ref, kk_ref, vt_ref):
    h = _rms(x_ref[...], g_ref[...]).astype(BF16)

    def rope(a, t0):
        return (a * tab_ref[t0] + pltpu.roll(a, LANES - A_ROT // 2, 1) * tab_ref[t0 + 1]
                + pltpu.roll(a, A_ROT // 2, 1) * tab_ref[t0 + 2])

    for c in range(4):
        acc = jnp.dot(h, w_ref[:, c * IN_TN:(c + 1) * IN_TN], preferred_element_type=F32)
        for s in range(4):
            q_ref[4 * c + s] = rope(acc[:, s * LANES:(s + 1) * LANES], 0).astype(BF16)
    for c in range(4):
        acc = jnp.dot(h, w_ref[:, BRANCH + c * IN_TN:BRANCH + (c + 1) * IN_TN],
                      preferred_element_type=F32)
        for s in range(4):
            gate_ref[4 * c + s] = acc[:, s * LANES:(s + 1) * LANES].astype(BF16)
    acc = jnp.dot(h, w_ref[:, 2 * BRANCH:2 * BRANCH + IN_TN], preferred_element_type=F32)
    for s in range(A_KVH):
        kk_ref[s] = rope(acc[:, s * LANES:(s + 1) * LANES], 3).astype(BF16)
    acc = jnp.dot(h, w_ref[:, 2 * BRANCH + IN_TN:], preferred_element_type=F32)
    vt_ref[0] = acc.T.astype(BF16)


def _inproj_a(x2d, g, w, tabs, bsz, seq):
    t = x2d.shape[0]
    nblk = seq // IN_TM
    n = w.shape[1]
    return pl.pallas_call(
        _inproj_a_kernel,
        grid=(t // IN_TM,),
        in_specs=[
            pl.BlockSpec((IN_TM, D_MODEL), lambda i: (i, 0)),
            pl.BlockSpec((1, D_MODEL), lambda i: (0, 0)),
            pl.BlockSpec((D_MODEL, n), lambda i: (0, 0)),
            pl.BlockSpec((6, IN_TM, LANES), lambda i: (0, i % nblk, 0)),
        ],
        out_specs=[
            pl.BlockSpec((A_PAIRS, IN_TM, LANES), lambda i: (0, i, 0)),
            pl.BlockSpec((A_PAIRS, IN_TM, LANES), lambda i: (0, i, 0)),
            pl.BlockSpec((A_KVH, IN_TM, LANES), lambda i: (0, i, 0)),
            pl.BlockSpec((1, A_KVH * A_HEAD, IN_TM), lambda i: (i // nblk, 0, i % nblk)),
        ],
        out_shape=[
            jax.ShapeDtypeStruct((A_PAIRS, t, LANES), BF16),
            jax.ShapeDtypeStruct((A_PAIRS, t, LANES), BF16),
            jax.ShapeDtypeStruct((A_KVH, t, LANES), BF16),
            jax.ShapeDtypeStruct((bsz, A_KVH * A_HEAD, seq), BF16),
        ],
        compiler_params=_params(("parallel",)),
        name="inproj_a",
    )(x2d, g, w, tabs)


def _inproj_kernel(x_ref, g_ref, w_ref, *rest, rope_heads):
    if rope_heads:
        tab_ref, o_ref = rest
    else:
        (o_ref,) = rest
    h = _rms(x_ref[...], g_ref[...]).astype(BF16)
    n = w_ref.shape[1]
    for c in range(n // IN_TN):
        acc = jnp.dot(h, w_ref[:, c * IN_TN:(c + 1) * IN_TN], preferred_element_type=F32)
        for s in range(IN_TN // B_QK):
            head = c * (IN_TN // B_QK) + s
            lo = s * B_QK
            if head < rope_heads:
                t0 = 0 if head < rope_heads // 2 else 2
                cos, sin = tab_ref[t0], tab_ref[t0 + 1]
                x1 = acc[:, lo:lo + LANES]
                x2 = acc[:, lo + LANES:lo + B_QK]
                o_ref[:, c * IN_TN + lo:c * IN_TN + lo + LANES] = (x1 * cos - x2 * sin).astype(BF16)
                o_ref[:, c * IN_TN + lo + LANES:c * IN_TN + lo + B_QK] = (x2 * cos + x1 * sin).astype(BF16)
            else:
                o_ref[:, c * IN_TN + lo:c * IN_TN + lo + B_QK] = acc[:, lo:lo + B_QK].astype(BF16)


def _inproj(x2d, g, w, tabs, seq, rope_heads):
    t = x2d.shape[0]
    n = w.shape[1]
    nblk = seq // IN_TM
    in_specs = [
        pl.BlockSpec((IN_TM, D_MODEL), lambda i: (i, 0)),
        pl.BlockSpec((1, D_MODEL), lambda i: (0, 0)),
        pl.BlockSpec((D_MODEL, n), lambda i: (0, 0)),
    ]
    args = [x2d, g, w]
    if rope_heads:
        in_specs.append(pl.BlockSpec((4, IN_TM, LANES), lambda i: (0, i % nblk, 0)))
        args.append(tabs)
    return pl.pallas_call(
        functools.partial(_inproj_kernel, rope_heads=rope_heads),
        grid=(t // IN_TM,),
        in_specs=in_specs,
        out_specs=pl.BlockSpec((IN_TM, n), lambda i: (i, 0)),
        out_shape=jax.ShapeDtypeStruct((t, n), BF16),
        compiler_params=_params(("parallel",)),
        name="inproj",
    )(*args)


def _attn_kernel(sink_ref, q_ref, gate_ref, kp_ref, kc_ref, kn_ref, vp_ref, vc_ref, vn_ref,
                 x_ref, w_ref, fg_ref, o_ref, ka_ref, kb_ref, v_ref, bias_ref, *, final):
    i = pl.program_id(1)
    last = pl.num_programs(1) - 1
    rows = A_SPAN - A_WIN
    per_kv = A_PAIRS // A_KVH

    lane = lax.broadcasted_iota(jnp.int32, (A_SPAN, LANES), 1)
    ones_row = (lax.broadcasted_iota(jnp.int32, (A_VPAD, A_SPAN), 0) == 0).astype(BF16)
    for h in range(A_KVH):
        span = jnp.concatenate([kp_ref[h], kc_ref[h], kn_ref[h]], axis=0)
        ka_ref[h] = jnp.where(lane < A_HEAD, span, jnp.zeros_like(span))
        kb_ref[h] = jnp.where(lane >= A_HEAD, span, jnp.zeros_like(span))
        v_ref[h, :A_HEAD, :] = jnp.concatenate([vp_ref[0, h], vc_ref[0, h], vn_ref[0, h]], axis=1)
        v_ref[h, A_HEAD:, :] = ones_row

    r = lax.broadcasted_iota(jnp.int32, (A_WIN, LANES), 0)
    c = lax.broadcasted_iota(jnp.int32, (A_WIN, LANES), 1)
    neg = jnp.full((A_WIN, LANES), -1e30, F32)
    top = jnp.where(r >= c, 0.0, neg)
    bot = jnp.where(r <= c, 0.0, neg)
    bias_ref[0] = jnp.where(i == 0, neg, top)
    bias_ref[1] = bot
    bias_ref[2] = top
    bias_ref[3] = jnp.where(i == last, neg, bot)
    pad = jnp.zeros((A_WIN, LANES), BF16)

    units = [(g, jl, a) for g in range(A_KVH) for jl in range(per_kv) for a in range(2)]

    def scores(n):
        g, jl, a = units[n]
        k = (ka_ref, kb_ref)[a][g]
        return lax.dot_general(k, q_ref[g * per_kv + jl], (((1,), (1,)), ((), ())),
                               preferred_element_type=F32)

    def softmax_pv(s_t, sink, g):
        ps, extra = [], []
        for jj in range(2):
            blk = s_t[jj * A_WIN:jj * A_WIN + rows, jj * LANES:(jj + 1) * LANES]
            parts = [blk[:A_WIN] + bias_ref[2 * jj], blk[A_WIN:2 * A_WIN],
                     blk[2 * A_WIN:] + bias_ref[2 * jj + 1]]
            mx = jnp.full((1, LANES), sink, F32)
            for part in parts:
                mx = jnp.maximum(mx, jnp.max(part, axis=0, keepdims=True))
            es = [jnp.exp2(part - mx).astype(BF16) for part in parts]
            ps.append(jnp.concatenate(es + [pad] if jj == 0 else [pad] + es, axis=0))
            extra.append(jnp.exp2(sink - mx))
        o_ext = jnp.dot(v_ref[g], jnp.concatenate(ps, axis=1), preferred_element_type=F32)
        den = o_ext[A_HEAD:A_HEAD + 1, :] + jnp.concatenate(extra, axis=1)
        return o_ext[:A_HEAD, :] / den

    pending = [scores(n) for n in range(A_LOOKAHEAD)]
    outs, gated = [], []
    y = x_ref[...]
    for n, (g, jl, a) in enumerate(units):
        if n + A_LOOKAHEAD < len(units):
            pending.append(scores(n + A_LOOKAHEAD))
        j = g * per_kv + jl
        outs.append(softmax_pv(pending.pop(0), sink_ref[2 * j + a] * LOG2E, g))
        if a == 1:
            o_pair = jnp.concatenate(outs[-2:], axis=0).T
            gt = gate_ref[j].astype(F32)
            gated.append((o_pair * _silu(gt)).astype(BF16))
            if jl == per_kv - 1:
                wg = w_ref[g * per_kv * LANES:(g + 1) * per_kv * LANES, :]
                y = y + jnp.dot(jnp.concatenate(gated[-per_kv:], axis=1), wg,
                                preferred_element_type=F32)
    if final:
        y = _rms(y, fg_ref[...])
    o_ref[...] = y


def _attention(x2d, q3, gate3, kk, vt, sink, w_out, fin_g, bsz, seq, final):
    t = x2d.shape[0]
    nqb = seq // A_QBLK
    nkb = seq // A_WIN
    vt4 = vt.reshape(bsz, A_KVH, A_HEAD, seq)
    grid_spec = pltpu.PrefetchScalarGridSpec(
        num_scalar_prefetch=1,
        grid=(bsz, nqb),
        in_specs=[
            pl.BlockSpec((A_PAIRS, A_QBLK, LANES), lambda b, i, s: (0, b * nqb + i, 0)),
            pl.BlockSpec((A_PAIRS, A_QBLK, LANES), lambda b, i, s: (0, b * nqb + i, 0)),
            pl.BlockSpec((A_KVH, A_WIN, LANES),
                         lambda b, i, s: (0, b * nkb + jnp.maximum(2 * i - 1, 0), 0)),
            pl.BlockSpec((A_KVH, A_QBLK, LANES), lambda b, i, s: (0, b * nqb + i, 0)),
            pl.BlockSpec((A_KVH, A_WIN, LANES),
                         lambda b, i, s: (0, b * nkb + jnp.minimum(2 * i + 2, nkb - 1), 0)),
            pl.BlockSpec((1, A_KVH, A_HEAD, A_WIN),
                         lambda b, i, s: (b, 0, 0, jnp.maximum(2 * i - 1, 0))),
            pl.BlockSpec((1, A_KVH, A_HEAD, A_QBLK), lambda b, i, s: (b, 0, 0, i)),
            pl.BlockSpec((1, A_KVH, A_HEAD, A_WIN),
                         lambda b, i, s: (b, 0, 0, jnp.minimum(2 * i + 2, nkb - 1))),
            pl.BlockSpec((A_QBLK, D_MODEL), lambda b, i, s: (b * nqb + i, 0)),
            pl.BlockSpec((BRANCH, D_MODEL), lambda b, i, s: (0, 0)),
            pl.BlockSpec((1, D_MODEL), lambda b, i, s: (0, 0)),
        ],
        out_specs=pl.BlockSpec((A_QBLK, D_MODEL), lambda b, i, s: (b * nqb + i, 0)),
        scratch_shapes=[
            pltpu.VMEM((A_KVH, A_SPAN, LANES), BF16),
            pltpu.VMEM((A_KVH, A_SPAN, LANES), BF16),
            pltpu.VMEM((A_KVH, A_HEAD + A_VPAD, A_SPAN), BF16),
            pltpu.VMEM((4, A_WIN, LANES), F32),
        ],
    )
    return pl.pallas_call(
        functools.partial(_attn_kernel, final=final),
        grid_spec=grid_spec,
        out_shape=jax.ShapeDtypeStruct((t, D_MODEL), F32),
        compiler_params=_params(("parallel", "parallel")),
        name="attention",
    )(sink, q3, gate3, kk, kk, kk, vt4, vt4, vt4, x2d, w_out, fin_g)


def _retention_heads(lg_ref, q_ref, k_ref, v_ref, st_ref, backward):
    n = B_CHUNK
    ii = lax.broadcasted_iota(jnp.int32, (n, n), 0)
    jj = lax.broadcasted_iota(jnp.int32, (n, n), 1)
    col = lax.broadcasted_iota(jnp.int32, (n, 1), 0).astype(F32)
    d = 1 if backward else 0
    heads = range(B_HEADS)
    qs = [q_ref[0, :, h * B_QK:(h + 1) * B_QK] for h in heads]
    ks = [k_ref[0, :, h * B_QK:(h + 1) * B_QK] for h in heads]
    vs = [v_ref[0, :, h * B_V:(h + 1) * B_V] for h in heads]
    sc = [lax.dot_general(qs[h], ks[h], (((1,), (1,)), ((), ())), preferred_element_type=F32)
          for h in heads]
    cross = [jnp.dot(qs[h], st_ref[h].astype(BF16), preferred_element_type=F32) for h in heads]
    outs = []
    for h in heads:
        lg = lg_ref[d, h]
        if backward:
            mask = jj > ii
            dist = (jj - ii).astype(F32)
            q_dec = jnp.exp((n - col) * lg)
            k_dec = jnp.exp(col * lg)
        else:
            mask = ii >= jj
            dist = (ii - jj).astype(F32)
            q_dec = jnp.exp((col + 1.0) * lg)
            k_dec = jnp.exp((n - 1.0 - col) * lg)
        intra = jnp.where(mask, jnp.exp(jnp.where(mask, dist, 0.0) * lg), 0.0)
        c_dec = jnp.exp(jnp.full((1, B_V), n * lg, F32))
        kd_t = (ks[h].astype(F32) * k_dec).T.astype(BF16)
        st_ref[h] = st_ref[h] * c_dec + jnp.dot(kd_t, vs[h], preferred_element_type=F32)
        inner = jnp.dot((sc[h] * intra).astype(BF16), vs[h], preferred_element_type=F32)
        outs.append(inner + cross[h] * q_dec)
    return outs


def _retention_bwd_kernel(lg_ref, q_ref, k_ref, v_ref, o_ref, st_ref):
    @pl.when(pl.program_id(1) == 0)
    def _():
        st_ref[...] = jnp.zeros_like(st_ref)

    outs = _retention_heads(lg_ref, q_ref, k_ref, v_ref, st_ref, True)
    for h, o in enumerate(outs):
        o_ref[0, :, h * B_V:(h + 1) * B_V] = o.astype(BF16)


def _retention_fwd_kernel(lg_ref, q_ref, k_ref, v_ref, gate_ref, ob_ref, x_ref, w_ref, o_ref, st_ref):
    @pl.when(pl.program_id(1) == 0)
    def _():
        st_ref[...] = jnp.zeros_like(st_ref)

    outs = _retention_heads(lg_ref, q_ref, k_ref, v_ref, st_ref, False)
    parts = []
    for h, o_f in enumerate(outs):
        sl = slice(h * B_V, (h + 1) * B_V)
        o = o_f + ob_ref[0, :, sl].astype(F32)
        on = o * lax.rsqrt(jnp.mean(o * o, axis=-1, keepdims=True) + NORM_EPS)
        parts.append((on * _silu(gate_ref[0, :, sl].astype(F32))).astype(BF16))
    gated = jnp.concatenate(parts, axis=1)
    o_ref[0] = x_ref[0] + jnp.dot(gated, w_ref[...], preferred_element_type=F32)


def _retention(proj3, log_g, x3, w_out):
    bsz, seq, _ = proj3.shape
    nc = seq // B_CHUNK
    nqk = B_HEADS * B_QK
    smem = pl.BlockSpec(memory_space=pltpu.SMEM)
    state = [pltpu.VMEM((B_HEADS, B_QK, B_V), F32)]

    def specs(chunk):
        return [
            pl.BlockSpec((1, B_CHUNK, nqk), lambda b, t: (b, chunk(t), 0)),
            pl.BlockSpec((1, B_CHUNK, nqk), lambda b, t: (b, chunk(t), 1)),
            pl.BlockSpec((1, B_CHUNK, BRANCH), lambda b, t: (b, chunk(t), 1)),
        ]

    def rev(t):
        return nc - 1 - t

    def fwd(t):
        return t

    o_b = pl.pallas_call(
        _retention_bwd_kernel,
        grid=(bsz, nc),
        in_specs=[smem] + specs(rev),
        out_specs=pl.BlockSpec((1, B_CHUNK, BRANCH), lambda b, t: (b, rev(t), 0)),
        out_shape=jax.ShapeDtypeStruct((bsz, seq, BRANCH), BF16),
        scratch_shapes=state,
        compiler_params=_params(("parallel", "arbitrary")),
        name="retention_bwd",
    )(log_g, proj3, proj3, proj3)
    return pl.pallas_call(
        _retention_fwd_kernel,
        grid=(bsz, nc),
        in_specs=[smem] + specs(fwd) + [
            pl.BlockSpec((1, B_CHUNK, BRANCH), lambda b, t: (b, t, 2)),
            pl.BlockSpec((1, B_CHUNK, BRANCH), lambda b, t: (b, t, 0)),
            pl.BlockSpec((1, B_CHUNK, D_MODEL), lambda b, t: (b, t, 0)),
            pl.BlockSpec((BRANCH, D_MODEL), lambda b, t: (0, 0)),
        ],
        out_specs=pl.BlockSpec((1, B_CHUNK, D_MODEL), lambda b, t: (b, t, 0)),
        out_shape=jax.ShapeDtypeStruct((bsz, seq, D_MODEL), F32),
        scratch_shapes=state,
        compiler_params=_params(("parallel", "arbitrary")),
        name="retention_fwd",
    )(log_g, proj3, proj3, proj3, proj3, o_b, x3, w_out)


def _fold_kernel(w_ref, cs_ref, a_ref, b_ref):
    r = jnp.dot(w_ref[...], cs_ref[...], preferred_element_type=F32,
                precision=lax.Precision.HIGHEST)
    a_ref[...] = r[:, :C_GDIM].astype(BF16)
    b_ref[...] = r[:, C_GDIM:].astype(BF16)


def _fold_channel_dft(w_u, cs):
    spec = pl.BlockSpec((D_MODEL, C_GDIM), lambda g: (0, g))
    return pl.pallas_call(
        _fold_kernel,
        grid=(C_GROUPS,),
        in_specs=[spec, pl.BlockSpec((C_GDIM, 2 * C_GDIM), lambda g: (0, 0))],
        out_specs=[spec, spec],
        out_shape=[jax.ShapeDtypeStruct((D_MODEL, BRANCH), BF16)] * 2,
        compiler_params=_params(("parallel",)),
        name="fold_channel_dft",
    )(w_u, cs)


DFT_ROWS = 512


def _dft_mix_kernel(m1_ref, m2_ref, p_ref, q_ref, gate_ref, o_ref):
    n = m1_ref.shape[0]
    rc = min(n, DFT_ROWS)
    p = p_ref[0]
    q = q_ref[0]
    for r in range(n // rc):
        rows = slice(r * rc, (r + 1) * rc)
        acc = jnp.dot(m1_ref[rows, :], p, preferred_element_type=F32)
        acc = acc + jnp.dot(m2_ref[rows, :], q, preferred_element_type=F32)
        gt = gate_ref[0, rows, :].astype(F32)
        o_ref[0, rows, :] = (acc * _silu(gt)).astype(BF16)


def _dft_mix(m1, m2, src_p, src_gate, out_shape, n, tc, p_idx, q_idx, g_idx, o_idx, grid):
    mspec = pl.BlockSpec((n, n), lambda b, c: (0, 0))

    def dspec(idx):
        return pl.BlockSpec((1, n, tc), lambda b, c: (b, 0, idx(c)))

    return pl.pallas_call(
        _dft_mix_kernel,
        grid=grid,
        in_specs=[mspec, mspec, dspec(p_idx), dspec(q_idx), dspec(g_idx)],
        out_specs=dspec(o_idx),
        out_shape=jax.ShapeDtypeStruct(out_shape, BF16),
        compiler_params=_params(("parallel", "parallel")),
        name="dft_mix",
    )(m1, m2, src_p, src_p, src_gate)


def _trig(m, period, scale):
    ang = (2.0 * math.pi / period) * (m % period).astype(F32)
    return jnp.cos(ang) * scale, jnp.sin(ang) * scale


def _fourier_mix(proj3):
    bsz, seq, _ = proj3.shape
    idx = jnp.arange(seq, dtype=jnp.int32)
    cm, sm = _trig(idx[:, None] * idx[None, :], seq, seq ** -0.5)
    tc = 512
    nct = BRANCH // tc
    return _dft_mix(cm.astype(BF16), (-sm).astype(BF16), proj3, proj3, (bsz, seq, BRANCH),
                    seq, tc, lambda c: c, lambda c: nct + c, lambda c: 2 * nct + c,
                    lambda c: c, (bsz, nct))


C_GRP = 8
C_SUB = 64
C_ROWS = 256
C_NCHUNK = 1024


def _inproj_perm_kernel(x_ref, g_ref, w_ref, o_ref, xs_ref):
    rh = x_ref.shape[1]
    n = w_ref.shape[1]
    for i in range(C_GRP):
        xs_ref[i * rh:(i + 1) * rh, :] = x_ref[0, :, i, :]
    per_dot = C_ROWS // rh
    for j in range(C_GRP // per_dot):
        h = _rms(xs_ref[j * C_ROWS:(j + 1) * C_ROWS, :], g_ref[...]).astype(BF16)
        for c in range(n // C_NCHUNK):
            cols = slice(c * C_NCHUNK, (c + 1) * C_NCHUNK)
            acc = jnp.dot(h, w_ref[:, cols], preferred_element_type=F32)
            for i in range(per_dot):
                o_ref[0, j * per_dot + i, :, cols] = acc[i * rh:(i + 1) * rh, :].astype(BF16)


def _inproj_perm(x4, g, w):
    bsz, r, _, _ = x4.shape
    n = w.shape[1]
    rh = min(r, C_SUB)
    return pl.pallas_call(
        _inproj_perm_kernel,
        grid=(bsz, r // C_GRP, r // rh),
        in_specs=[
            pl.BlockSpec((1, rh, C_GRP, D_MODEL), lambda b, m, s: (b, s, m, 0)),
            pl.BlockSpec((1, D_MODEL), lambda b, m, s: (0, 0)),
            pl.BlockSpec((D_MODEL, n), lambda b, m, s: (0, 0)),
        ],
        out_specs=pl.BlockSpec((1, C_GRP, rh, n), lambda b, m, s: (b, m, s, 0)),
        out_shape=jax.ShapeDtypeStruct((bsz, r, r, n), BF16),
        scratch_shapes=[pltpu.VMEM((C_GRP * rh, D_MODEL), F32)],
        compiler_params=_params(("parallel", "parallel", "parallel")),
        name="inproj_perm",
    )(x4, g, w)


def _dft_stage1_kernel(d_ref, a_ref, b_ref, yr_ref, yi_ref):
    r = a_ref.shape[2]
    for i in range(C_GRP):
        dc = d_ref[i, :r, :]
        ds = d_ref[i, r:, :]
        a = a_ref[0, i]
        b = b_ref[0, i]
        yr = jnp.dot(dc, a, preferred_element_type=F32) - jnp.dot(ds, b, preferred_element_type=F32)
        yi = jnp.dot(dc, b, preferred_element_type=F32) + jnp.dot(ds, a, preferred_element_type=F32)
        yr_ref[0, :, i, :] = yr
        yi_ref[0, :, i, :] = -yi


def _dft_stage1(proj4, dtab):
    bsz, r, _, _ = proj4.shape
    tc = 1024
    nct = BRANCH // tc
    yspec = pl.BlockSpec((1, r, C_GRP, tc), lambda b, m, c: (b, 0, m, c))
    return pl.pallas_call(
        _dft_stage1_kernel,
        grid=(bsz, r // C_GRP, nct),
        in_specs=[
            pl.BlockSpec((C_GRP, 2 * r, r), lambda b, m, c: (m, 0, 0)),
            pl.BlockSpec((1, C_GRP, r, tc), lambda b, m, c: (b, m, 0, c)),
            pl.BlockSpec((1, C_GRP, r, tc), lambda b, m, c: (b, m, 0, nct + c)),
        ],
        out_specs=[yspec, yspec],
        out_shape=[jax.ShapeDtypeStruct((bsz, r, r, BRANCH), F32)] * 2,
        compiler_params=_params(("parallel", "parallel", "parallel")),
        name="dft_stage1",
    )(dtab, proj4, proj4)


def _dft_stage2_kernel(c1_ref, s1_ref, yr_ref, yi_ref, gate_ref, w_ref, x_ref, o_ref, acc_ref):
    c = pl.program_id(2)
    r = c1_ref.shape[0]

    @pl.when(c == 0)
    def _():
        for i in range(C_GRP):
            acc_ref[i * r:(i + 1) * r, :] = x_ref[0, :, i, :]

    parts = []
    for i in range(C_GRP):
        mix = jnp.dot(c1_ref[...], yr_ref[0, i].astype(BF16), preferred_element_type=F32)
        mix = mix + jnp.dot(s1_ref[...], yi_ref[0, i].astype(BF16), preferred_element_type=F32)
        parts.append((mix * _silu(gate_ref[0, i].astype(F32))).astype(BF16))
    acc_ref[...] += jnp.dot(jnp.concatenate(parts, axis=0), w_ref[...], preferred_element_type=F32)

    @pl.when(c == pl.num_programs(2) - 1)
    def _():
        for i in range(C_GRP):
            o_ref[0, :, i, :] = acc_ref[i * r:(i + 1) * r, :]


def _dft_stage2(c1, s1, y_r, y_i, proj4, w_out, x4):
    bsz, r, _, _ = x4.shape
    tc = 512
    nct = BRANCH // tc
    mspec = pl.BlockSpec((r, r), lambda b, m, c: (0, 0))
    yspec = pl.BlockSpec((1, C_GRP, r, tc), lambda b, m, c: (b, m, 0, c))
    xspec = pl.BlockSpec((1, r, C_GRP, D_MODEL), lambda b, m, c: (b, 0, m, 0))
    return pl.pallas_call(
        _dft_stage2_kernel,
        grid=(bsz, r // C_GRP, nct),
        in_specs=[
            mspec, mspec, yspec, yspec,
            pl.BlockSpec((1, C_GRP, r, tc), lambda b, m, c: (b, m, 0, 2 * nct + c)),
            pl.BlockSpec((tc, D_MODEL), lambda b, m, c: (c, 0)),
            xspec,
        ],
        out_specs=xspec,
        out_shape=jax.ShapeDtypeStruct(x4.shape, F32),
        scratch_shapes=[pltpu.VMEM((C_GRP * r, D_MODEL), F32)],
        compiler_params=_params(("parallel", "parallel", "arbitrary")),
        name="dft_stage2",
    )(c1, s1, y_r, y_i, proj4, w_out, x4)


def _fourier_long(x2d, g, wc, w_out, bsz, seq):
    r = math.isqrt(seq)
    assert r * r == seq and r % C_GRP == 0
    x4 = x2d.reshape(bsz, r, r, D_MODEL)
    proj4 = _inproj_perm(x4, g, wc)
    idx = jnp.arange(r, dtype=jnp.int32)
    tok = idx[:, None, None] + r * idx[None, None, :]
    dc, ds = _trig(idx[None, :, None] * tok, seq, r ** -0.5)
    dtab = jnp.concatenate([dc, ds], axis=1).astype(BF16)
    y_r, y_i = _dft_stage1(proj4, dtab)
    c1, s1 = _trig(idx[:, None] * idx[None, :], r, r ** -0.5)
    out = _dft_stage2(c1.astype(BF16), s1.astype(BF16), y_r, y_i, proj4, w_out, x4)
    return out.reshape(bsz * seq, D_MODEL)


OUT_TM = 512


def _outproj_kernel(m_ref, x_ref, w_ref, o_ref):
    o_ref[...] = x_ref[...] + jnp.dot(m_ref[...], w_ref[...], preferred_element_type=F32)


def _outproj(mixed2d, x2d, w_out):
    t = x2d.shape[0]
    return pl.pallas_call(
        _outproj_kernel,
        grid=(t // OUT_TM,),
        in_specs=[
            pl.BlockSpec((OUT_TM, BRANCH), lambda i: (i, 0)),
            pl.BlockSpec((OUT_TM, D_MODEL), lambda i: (i, 0)),
            pl.BlockSpec((BRANCH, D_MODEL), lambda i: (0, 0)),
        ],
        out_specs=pl.BlockSpec((OUT_TM, D_MODEL), lambda i: (i, 0)),
        out_shape=jax.ShapeDtypeStruct((t, D_MODEL), F32),
        compiler_params=_params(("parallel",)),
        name="outproj",
    )(mixed2d, x2d, w_out)


def _rope_tables_a(seq):
    half = A_ROT // 2
    inv_freq = jnp.exp(-(jnp.arange(half, dtype=F32) * (2.0 / A_ROT)) * math.log(A_THETA))
    ang = jnp.arange(seq, dtype=F32)[:, None] * inv_freq[None, :]
    cos, sin = jnp.cos(ang), jnp.sin(ang)
    d = jnp.arange(LANES) % A_HEAD
    f = d % half
    cm = jnp.where(d[None, :] < A_ROT, cos[:, f], 1.0)
    s1 = jnp.where(d[None, :] < half, -sin[:, f], 0.0)
    s2 = jnp.where((d[None, :] >= half) & (d[None, :] < A_ROT), sin[:, f], 0.0)
    qs = A_HEAD ** -0.5 * LOG2E
    return jnp.stack([cm * qs, s1 * qs, s2 * qs, cm, s1, s2]).astype(F32)


def _rope_tables_b(seq):
    half = B_QK // 2
    inv_freq = jnp.exp(-(jnp.arange(half, dtype=F32) * (2.0 / B_QK)) * math.log(B_THETA))
    ang = jnp.arange(seq, dtype=F32)[:, None] * inv_freq[None, :]
    cos, sin = jnp.cos(ang), jnp.sin(ang)
    ks = B_QK ** -0.5
    return jnp.stack([cos, sin, cos * ks, sin * ks]).astype(F32)


def _weights_a(w_in):
    nq = A_QH * A_HEAD
    nkv = A_KVH * A_HEAD
    q = w_in[:, :nq]
    k = w_in[:, nq:nq + nkv].reshape(D_MODEL, A_KVH, 1, A_HEAD)
    v = w_in[:, nq + nkv:nq + 2 * nkv]
    gate = w_in[:, nq + 2 * nkv:]
    kk = jnp.broadcast_to(k, (D_MODEL, A_KVH, 2, A_HEAD)).reshape(D_MODEL, 2 * nkv)
    return jnp.concatenate([q, gate, kk, v], axis=1).astype(BF16)


def _weights_c(w_in):
    m = jnp.arange(C_GDIM, dtype=jnp.int32)
    cm, sm = _trig(m[:, None] * m[None, :], C_GDIM, C_GDIM ** -0.5)
    w_a, w_b = _fold_channel_dft(w_in[:, :BRANCH], jnp.concatenate([cm, sm], axis=1))
    return jnp.concatenate([w_a, w_b, w_in[:, BRANCH:].astype(BF16)], axis=1)


def _trunk(x, norm_g, fin_g, wa, a_sink, a_w_out, wb, log_g, b_w_out, wc, c_w_out):
    bsz, seq, _ = x.shape
    t = bsz * seq
    x2d = x.reshape(t, D_MODEL)
    tabs_a = _rope_tables_a(seq)
    tabs_b = _rope_tables_b(seq)
    fin = fin_g.reshape(1, D_MODEL)

    def layer_a(x2d, layer, j, final):
        q3, gate3, kk, vt = _inproj_a(x2d, norm_g[layer].reshape(1, D_MODEL), wa[j], tabs_a, bsz, seq)
        return _attention(x2d, q3, gate3, kk, vt, a_sink[j], a_w_out[j], fin, bsz, seq, final)

    x2d = layer_a(x2d, 0, 0, False)

    proj = _inproj(x2d, norm_g[1].reshape(1, D_MODEL), wb, tabs_b, seq, 2 * B_HEADS)
    x2d = _retention(proj.reshape(bsz, seq, -1), log_g, x2d.reshape(bsz, seq, D_MODEL),
                     b_w_out).reshape(t, D_MODEL)

    if seq <= C_DIRECT_MAX:
        proj = _inproj(x2d, norm_g[2].reshape(1, D_MODEL), wc, None, seq, 0)
        mixed = _fourier_mix(proj.reshape(bsz, seq, -1))
        x2d = _outproj(mixed.reshape(t, BRANCH), x2d, c_w_out)
    else:
        x2d = _fourier_long(x2d, norm_g[2].reshape(1, D_MODEL), wc, c_w_out, bsz, seq)

    x2d = layer_a(x2d, 3, 1, True)
    return x2d.reshape(bsz, seq, D_MODEL)


def kernel(x_prompt, x_sample, norm_g, final_norm_g, a_w_in, a_sink, a_w_out, b_w_in, b_decay,
           b_w_out, c_w_in, c_w_out):
    wa = [_weights_a(a_w_in[j]) for j in range(a_w_in.shape[0])]
    a_out = [a_w_out[j].astype(BF16) for j in range(a_w_out.shape[0])]
    wb = b_w_in[0].astype(BF16)
    log_g = jax.nn.log_sigmoid(b_decay[0].astype(F32))
    wc = _weights_c(c_w_in[0])
    args = (norm_g, final_norm_g, wa, a_sink, a_out, wb, log_g, b_w_out[0].astype(BF16),
            wc, c_w_out[0].astype(BF16))
    return (_trunk(x_prompt, *args), _trunk(x_sample, *args))
```

```python
import functools
import math

import jax
import jax.numpy as jnp
from jax import lax
from jax.experimental import pallas as pl
from jax.experimental.pallas import tpu as pltpu

D_MODEL = 1024
BRANCH = 2048
NORM_EPS = 1e-6

A_HEAD = 64
A_QH = 32
A_KVH = 4
A_ROT = 16
A_THETA = 500000.0
A_WIN = 128
A_QBLK = 256
A_SPAN = A_QBLK + 2 * A_WIN
A_PAIRS = A_QH // 2
A_LOOKAHEAD = 3
A_VPAD = 16
LOG2E = math.log2(math.e)

B_QK = 256
B_HEADS = 4
B_V = 512
B_CHUNK = 256
B_THETA = 10000.0

C_GROUPS = 4
C_GDIM = 512
C_DIRECT_MAX = 2048

LANES = 128
VMEM_LIMIT = 56 * 1024 * 1024

BF16 = jnp.bfloat16
F32 = jnp.float32


def _params(sem):
    return pltpu.CompilerParams(dimension_semantics=sem, vmem_limit_bytes=VMEM_LIMIT)


def _silu(x):
    return x * jax.nn.sigmoid(x)


def _rms(x, g):
    ms = jnp.mean(x * x, axis=-1, keepdims=True)
    return (x * lax.rsqrt(ms + NORM_EPS)) * g


IN_TM = 512
IN_TN = 512


def _inproj_a_kernel(x_ref, g_ref, w_ref, tab_ref, q_ref, gate_ref, kk_ref, vt_ref):
    h = _rms(x_ref[...], g_ref[...]).astype(BF16)

    def rope(a, t0):
        return (a * tab_ref[t0] + pltpu.roll(a, LANES - A_ROT // 2, 1) * tab_ref[t0 + 1]
                + pltpu.roll(a, A_ROT // 2, 1) * tab_ref[t0 + 2])

    for c in range(4):
        acc = jnp.dot(h, w_ref[:, c * IN_TN:(c + 1) * IN_TN], preferred_element_type=F32)
        for s in range(4):
            q_ref[4 * c + s] = rope(acc[:, s * LANES:(s + 1) * LANES], 0).astype(BF16)
    for c in range(4):
        acc = jnp.dot(h, w_ref[:, BRANCH + c * IN_TN:BRANCH + (c + 1) * IN_TN],
                      preferred_element_type=F32)
        for s in range(4):
            gate_ref[4 * c + s] = acc[:, s * LANES:(s + 1) * LANES].astype(BF16)
    acc = jnp.dot(h, w_ref[:, 2 * BRANCH:2 * BRANCH + IN_TN], preferred_element_type=F32)
    for s in range(A_KVH):
        kk_ref[s] = rope(acc[:, s * LANES:(s + 1) * LANES], 3).astype(BF16)
    acc = jnp.dot(h, w_ref[:, 2 * BRANCH + IN_TN:], preferred_element_type=F32)
    vt_ref[0] = acc.T.astype(BF16)


def _inproj_a(x2d, g, w, tabs, bsz, seq):
    t = x2d.shape[0]
    nblk = seq // IN_TM
    n = w.shape[1]
    return pl.pallas_call(
        _inproj_a_kernel,
        grid=(t // IN_TM,),
        in_specs=[
            pl.BlockSpec((IN_TM, D_MODEL), lambda i: (i, 0)),
            pl.BlockSpec((1, D_MODEL), lambda i: (0, 0)),
            pl.BlockSpec((D_MODEL, n), lambda i: (0, 0)),
            pl.BlockSpec((6, IN_TM, LANES), lambda i: (0, i % nblk, 0)),
        ],
        out_specs=[
            pl.BlockSpec((A_PAIRS, IN_TM, LANES), lambda i: (0, i, 0)),
            pl.BlockSpec((A_PAIRS, IN_TM, LANES), lambda i: (0, i, 0)),
            pl.BlockSpec((A_KVH, IN_TM, LANES), lambda i: (0, i, 0)),
            pl.BlockSpec((1, A_KVH * A_HEAD, IN_TM), lambda i: (i // nblk, 0, i % nblk)),
        ],
        out_shape=[
            jax.ShapeDtypeStruct((A_PAIRS, t, LANES), BF16),
            jax.ShapeDtypeStruct((A_PAIRS, t, LANES), BF16),
            jax.ShapeDtypeStruct((A_KVH, t, LANES), BF16),
            jax.ShapeDtypeStruct((bsz, A_KVH * A_HEAD, seq), BF16),
        ],
        compiler_params=_params(("parallel",)),
        name="inproj_a",
    )(x2d, g, w, tabs)


def _inproj_kernel(x_ref, g_ref, w_ref, *rest, rope_heads):
    if rope_heads:
        tab_ref, o_ref = rest
    else:
        (o_ref,) = rest
    h = _rms(x_ref[...], g_ref[...]).astype(BF16)
    n = w_ref.shape[1]
    for c0 in range(0, n, IN_TN):
        width = min(IN_TN, n - c0)
        acc = jnp.dot(h, w_ref[:, c0:c0 + width], preferred_element_type=F32)
        for lo in range(0, width, B_QK):
            head = (c0 + lo) // B_QK
            if head < rope_heads:
                t0 = 0 if head < rope_heads // 2 else 2
                cos, sin = tab_ref[t0], tab_ref[t0 + 1]
                x1 = acc[:, lo:lo + LANES]
                x2 = acc[:, lo + LANES:lo + B_QK]
                o_ref[:, c0 + lo:c0 + lo + LANES] = (x1 * cos - x2 * sin).astype(BF16)
                o_ref[:, c0 + lo + LANES:c0 + lo + B_QK] = (x2 * cos + x1 * sin).astype(BF16)
            else:
                o_ref[:, c0 + lo:c0 + lo + B_QK] = acc[:, lo:lo + B_QK].astype(BF16)


def _inproj(x2d, g, w, tabs, seq, rope_heads):
    t = x2d.shape[0]
    n = w.shape[1]
    nblk = seq // IN_TM
    in_specs = [
        pl.BlockSpec((IN_TM, D_MODEL), lambda i: (i, 0)),
        pl.BlockSpec((1, D_MODEL), lambda i: (0, 0)),
        pl.BlockSpec((D_MODEL, n), lambda i: (0, 0)),
    ]
    args = [x2d, g, w]
    if rope_heads:
        in_specs.append(pl.BlockSpec((4, IN_TM, LANES), lambda i: (0, i % nblk, 0)))
        args.append(tabs)
    return pl.pallas_call(
        functools.partial(_inproj_kernel, rope_heads=rope_heads),
        grid=(t // IN_TM,),
        in_specs=in_specs,
        out_specs=pl.BlockSpec((IN_TM, n), lambda i: (i, 0)),
        out_shape=jax.ShapeDtypeStruct((t, n), BF16),
        compiler_params=_params(("parallel",)),
        name="inproj",
    )(*args)


def _attn_kernel(sink_ref, q_ref, gate_ref, kp_ref, kc_ref, kn_ref, vp_ref, vc_ref, vn_ref,
                 x_ref, w_ref, fg_ref, o_ref, ka_ref, kb_ref, v_ref, bias_ref, *, final):
    i = pl.program_id(1)
    last = pl.num_programs(1) - 1
    rows = A_SPAN - A_WIN
    per_kv = A_PAIRS // A_KVH

    lane = lax.broadcasted_iota(jnp.int32, (A_SPAN, LANES), 1)
    ones_row = (lax.broadcasted_iota(jnp.int32, (A_VPAD, A_SPAN), 0) == 0).astype(BF16)
    for h in range(A_KVH):
        span = jnp.concatenate([kp_ref[h], kc_ref[h], kn_ref[h]], axis=0)
        ka_ref[h] = jnp.where(lane < A_HEAD, span, jnp.zeros_like(span))
        kb_ref[h] = jnp.where(lane >= A_HEAD, span, jnp.zeros_like(span))
        v_ref[h, :A_HEAD, :] = jnp.concatenate([vp_ref[0, h], vc_ref[0, h], vn_ref[0, h]], axis=1)
        v_ref[h, A_HEAD:, :] = ones_row

    r = lax.broadcasted_iota(jnp.int32, (A_WIN, LANES), 0)
    c = lax.broadcasted_iota(jnp.int32, (A_WIN, LANES), 1)
    neg = jnp.full((A_WIN, LANES), -1e30, F32)
    top = jnp.where(r >= c, 0.0, neg)
    bot = jnp.where(r <= c, 0.0, neg)
    bias_ref[0] = jnp.where(i == 0, neg, top)
    bias_ref[1] = bot
    bias_ref[2] = top
    bias_ref[3] = jnp.where(i == last, neg, bot)
    pad = jnp.zeros((A_WIN, LANES), BF16)

    units = [(g, jl, a) for g in range(A_KVH) for jl in range(per_kv) for a in range(2)]

    def scores(n):
        g, jl, a = units[n]
        k = (ka_ref, kb_ref)[a][g]
        return lax.dot_general(k, q_ref[g * per_kv + jl], (((1,), (1,)), ((), ())),
                               preferred_element_type=F32)

    def softmax_pv(s_t, sink, g):
        ps, extra = [], []
        for jj in range(2):
            blk = s_t[jj * A_WIN:jj * A_WIN + rows, jj * LANES:(jj + 1) * LANES]
            parts = [blk[:A_WIN] + bias_ref[2 * jj], blk[A_WIN:2 * A_WIN],
                     blk[2 * A_WIN:] + bias_ref[2 * jj + 1]]
            mx = jnp.full((1, LANES), sink, F32)
            for part in parts:
                mx = jnp.maximum(mx, jnp.max(part, axis=0, keepdims=True))
            es = [jnp.exp2(part - mx).astype(BF16) for part in parts]
            ps.append(jnp.concatenate(es + [pad] if jj == 0 else [pad] + es, axis=0))
            extra.append(jnp.exp2(sink - mx))
        o_ext = jnp.dot(v_ref[g], jnp.concatenate(ps, axis=1), preferred_element_type=F32)
        den = o_ext[A_HEAD:A_HEAD + 1, :] + jnp.concatenate(extra, axis=1)
        return o_ext[:A_HEAD, :] / den

    pending = [scores(n) for n in range(A_LOOKAHEAD)]
    outs, gated = [], []
    y = x_ref[...]
    for n, (g, jl, a) in enumerate(units):
        if n + A_LOOKAHEAD < len(units):
            pending.append(scores(n + A_LOOKAHEAD))
        j = g * per_kv + jl
        outs.append(softmax_pv(pending.pop(0), sink_ref[2 * j + a] * LOG2E, g))
        if a == 1:
            o_pair = jnp.concatenate(outs[-2:], axis=0).T
            gt = gate_ref[j].astype(F32)
            gated.append((o_pair * _silu(gt)).astype(BF16))
            if jl == per_kv - 1:
                wg = w_ref[g * per_kv * LANES:(g + 1) * per_kv * LANES, :]
                y = y + jnp.dot(jnp.concatenate(gated[-per_kv:], axis=1), wg,
                                preferred_element_type=F32)
    if final:
        y = _rms(y, fg_ref[...])
    o_ref[...] = y


def _attention(x2d, q3, gate3, kk, vt, sink, w_out, fin_g, bsz, seq, final):
    t = x2d.shape[0]
    nqb = seq // A_QBLK
    nkb = seq // A_WIN
    vt4 = vt.reshape(bsz, A_KVH, A_HEAD, seq)
    grid_spec = pltpu.PrefetchScalarGridSpec(
        num_scalar_prefetch=1,
        grid=(bsz, nqb),
        in_specs=[
            pl.BlockSpec((A_PAIRS, A_QBLK, LANES), lambda b, i, s: (0, b * nqb + i, 0)),
            pl.BlockSpec((A_PAIRS, A_QBLK, LANES), lambda b, i, s: (0, b * nqb + i, 0)),
            pl.BlockSpec((A_KVH, A_WIN, LANES),
                         lambda b, i, s: (0, b * nkb + jnp.maximum(2 * i - 1, 0), 0)),
            pl.BlockSpec((A_KVH, A_QBLK, LANES), lambda b, i, s: (0, b * nqb + i, 0)),
            pl.BlockSpec((A_KVH, A_WIN, LANES),
                         lambda b, i, s: (0, b * nkb + jnp.minimum(2 * i + 2, nkb - 1), 0)),
            pl.BlockSpec((1, A_KVH, A_HEAD, A_WIN),
                         lambda b, i, s: (b, 0, 0, jnp.maximum(2 * i - 1, 0))),
            pl.BlockSpec((1, A_KVH, A_HEAD, A_QBLK), lambda b, i, s: (b, 0, 0, i)),
            pl.BlockSpec((1, A_KVH, A_HEAD, A_WIN),
                         lambda b, i, s: (b, 0, 0, jnp.minimum(2 * i + 2, nkb - 1))),
            pl.BlockSpec((A_QBLK, D_MODEL), lambda b, i, s: (b * nqb + i, 0)),
            pl.BlockSpec((BRANCH, D_MODEL), lambda b, i, s: (0, 0)),
            pl.BlockSpec((1, D_MODEL), lambda b, i, s: (0, 0)),
        ],
        out_specs=pl.BlockSpec((A_QBLK, D_MODEL), lambda b, i, s: (b * nqb + i, 0)),
        scratch_shapes=[
            pltpu.VMEM((A_KVH, A_SPAN, LANES), BF16),
            pltpu.VMEM((A_KVH, A_SPAN, LANES), BF16),
            pltpu.VMEM((A_KVH, A_HEAD + A_VPAD, A_SPAN), BF16),
            pltpu.VMEM((4, A_WIN, LANES), F32),
        ],
    )
    return pl.pallas_call(
        functools.partial(_attn_kernel, final=final),
        grid_spec=grid_spec,
        out_shape=jax.ShapeDtypeStruct((t, D_MODEL), F32),
        compiler_params=_params(("parallel", "parallel")),
        name="attention",
    )(sink, q3, gate3, kk, kk, kk, vt4, vt4, vt4, x2d, w_out, fin_g)


def _retention_heads(lg_ref, q_ref, k_ref, v_ref, st_ref, backward):
    n = B_CHUNK
    ii = lax.broadcasted_iota(jnp.int32, (n, n), 0)
    jj = lax.broadcasted_iota(jnp.int32, (n, n), 1)
    col = lax.broadcasted_iota(jnp.int32, (n, 1), 0).astype(F32)
    d = 1 if backward else 0
    heads = range(B_HEADS)
    qs = [q_ref[0, :, h * B_QK:(h + 1) * B_QK] for h in heads]
    ks = [k_ref[0, :, h * B_QK:(h + 1) * B_QK] for h in heads]
    vs = [v_ref[0, :, h * B_V:(h + 1) * B_V] for h in heads]
    sc = [lax.dot_general(qs[h], ks[h], (((1,), (1,)), ((), ())), preferred_element_type=F32)
          for h in heads]
    cross = [jnp.dot(qs[h], st_ref[h].astype(BF16), preferred_element_type=F32) for h in heads]
    outs = []
    for h in heads:
        lg = lg_ref[d, h]
        if backward:
            mask = jj > ii
            dist = (jj - ii).astype(F32)
            q_dec = jnp.exp((n - col) * lg)
            k_dec = jnp.exp(col * lg)
        else:
            mask = ii >= jj
            dist = (ii - jj).astype(F32)
            q_dec = jnp.exp((col + 1.0) * lg)
            k_dec = jnp.exp((n - 1.0 - col) * lg)
        intra = jnp.where(mask, jnp.exp(jnp.where(mask, dist, 0.0) * lg), 0.0)
        c_dec = jnp.exp(jnp.full((1, B_V), n * lg, F32))
        kd_t = (ks[h].astype(F32) * k_dec).T.astype(BF16)
        st_ref[h] = st_ref[h] * c_dec + jnp.dot(kd_t, vs[h], preferred_element_type=F32)
        inner = jnp.dot((sc[h] * intra).astype(BF16), vs[h], preferred_element_type=F32)
        outs.append(inner + cross[h] * q_dec)
    return outs


def _retention_bwd_kernel(lg_ref, q_ref, k_ref, v_ref, o_ref, st_ref):
    @pl.when(pl.program_id(1) == 0)
    def _():
        st_ref[...] = jnp.zeros_like(st_ref)

    outs = _retention_heads(lg_ref, q_ref, k_ref, v_ref, st_ref, True)
    for h, o in enumerate(outs):
        o_ref[0, :, h * B_V:(h + 1) * B_V] = o.astype(BF16)


def _retention_fwd_kernel(lg_ref, q_ref, k_ref, v_ref, gate_ref, ob_ref, x_ref, w_ref, o_ref, st_ref):
    @pl.when(pl.program_id(1) == 0)
    def _():
        st_ref[...] = jnp.zeros_like(st_ref)

    outs = _retention_heads(lg_ref, q_ref, k_ref, v_ref, st_ref, False)
    parts = []
    for h, o_f in enumerate(outs):
        sl = slice(h * B_V, (h + 1) * B_V)
        o = o_f + ob_ref[0, :, sl].astype(F32)
        on = o * lax.rsqrt(jnp.mean(o * o, axis=-1, keepdims=True) + NORM_EPS)
        parts.append((on * _silu(gate_ref[0, :, sl].astype(F32))).astype(BF16))
    gated = jnp.concatenate(parts, axis=1)
    o_ref[0] = x_ref[0] + jnp.dot(gated, w_ref[...], preferred_element_type=F32)


def _retention(proj3, log_g, x3, w_out):
    bsz, seq, _ = proj3.shape
    nc = seq // B_CHUNK
    nqk = B_HEADS * B_QK
    smem = pl.BlockSpec(memory_space=pltpu.SMEM)
    state = [pltpu.VMEM((B_HEADS, B_QK, B_V), F32)]

    def specs(chunk):
        return [
            pl.BlockSpec((1, B_CHUNK, nqk), lambda b, t: (b, chunk(t), 0)),
            pl.BlockSpec((1, B_CHUNK, nqk), lambda b, t: (b, chunk(t), 1)),
            pl.BlockSpec((1, B_CHUNK, BRANCH), lambda b, t: (b, chunk(t), 1)),
        ]

    def rev(t):
        return nc - 1 - t

    def fwd(t):
        return t

    o_b = pl.pallas_call(
        _retention_bwd_kernel,
        grid=(bsz, nc),
        in_specs=[smem] + specs(rev),
        out_specs=pl.BlockSpec((1, B_CHUNK, BRANCH), lambda b, t: (b, rev(t), 0)),
        out_shape=jax.ShapeDtypeStruct((bsz, seq, BRANCH), BF16),
        scratch_shapes=state,
        compiler_params=_params(("parallel", "arbitrary")),
        name="retention_bwd",
    )(log_g, proj3, proj3, proj3)
    return pl.pallas_call(
        _retention_fwd_kernel,
        grid=(bsz, nc),
        in_specs=[smem] + specs(fwd) + [
            pl.BlockSpec((1, B_CHUNK, BRANCH), lambda b, t: (b, t, 2)),
            pl.BlockSpec((1, B_CHUNK, BRANCH), lambda b, t: (b, t, 0)),
            pl.BlockSpec((1, B_CHUNK, D_MODEL), lambda b, t: (b, t, 0)),
            pl.BlockSpec((BRANCH, D_MODEL), lambda b, t: (0, 0)),
        ],
        out_specs=pl.BlockSpec((1, B_CHUNK, D_MODEL), lambda b, t: (b, t, 0)),
        out_shape=jax.ShapeDtypeStruct((bsz, seq, D_MODEL), F32),
        scratch_shapes=state,
        compiler_params=_params(("parallel", "arbitrary")),
        name="retention_fwd",
    )(log_g, proj3, proj3, proj3, proj3, o_b, x3, w_out)


C_HALF = C_GDIM // 2
C_MAIN = C_GROUPS * C_HALF
C_FOLD = 2 * C_HALF + LANES


def _fold_kernel(w_ref, cs_ref, a_ref, b_ref, sp_ref):
    r = jnp.dot(w_ref[...], cs_ref[0], preferred_element_type=F32,
                precision=lax.Precision.HIGHEST)
    a_ref[...] = r[:, :C_HALF].astype(BF16)
    b_ref[...] = r[:, C_HALF:2 * C_HALF].astype(BF16)

    @pl.when(pl.program_id(0) == 0)
    def _():
        sp_ref[...] = jnp.zeros_like(sp_ref)

    sp_ref[...] += r[:, 2 * C_HALF:]


def _fold_channel_dft(w_u, cs):
    spec = pl.BlockSpec((D_MODEL, C_HALF), lambda g: (0, g))
    return pl.pallas_call(
        _fold_kernel,
        grid=(C_GROUPS,),
        in_specs=[pl.BlockSpec((D_MODEL, C_GDIM), lambda g: (0, g)),
                  pl.BlockSpec((1, C_GDIM, C_FOLD), lambda g: (g, 0, 0))],
        out_specs=[spec, spec, pl.BlockSpec((D_MODEL, LANES), lambda g: (0, 0))],
        out_shape=[jax.ShapeDtypeStruct((D_MODEL, C_MAIN), BF16)] * 2
        + [jax.ShapeDtypeStruct((D_MODEL, LANES), F32)],
        compiler_params=_params(("arbitrary",)),
        name="fold_channel_dft",
    )(w_u, cs)


DFT_ROWS = 512


def _dft_mix_kernel(cm_ref, sm_ref, a_ref, b_ref, gm_ref, gp_ref, sp_ref, gsp_ref,
                    om_ref, op_ref, osp_ref):
    n = cm_ref.shape[0]
    rc = min(n, DFT_ROWS)
    a = a_ref[0]
    b = b_ref[0]
    for r in range(n // rc):
        rows = slice(r * rc, (r + 1) * rc)
        ea = jnp.dot(cm_ref[rows, :], a, preferred_element_type=F32)
        eb = jnp.dot(sm_ref[rows, :], b, preferred_element_type=F32)
        om_ref[0, rows, :] = ((ea - eb) * _silu(gm_ref[0, rows, :].astype(F32))).astype(BF16)
        op_ref[0, rows, :] = ((ea + eb) * _silu(gp_ref[0, rows, :].astype(F32))).astype(BF16)

    @pl.when(pl.program_id(1) == 0)
    def _():
        es = jnp.dot(cm_ref[...], sp_ref[0], preferred_element_type=F32)
        osp_ref[0] = (es * _silu(gsp_ref[0].astype(F32))).astype(BF16)


def _trig(m, period, scale):
    ang = (2.0 * math.pi / period) * (m % period).astype(F32)
    return jnp.cos(ang) * scale, jnp.sin(ang) * scale


def _fourier_mix(proj3):
    bsz, seq, _ = proj3.shape
    idx = jnp.arange(seq, dtype=jnp.int32)
    cm, sm = _trig(idx[:, None] * idx[None, :], seq, seq ** -0.5)
    tc = 512
    nct = C_MAIN // tc
    mspec = pl.BlockSpec((seq, seq), lambda b, c: (0, 0))

    def dspec(width, idx):
        return pl.BlockSpec((1, seq, width), lambda b, c: (b, 0, idx(c)))

    sp_blk = 4 * C_MAIN // LANES
    return pl.pallas_call(
        _dft_mix_kernel,
        grid=(bsz, nct),
        in_specs=[mspec, mspec,
                  dspec(tc, lambda c: c), dspec(tc, lambda c: nct + c),
                  dspec(tc, lambda c: 2 * nct + c), dspec(tc, lambda c: 3 * nct + c),
                  dspec(LANES, lambda c: sp_blk), dspec(LANES, lambda c: sp_blk + 1)],
        out_specs=[dspec(tc, lambda c: c), dspec(tc, lambda c: c), dspec(LANES, lambda c: 0)],
        out_shape=[jax.ShapeDtypeStruct((bsz, seq, C_MAIN), BF16)] * 2
        + [jax.ShapeDtypeStruct((bsz, seq, LANES), BF16)],
        compiler_params=_params(("parallel", "arbitrary")),
        name="dft_mix",
    )(cm.astype(BF16), sm.astype(BF16), proj3, proj3, proj3, proj3, proj3, proj3)


C_GRP = 8
C_SUB = 64
C_ROWS = 256
C_NCHUNK = 1024


def _inproj_perm_kernel(x_ref, g_ref, w_ref, o_ref, xs_ref):
    rh = x_ref.shape[1]
    n = w_ref.shape[1]
    for i in range(C_GRP):
        xs_ref[i * rh:(i + 1) * rh, :] = x_ref[0, :, i, :]
    per_dot = C_ROWS // rh
    for j in range(C_GRP // per_dot):
        h = _rms(xs_ref[j * C_ROWS:(j + 1) * C_ROWS, :], g_ref[...]).astype(BF16)
        for c0 in range(0, n, C_NCHUNK):
            cols = slice(c0, min(c0 + C_NCHUNK, n))
            acc = jnp.dot(h, w_ref[:, cols], preferred_element_type=F32)
            for i in range(per_dot):
                o_ref[0, j * per_dot + i, :, cols] = acc[i * rh:(i + 1) * rh, :].astype(BF16)


def _inproj_perm(x4, g, w):
    bsz, r, _, _ = x4.shape
    n = w.shape[1]
    rh = min(r, C_SUB)
    return pl.pallas_call(
        _inproj_perm_kernel,
        grid=(bsz, r // C_GRP, r // rh),
        in_specs=[
            pl.BlockSpec((1, rh, C_GRP, D_MODEL), lambda b, m, s: (b, s, m, 0)),
            pl.BlockSpec((1, D_MODEL), lambda b, m, s: (0, 0)),
            pl.BlockSpec((D_MODEL, n), lambda b, m, s: (0, 0)),
        ],
        out_specs=pl.BlockSpec((1, C_GRP, rh, n), lambda b, m, s: (b, m, s, 0)),
        out_shape=jax.ShapeDtypeStruct((bsz, r, r, n), BF16),
        scratch_shapes=[pltpu.VMEM((C_GRP * rh, D_MODEL), F32)],
        compiler_params=_params(("parallel", "parallel", "parallel")),
        name="inproj_perm",
    )(x4, g, w)


def _dft_stage1_kernel(d_ref, x_ref, sp_ref, yr_ref, yi_ref, ysr_ref, ysi_ref):
    r = x_ref.shape[2]
    for i in range(C_GRP):
        dc = d_ref[i, :r, :]
        ds = d_ref[i, r:, :]
        yr_ref[0, :, i, :] = jnp.dot(dc, x_ref[0, i], preferred_element_type=F32)
        yi_ref[0, :, i, :] = jnp.dot(ds, x_ref[0, i], preferred_element_type=F32)
        ysr_ref[0, :, i, :] = jnp.dot(dc, sp_ref[0, i], preferred_element_type=F32)
        ysi_ref[0, :, i, :] = jnp.dot(ds, sp_ref[0, i], preferred_element_type=F32)


def _dft_stage1(proj4, dtab):
    bsz, r, _, _ = proj4.shape
    tc = 1024
    nct = 2 * C_MAIN // tc

    def yspec(width, idx):
        return pl.BlockSpec((1, r, C_GRP, width), lambda b, m, c: (b, 0, m, idx(c)))

    return pl.pallas_call(
        _dft_stage1_kernel,
        grid=(bsz, r // C_GRP, nct),
        in_specs=[
            pl.BlockSpec((C_GRP, 2 * r, r), lambda b, m, c: (m, 0, 0)),
            pl.BlockSpec((1, C_GRP, r, tc), lambda b, m, c: (b, m, 0, c)),
            pl.BlockSpec((1, C_GRP, r, LANES), lambda b, m, c: (b, m, 0, 4 * C_MAIN // LANES)),
        ],
        out_specs=[yspec(tc, lambda c: c), yspec(tc, lambda c: c),
                   yspec(LANES, lambda c: 0), yspec(LANES, lambda c: 0)],
        out_shape=[jax.ShapeDtypeStruct((bsz, r, r, 2 * C_MAIN), F32)] * 2
        + [jax.ShapeDtypeStruct((bsz, r, r, LANES), F32)] * 2,
        compiler_params=_params(("parallel", "parallel", "arbitrary")),
        name="dft_stage1",
    )(dtab, proj4, proj4)


def _dft_stage2_kernel(c1_ref, s1_ref, yra_ref, yia_ref, yrb_ref, yib_ref, gm_ref, gp_ref,
                       wm_ref, wp_ref, ysr_ref, ysi_ref, gsp_ref, wsp_ref, x_ref, o_ref, acc_ref):
    c = pl.program_id(2)
    r = c1_ref.shape[0]
    c1 = c1_ref[...]
    s1 = s1_ref[...]

    def dot(m, y):
        return jnp.dot(m, y.astype(BF16), preferred_element_type=F32)

    @pl.when(c == 0)
    def _():
        parts = []
        for i in range(C_GRP):
            es = dot(c1, ysr_ref[0, i]) - dot(s1, ysi_ref[0, i])
            parts.append((es * _silu(gsp_ref[0, i].astype(F32))).astype(BF16))
        nyq = jnp.dot(jnp.concatenate(parts, axis=0), wsp_ref[...], preferred_element_type=F32)
        for i in range(C_GRP):
            acc_ref[i * r:(i + 1) * r, :] = x_ref[0, :, i, :] + nyq[i * r:(i + 1) * r, :]

    minus, plus = [], []
    for i in range(C_GRP):
        ea = dot(c1, yra_ref[0, i]) - dot(s1, yia_ref[0, i])
        eb = dot(c1, yib_ref[0, i]) + dot(s1, yrb_ref[0, i])
        minus.append(((ea - eb) * _silu(gm_ref[0, i].astype(F32))).astype(BF16))
        plus.append(((ea + eb) * _silu(gp_ref[0, i].astype(F32))).astype(BF16))
    acc_ref[...] += (jnp.dot(jnp.concatenate(minus, axis=0), wm_ref[...], preferred_element_type=F32)
                     + jnp.dot(jnp.concatenate(plus, axis=0), wp_ref[...], preferred_element_type=F32))

    @pl.when(c == pl.num_programs(2) - 1)
    def _():
        for i in range(C_GRP):
            o_ref[0, :, i, :] = acc_ref[i * r:(i + 1) * r, :]


def _dft_stage2(c1, s1, y_r, y_i, y_sr, y_si, proj4, w_out, x4):
    bsz, r, _, _ = x4.shape
    tc = 512
    nct = C_MAIN // tc
    mspec = pl.BlockSpec((r, r), lambda b, m, c: (0, 0))

    def slab(width, idx):
        return pl.BlockSpec((1, C_GRP, r, width), lambda b, m, c: (b, m, 0, idx(c)))

    def wspec(rows, idx):
        return pl.BlockSpec((rows, D_MODEL), lambda b, m, c: (idx(c), 0))

    xspec = pl.BlockSpec((1, r, C_GRP, D_MODEL), lambda b, m, c: (b, 0, m, 0))
    sp_blk = 4 * C_MAIN // LANES
    return pl.pallas_call(
        _dft_stage2_kernel,
        grid=(bsz, r // C_GRP, nct),
        in_specs=[
            mspec, mspec,
            slab(tc, lambda c: c), slab(tc, lambda c: c),
            slab(tc, lambda c: nct + c), slab(tc, lambda c: nct + c),
            slab(tc, lambda c: 2 * nct + c), slab(tc, lambda c: 3 * nct + c),
            wspec(tc, lambda c: c), wspec(tc, lambda c: nct + c),
            slab(LANES, lambda c: 0), slab(LANES, lambda c: 0),
            slab(LANES, lambda c: sp_blk + 1),
            wspec(LANES, lambda c: 2 * C_MAIN // LANES),
            xspec,
        ],
        out_specs=xspec,
        out_shape=jax.ShapeDtypeStruct(x4.shape, F32),
        scratch_shapes=[pltpu.VMEM((C_GRP * r, D_MODEL), F32)],
        compiler_params=_params(("parallel", "parallel", "arbitrary")),
        name="dft_stage2",
    )(c1, s1, y_r, y_i, y_r, y_i, proj4, proj4, w_out, w_out, y_sr, y_si, proj4, w_out, x4)


def _fourier_long(x2d, g, wc, w_out, bsz, seq):
    r = math.isqrt(seq)
    assert r * r == seq and r % C_GRP == 0
    x4 = x2d.reshape(bsz, r, r, D_MODEL)
    proj4 = _inproj_perm(x4, g, wc)
    idx = jnp.arange(r, dtype=jnp.int32)
    tok = idx[:, None, None] + r * idx[None, None, :]
    dc, ds = _trig(idx[None, :, None] * tok, seq, r ** -0.5)
    dtab = jnp.concatenate([dc, ds], axis=1).astype(BF16)
    y_r, y_i, y_sr, y_si = _dft_stage1(proj4, dtab)
    c1, s1 = _trig(idx[:, None] * idx[None, :], r, r ** -0.5)
    out = _dft_stage2(c1.astype(BF16), s1.astype(BF16), y_r, y_i, y_sr, y_si, proj4, w_out, x4)
    return out.reshape(bsz * seq, D_MODEL)


OUT_TM = 512


def _outproj_kernel(mm_ref, mp_ref, ms_ref, x_ref, w_ref, o_ref):
    acc = jnp.dot(mm_ref[...], w_ref[:C_MAIN, :], preferred_element_type=F32)
    acc = acc + jnp.dot(mp_ref[...], w_ref[C_MAIN:2 * C_MAIN, :], preferred_element_type=F32)
    acc = acc + jnp.dot(ms_ref[...], w_ref[2 * C_MAIN:, :], preferred_element_type=F32)
    o_ref[...] = x_ref[...] + acc


def _outproj(minus2d, plus2d, nyq2d, x2d, w_out):
    t = x2d.shape[0]
    return pl.pallas_call(
        _outproj_kernel,
        grid=(t // OUT_TM,),
        in_specs=[
            pl.BlockSpec((OUT_TM, C_MAIN), lambda i: (i, 0)),
            pl.BlockSpec((OUT_TM, C_MAIN), lambda i: (i, 0)),
            pl.BlockSpec((OUT_TM, LANES), lambda i: (i, 0)),
            pl.BlockSpec((OUT_TM, D_MODEL), lambda i: (i, 0)),
            pl.BlockSpec((2 * C_MAIN + LANES, D_MODEL), lambda i: (0, 0)),
        ],
        out_specs=pl.BlockSpec((OUT_TM, D_MODEL), lambda i: (i, 0)),
        out_shape=jax.ShapeDtypeStruct((t, D_MODEL), F32),
        compiler_params=_params(("parallel",)),
        name="outproj",
    )(minus2d, plus2d, nyq2d, x2d, w_out)


def _rope_tables_a(seq):
    half = A_ROT // 2
    inv_freq = jnp.exp(-(jnp.arange(half, dtype=F32) * (2.0 / A_ROT)) * math.log(A_THETA))
    ang = jnp.arange(seq, dtype=F32)[:, None] * inv_freq[None, :]
    cos, sin = jnp.cos(ang), jnp.sin(ang)
    d = jnp.arange(LANES) % A_HEAD
    f = d % half
    cm = jnp.where(d[None, :] < A_ROT, cos[:, f], 1.0)
    s1 = jnp.where(d[None, :] < half, -sin[:, f], 0.0)
    s2 = jnp.where((d[None, :] >= half) & (d[None, :] < A_ROT), sin[:, f], 0.0)
    qs = A_HEAD ** -0.5 * LOG2E
    return jnp.stack([cm * qs, s1 * qs, s2 * qs, cm, s1, s2]).astype(F32)


def _rope_tables_b(seq):
    half = B_QK // 2
    inv_freq = jnp.exp(-(jnp.arange(half, dtype=F32) * (2.0 / B_QK)) * math.log(B_THETA))
    ang = jnp.arange(seq, dtype=F32)[:, None] * inv_freq[None, :]
    cos, sin = jnp.cos(ang), jnp.sin(ang)
    ks = B_QK ** -0.5
    return jnp.stack([cos, sin, cos * ks, sin * ks]).astype(F32)


def _weights_a(w_in):
    nq = A_QH * A_HEAD
    nkv = A_KVH * A_HEAD
    q = w_in[:, :nq]
    k = w_in[:, nq:nq + nkv].reshape(D_MODEL, A_KVH, 1, A_HEAD)
    v = w_in[:, nq + nkv:nq + 2 * nkv]
    gate = w_in[:, nq + 2 * nkv:]
    kk = jnp.broadcast_to(k, (D_MODEL, A_KVH, 2, A_HEAD)).reshape(D_MODEL, 2 * nkv)
    return jnp.concatenate([q, gate, kk, v], axis=1).astype(BF16)


def _c_index():
    g = jnp.arange(C_GROUPS, dtype=jnp.int32)[:, None] * C_GDIM
    m = jnp.arange(C_HALF, dtype=jnp.int32)[None, :]
    minus = (g + m).reshape(-1)
    plus = (g + jnp.where(m == 0, 0, C_GDIM - m)).reshape(-1)
    nyq = (g + C_HALF).reshape(-1)
    return minus, plus, nyq, jnp.tile(m == 0, (C_GROUPS, 1)).reshape(-1)


def _weights_c(w_in):
    ch = jnp.arange(C_GDIM, dtype=jnp.int32)
    m = jnp.arange(C_HALF, dtype=jnp.int32)
    cm, sm = _trig(ch[:, None] * m[None, :], C_GDIM, C_GDIM ** -0.5)
    nyq, _ = _trig(ch * C_HALF, C_GDIM, C_GDIM ** -0.5)
    lane = jnp.arange(LANES)[None, None, :] == jnp.arange(C_GROUPS)[:, None, None]
    nyq = jnp.where(lane, nyq[None, :, None], 0.0)
    cs = jnp.concatenate([jnp.broadcast_to(cm, (C_GROUPS,) + cm.shape),
                          jnp.broadcast_to(sm, (C_GROUPS,) + sm.shape), nyq], axis=2)
    w_a, w_b, w_sp = _fold_channel_dft(w_in[:, :BRANCH], cs)
    minus, plus, nyq_idx, _ = _c_index()
    gate = w_in[:, BRANCH:]
    g_sp = jnp.zeros((D_MODEL, LANES), F32).at[:, :C_GROUPS].set(gate[:, nyq_idx])
    return jnp.concatenate([w_a, w_b, gate[:, minus].astype(BF16), gate[:, plus].astype(BF16),
                            w_sp.astype(BF16), g_sp.astype(BF16)], axis=1)


def _weights_c_out(w_out):
    minus, plus, nyq_idx, dup = _c_index()
    w_plus = jnp.where(dup[:, None], 0.0, w_out[plus])
    w_sp = jnp.zeros((LANES, D_MODEL), F32).at[:C_GROUPS].set(w_out[nyq_idx])
    return jnp.concatenate([w_out[minus], w_plus, w_sp], axis=0).astype(BF16)


def _trunk(x, norm_g, fin_g, wa, a_sink, a_w_out, wb, log_g, b_w_out, wc, c_w_out):
    bsz, seq, _ = x.shape
    t = bsz * seq
    x2d = x.reshape(t, D_MODEL)
    tabs_a = _rope_tables_a(seq)
    tabs_b = _rope_tables_b(seq)
    fin = fin_g.reshape(1, D_MODEL)

    def layer_a(x2d, layer, j, final):
        q3, gate3, kk, vt = _inproj_a(x2d, norm_g[layer].reshape(1, D_MODEL), wa[j], tabs_a, bsz, seq)
        return _attention(x2d, q3, gate3, kk, vt, a_sink[j], a_w_out[j], fin, bsz, seq, final)

    x2d = layer_a(x2d, 0, 0, False)

    proj = _inproj(x2d, norm_g[1].reshape(1, D_MODEL), wb, tabs_b, seq, 2 * B_HEADS)
    x2d = _retention(proj.reshape(bsz, seq, -1), log_g, x2d.reshape(bsz, seq, D_MODEL),
                     b_w_out).reshape(t, D_MODEL)

    if seq <= C_DIRECT_MAX:
        proj = _inproj(x2d, norm_g[2].reshape(1, D_MODEL), wc, None, seq, 0)
        minus, plus, nyq = _fourier_mix(proj.reshape(bsz, seq, -1))
        x2d = _outproj(minus.reshape(t, C_MAIN), plus.reshape(t, C_MAIN), nyq.reshape(t, LANES),
                       x2d, c_w_out)
    else:
        x2d = _fourier_long(x2d, norm_g[2].reshape(1, D_MODEL), wc, c_w_out, bsz, seq)

    x2d = layer_a(x2d, 3, 1, True)
    return x2d.reshape(bsz, seq, D_MODEL)


def kernel(x_prompt, x_sample, norm_g, final_norm_g, a_w_in, a_sink, a_w_out, b_w_in, b_decay,
           b_w_out, c_w_in, c_w_out):
    wa = [_weights_a(a_w_in[j]) for j in range(a_w_in.shape[0])]
    a_out = [a_w_out[j].astype(BF16) for j in range(a_w_out.shape[0])]
    wb = b_w_in[0].astype(BF16)
    log_g = jax.nn.log_sigmoid(b_decay[0].astype(F32))
    wc = _weights_c(c_w_in[0])
    args = (norm_g, final_norm_g, wa, a_sink, a_out, wb, log_g, b_w_out[0].astype(BF16),
            wc, _weights_c_out(c_w_out[0]))
    return (_trunk(x_prompt, *args), _trunk(x_sample, *args))
```

```python
import functools
import math

import jax
import jax.numpy as jnp
from jax import lax
from jax.experimental import pallas as pl
from jax.experimental.pallas import tpu as pltpu

D_MODEL = 1024
BRANCH = 2048
NORM_EPS = 1e-6

A_HEAD = 64
A_QH = 32
A_KVH = 4
A_ROT = 16
A_THETA = 500000.0
A_WIN = 128
A_QBLK = 256
A_SPAN = A_QBLK + 2 * A_WIN
A_PAIRS = A_QH // 2
A_LOOKAHEAD = 6
A_VPAD = 16
LOG2E = math.log2(math.e)

B_QK = 256
B_HEADS = 4
B_V = 512
B_CHUNK = 256
B_OUT_HEADS = 4
B_THETA = 10000.0

C_GROUPS = 4
C_GDIM = 512
C_DIRECT_MAX = 2048

LANES = 128
VMEM_LIMIT = 56 * 1024 * 1024

BF16 = jnp.bfloat16
F32 = jnp.float32


def _params(sem):
    return pltpu.CompilerParams(dimension_semantics=sem, vmem_limit_bytes=VMEM_LIMIT)


def _silu(x):
    return x * jax.nn.sigmoid(x)


def _rms(x, g):
    ms = jnp.mean(x * x, axis=-1, keepdims=True)
    return (x * lax.rsqrt(ms + NORM_EPS)) * g


IN_TM = 512
IN_TN = 512


def _inproj_a_kernel(x_ref, g_ref, w_ref, tab_ref, q_ref, gate_ref, kk_ref, vt_ref):
    h = _rms(x_ref[...], g_ref[...]).astype(BF16)

    def rope(a, t0):
        return (a * tab_ref[t0] + pltpu.roll(a, LANES - A_ROT // 2, 1) * tab_ref[t0 + 1]
                + pltpu.roll(a, A_ROT // 2, 1) * tab_ref[t0 + 2])

    for c in range(4):
        acc = jnp.dot(h, w_ref[:, c * IN_TN:(c + 1) * IN_TN], preferred_element_type=F32)
        for s in range(4):
            q_ref[4 * c + s] = rope(acc[:, s * LANES:(s + 1) * LANES], 0).astype(BF16)
    for c in range(4):
        acc = jnp.dot(h, w_ref[:, BRANCH + c * IN_TN:BRANCH + (c + 1) * IN_TN],
                      preferred_element_type=F32)
        for s in range(4):
            gate_ref[4 * c + s] = acc[:, s * LANES:(s + 1) * LANES].astype(BF16)
    acc = jnp.dot(h, w_ref[:, 2 * BRANCH:], preferred_element_type=F32)
    nkv = A_KVH * A_HEAD
    lane = lax.broadcasted_iota(jnp.int32, (IN_TM, LANES), 1)
    for p in range(nkv // LANES):
        a = rope(acc[:, p * LANES:(p + 1) * LANES], 3)
        swapped = pltpu.roll(a, A_HEAD, 1)
        kk_ref[2 * p] = jnp.where(lane < A_HEAD, a, swapped).astype(BF16)
        kk_ref[2 * p + 1] = jnp.where(lane >= A_HEAD, a, swapped).astype(BF16)
    vt_ref[0] = acc[:, nkv:].T.astype(BF16)


def _inproj_a(x2d, g, w, tabs, bsz, seq):
    t = x2d.shape[0]
    nblk = seq // IN_TM
    n = w.shape[1]
    return pl.pallas_call(
        _inproj_a_kernel,
        grid=(t // IN_TM,),
        in_specs=[
            pl.BlockSpec((IN_TM, D_MODEL), lambda i: (i, 0)),
            pl.BlockSpec((1, D_MODEL), lambda i: (0, 0)),
            pl.BlockSpec((D_MODEL, n), lambda i: (0, 0)),
            pl.BlockSpec((6, IN_TM, LANES), lambda i: (0, i % nblk, 0)),
        ],
        out_specs=[
            pl.BlockSpec((A_PAIRS, IN_TM, LANES), lambda i: (0, i, 0)),
            pl.BlockSpec((A_PAIRS, IN_TM, LANES), lambda i: (0, i, 0)),
            pl.BlockSpec((A_KVH, IN_TM, LANES), lambda i: (0, i, 0)),
            pl.BlockSpec((1, A_KVH * A_HEAD, IN_TM), lambda i: (i // nblk, 0, i % nblk)),
        ],
        out_shape=[
            jax.ShapeDtypeStruct((A_PAIRS, t, LANES), BF16),
            jax.ShapeDtypeStruct((A_PAIRS, t, LANES), BF16),
            jax.ShapeDtypeStruct((A_KVH, t, LANES), BF16),
            jax.ShapeDtypeStruct((bsz, A_KVH * A_HEAD, seq), BF16),
        ],
        compiler_params=_params(("parallel",)),
        name="inproj_a",
    )(x2d, g, w, tabs)


def _inproj_kernel(x_ref, g_ref, w_ref, *rest, rope_heads):
    if rope_heads:
        tab_ref, o_ref = rest
    else:
        (o_ref,) = rest
    h = _rms(x_ref[...], g_ref[...]).astype(BF16)
    n = w_ref.shape[1]
    for c0 in range(0, n, IN_TN):
        width = min(IN_TN, n - c0)
        acc = jnp.dot(h, w_ref[:, c0:c0 + width], preferred_element_type=F32)
        for lo in range(0, width, B_QK):
            head = (c0 + lo) // B_QK
            if head < rope_heads:
                t0 = 0 if head < rope_heads // 2 else 2
                cos, sin = tab_ref[t0], tab_ref[t0 + 1]
                x1 = acc[:, lo:lo + LANES]
                x2 = acc[:, lo + LANES:lo + B_QK]
                o_ref[:, c0 + lo:c0 + lo + LANES] = (x1 * cos - x2 * sin).astype(BF16)
                o_ref[:, c0 + lo + LANES:c0 + lo + B_QK] = (x2 * cos + x1 * sin).astype(BF16)
            else:
                o_ref[:, c0 + lo:c0 + lo + B_QK] = acc[:, lo:lo + B_QK].astype(BF16)


def _inproj(x2d, g, w, tabs, seq, rope_heads):
    t = x2d.shape[0]
    n = w.shape[1]
    nblk = seq // IN_TM
    in_specs = [
        pl.BlockSpec((IN_TM, D_MODEL), lambda i: (i, 0)),
        pl.BlockSpec((1, D_MODEL), lambda i: (0, 0)),
        pl.BlockSpec((D_MODEL, n), lambda i: (0, 0)),
    ]
    args = [x2d, g, w]
    if rope_heads:
        in_specs.append(pl.BlockSpec((4, IN_TM, LANES), lambda i: (0, i % nblk, 0)))
        args.append(tabs)
    return pl.pallas_call(
        functools.partial(_inproj_kernel, rope_heads=rope_heads),
        grid=(t // IN_TM,),
        in_specs=in_specs,
        out_specs=pl.BlockSpec((IN_TM, n), lambda i: (i, 0)),
        out_shape=jax.ShapeDtypeStruct((t, n), BF16),
        compiler_params=_params(("parallel",)),
        name="inproj",
    )(*args)


def _attn_kernel(sink_ref, q_ref, gate_ref, kp_ref, kc_ref, kn_ref, vp_ref, vc_ref, vn_ref,
                 x_ref, w_ref, fg_ref, o_ref, k_ref, v_ref, bias_ref, *, final):
    i = pl.program_id(1)
    last = pl.num_programs(1) - 1
    rows = A_SPAN - A_WIN
    per_kv = A_PAIRS // A_KVH

    ones_row = (lax.broadcasted_iota(jnp.int32, (A_VPAD, A_SPAN), 0) == 0).astype(BF16)
    for h in range(A_KVH):
        k_ref[h] = jnp.concatenate([kp_ref[h], kc_ref[h], kn_ref[h]], axis=0)
        v_ref[h, :A_HEAD, :] = jnp.concatenate([vp_ref[0, h], vc_ref[0, h], vn_ref[0, h]], axis=1)
        v_ref[h, A_HEAD:, :] = ones_row

    r = lax.broadcasted_iota(jnp.int32, (A_WIN, LANES), 0)
    c = lax.broadcasted_iota(jnp.int32, (A_WIN, LANES), 1)
    neg = jnp.full((A_WIN, LANES), -1e30, F32)
    top = jnp.where(r >= c, 0.0, neg)
    bot = jnp.where(r <= c, 0.0, neg)
    bias_ref[0] = jnp.where(i == 0, neg, top)
    bias_ref[1] = bot
    bias_ref[2] = top
    bias_ref[3] = jnp.where(i == last, neg, bot)
    qlane = lax.broadcasted_iota(jnp.int32, (LANES, LANES), 1)

    units = [(g, jl, jj) for g in range(A_KVH) for jl in range(per_kv) for jj in range(2)]

    def scores(n):
        g, jl, jj = units[n]
        q = q_ref[g * per_kv + jl, jj * LANES:(jj + 1) * LANES, :]
        zero = jnp.zeros_like(q)
        rhs = jnp.concatenate([jnp.where(qlane < A_HEAD, q, zero),
                               jnp.where(qlane >= A_HEAD, q, zero)], axis=0)
        return lax.dot_general(k_ref[g, jj * A_WIN:jj * A_WIN + rows, :], rhs,
                               (((1,), (1,)), ((), ())), preferred_element_type=F32)

    def softmax_pv(s_t, sinks, g, jj):
        ps, extra = [], []
        for a in range(2):
            blk = s_t[:, a * LANES:(a + 1) * LANES]
            parts = [blk[:A_WIN] + bias_ref[2 * jj], blk[A_WIN:2 * A_WIN],
                     blk[2 * A_WIN:] + bias_ref[2 * jj + 1]]
            mx = jnp.full((1, LANES), sinks[a], F32)
            for part in parts:
                mx = jnp.maximum(mx, jnp.max(part, axis=0, keepdims=True))
            ps.append(jnp.concatenate([jnp.exp2(part - mx).astype(BF16) for part in parts], axis=0))
            extra.append(jnp.exp2(sinks[a] - mx))
        o_ext = jnp.dot(v_ref[g, :, jj * A_WIN:jj * A_WIN + rows], jnp.concatenate(ps, axis=1),
                        preferred_element_type=F32)
        den = o_ext[A_HEAD:A_HEAD + 1, :] + jnp.concatenate(extra, axis=1)
        return o_ext[:A_HEAD, :] / den

    pending = [scores(n) for n in range(A_LOOKAHEAD)]
    outs, gated = [], []
    y = x_ref[...]
    for n, (g, jl, jj) in enumerate(units):
        if n + A_LOOKAHEAD < len(units):
            pending.append(scores(n + A_LOOKAHEAD))
        j = g * per_kv + jl
        sinks = (sink_ref[2 * j] * LOG2E, sink_ref[2 * j + 1] * LOG2E)
        outs.append(softmax_pv(pending.pop(0), sinks, g, jj))
        if jj == 1:
            o0, o1 = outs[-2:]
            o_pair = jnp.concatenate(
                [jnp.concatenate([o0[:, :LANES], o1[:, :LANES]], axis=1),
                 jnp.concatenate([o0[:, LANES:], o1[:, LANES:]], axis=1)], axis=0).T
            gt = gate_ref[j].astype(F32)
            gated.append((o_pair * _silu(gt)).astype(BF16))
            if jl == per_kv - 1:
                wg = w_ref[g * per_kv * LANES:(g + 1) * per_kv * LANES, :]
                y = y + jnp.dot(jnp.concatenate(gated[-per_kv:], axis=1), wg,
                                preferred_element_type=F32)
    if final:
        y = _rms(y, fg_ref[...])
    o_ref[...] = y


def _attention(x2d, q3, gate3, kk, vt, sink, w_out, fin_g, bsz, seq, final):
    t = x2d.shape[0]
    nqb = seq // A_QBLK
    nkb = seq // A_WIN
    vt4 = vt.reshape(bsz, A_KVH, A_HEAD, seq)
    grid_spec = pltpu.PrefetchScalarGridSpec(
        num_scalar_prefetch=1,
        grid=(bsz, nqb),
        in_specs=[
            pl.BlockSpec((A_PAIRS, A_QBLK, LANES), lambda b, i, s: (0, b * nqb + i, 0)),
            pl.BlockSpec((A_PAIRS, A_QBLK, LANES), lambda b, i, s: (0, b * nqb + i, 0)),
            pl.BlockSpec((A_KVH, A_WIN, LANES),
                         lambda b, i, s: (0, b * nkb + jnp.maximum(2 * i - 1, 0), 0)),
            pl.BlockSpec((A_KVH, A_QBLK, LANES), lambda b, i, s: (0, b * nqb + i, 0)),
            pl.BlockSpec((A_KVH, A_WIN, LANES),
                         lambda b, i, s: (0, b * nkb + jnp.minimum(2 * i + 2, nkb - 1), 0)),
            pl.BlockSpec((1, A_KVH, A_HEAD, A_WIN),
                         lambda b, i, s: (b, 0, 0, jnp.maximum(2 * i - 1, 0))),
            pl.BlockSpec((1, A_KVH, A_HEAD, A_QBLK), lambda b, i, s: (b, 0, 0, i)),
            pl.BlockSpec((1, A_KVH, A_HEAD, A_WIN),
                         lambda b, i, s: (b, 0, 0, jnp.minimum(2 * i + 2, nkb - 1))),
            pl.BlockSpec((A_QBLK, D_MODEL), lambda b, i, s: (b * nqb + i, 0)),
            pl.BlockSpec((BRANCH, D_MODEL), lambda b, i, s: (0, 0)),
            pl.BlockSpec((1, D_MODEL), lambda b, i, s: (0, 0)),
        ],
        out_specs=pl.BlockSpec((A_QBLK, D_MODEL), lambda b, i, s: (b * nqb + i, 0)),
        scratch_shapes=[
            pltpu.VMEM((A_KVH, A_SPAN, LANES), BF16),
            pltpu.VMEM((A_KVH, A_HEAD + A_VPAD, A_SPAN), BF16),
            pltpu.VMEM((4, A_WIN, LANES), F32),
        ],
    )
    return pl.pallas_call(
        functools.partial(_attn_kernel, final=final),
        grid_spec=grid_spec,
        out_shape=jax.ShapeDtypeStruct((t, D_MODEL), F32),
        compiler_params=_params(("parallel", "parallel")),
        name="attention",
    )(sink, q3, gate3, kk, kk, kk, vt4, vt4, vt4, x2d, w_out, fin_g)


def _retention_start(lg_ref, st_ref, intra_ref, dec_ref, backward):
    n = B_CHUNK
    ii = lax.broadcasted_iota(jnp.int32, (n, n), 0)
    jj = lax.broadcasted_iota(jnp.int32, (n, n), 1)
    col = lax.broadcasted_iota(jnp.int32, (n, LANES), 0).astype(F32)
    st_ref[...] = jnp.zeros_like(st_ref)
    for h in range(B_HEADS):
        lg = lg_ref[1 if backward else 0, h]
        if backward:
            mask = jj > ii
            dist = (jj - ii).astype(F32)
            dec_ref[h, 0] = jnp.exp((n - col) * lg)
            dec_ref[h, 1] = jnp.exp(col * lg)
        else:
            mask = ii >= jj
            dist = (ii - jj).astype(F32)
            dec_ref[h, 0] = jnp.exp((col + 1.0) * lg)
            dec_ref[h, 1] = jnp.exp((n - 1.0 - col) * lg)
        intra_ref[h] = jnp.where(mask, jnp.exp(jnp.where(mask, dist, 0.0) * lg), 0.0)


def _retention_heads(lg_ref, q_ref, k_ref, v_ref, st_ref, intra_ref, dec_ref, backward, emit):
    heads = range(B_HEADS)
    qs = [q_ref[0, :, h * B_QK:(h + 1) * B_QK] for h in heads]
    ks = [k_ref[0, :, h * B_QK:(h + 1) * B_QK] for h in heads]
    vs = [v_ref[0, :, h * B_V:(h + 1) * B_V] for h in heads]
    sc = [lax.dot_general(qs[h], ks[h], (((1,), (1,)), ((), ())), preferred_element_type=F32)
          for h in heads]
    cross = [jnp.dot(qs[h], st_ref[h].astype(BF16), preferred_element_type=F32) for h in heads]
    for h in heads:
        q_dec = jnp.concatenate([dec_ref[h, 0]] * (B_V // LANES), axis=1)
        k_dec = jnp.concatenate([dec_ref[h, 1]] * (B_QK // LANES), axis=1)
        c_dec = jnp.exp(jnp.full((1, B_V), B_CHUNK * lg_ref[1 if backward else 0, h], F32))
        kd_t = (ks[h].astype(F32) * k_dec).T.astype(BF16)
        st_ref[h] = st_ref[h] * c_dec + jnp.dot(kd_t, vs[h], preferred_element_type=F32)
        inner = jnp.dot((sc[h] * intra_ref[h]).astype(BF16), vs[h], preferred_element_type=F32)
        emit(h, inner + cross[h] * q_dec)


def _retention_bwd_kernel(lg_ref, q_ref, k_ref, v_ref, o_ref, st_ref, intra_ref, dec_ref):
    @pl.when(pl.program_id(1) == 0)
    def _():
        _retention_start(lg_ref, st_ref, intra_ref, dec_ref, True)

    def emit(h, o):
        o_ref[0, :, h * B_V:(h + 1) * B_V] = o.astype(BF16)

    _retention_heads(lg_ref, q_ref, k_ref, v_ref, st_ref, intra_ref, dec_ref, True, emit)


def _retention_fwd_kernel(lg_ref, q_ref, k_ref, v_ref, gate_ref, ob_ref, x_ref, w_ref, o_ref,
                          st_ref, intra_ref, dec_ref):
    @pl.when(pl.program_id(1) == 0)
    def _():
        _retention_start(lg_ref, st_ref, intra_ref, dec_ref, False)

    y = [x_ref[0]]
    parts = []

    def emit(h, o_f):
        sl = slice(h * B_V, (h + 1) * B_V)
        o = o_f + ob_ref[0, :, sl].astype(F32)
        on = o * lax.rsqrt(jnp.mean(o * o, axis=-1, keepdims=True) + NORM_EPS)
        parts.append((on * _silu(gate_ref[0, :, sl].astype(F32))).astype(BF16))
        if len(parts) == B_OUT_HEADS:
            rows = slice((h + 1 - B_OUT_HEADS) * B_V, (h + 1) * B_V)
            y[0] = y[0] + jnp.dot(jnp.concatenate(parts, axis=1), w_ref[rows, :],
                                  preferred_element_type=F32)
            parts.clear()

    _retention_heads(lg_ref, q_ref, k_ref, v_ref, st_ref, intra_ref, dec_ref, False, emit)
    o_ref[0] = y[0]


def _retention(proj3, log_g, x3, w_out):
    bsz, seq, _ = proj3.shape
    nc = seq // B_CHUNK
    nqk = B_HEADS * B_QK
    smem = pl.BlockSpec(memory_space=pltpu.SMEM)
    state = [pltpu.VMEM((B_HEADS, B_QK, B_V), F32),
             pltpu.VMEM((B_HEADS, B_CHUNK, B_CHUNK), F32),
             pltpu.VMEM((B_HEADS, 2, B_CHUNK, LANES), F32)]

    def specs(chunk):
        return [
            pl.BlockSpec((1, B_CHUNK, nqk), lambda b, t: (b, chunk(t), 0)),
            pl.BlockSpec((1, B_CHUNK, nqk), lambda b, t: (b, chunk(t), 1)),
            pl.BlockSpec((1, B_CHUNK, BRANCH), lambda b, t: (b, chunk(t), 1)),
        ]

    def rev(t):
        return nc - 1 - t

    def fwd(t):
        return t

    o_b = pl.pallas_call(
        _retention_bwd_kernel,
        grid=(bsz, nc),
        in_specs=[smem] + specs(rev),
        out_specs=pl.BlockSpec((1, B_CHUNK, BRANCH), lambda b, t: (b, rev(t), 0)),
        out_shape=jax.ShapeDtypeStruct((bsz, seq, BRANCH), BF16),
        scratch_shapes=state,
        compiler_params=_params(("parallel", "arbitrary")),
        name="retention_bwd",
    )(log_g, proj3, proj3, proj3)
    return pl.pallas_call(
        _retention_fwd_kernel,
        grid=(bsz, nc),
        in_specs=[smem] + specs(fwd) + [
            pl.BlockSpec((1, B_CHUNK, BRANCH), lambda b, t: (b, t, 2)),
            pl.BlockSpec((1, B_CHUNK, BRANCH), lambda b, t: (b, t, 0)),
            pl.BlockSpec((1, B_CHUNK, D_MODEL), lambda b, t: (b, t, 0)),
            pl.BlockSpec((BRANCH, D_MODEL), lambda b, t: (0, 0)),
        ],
        out_specs=pl.BlockSpec((1, B_CHUNK, D_MODEL), lambda b, t: (b, t, 0)),
        out_shape=jax.ShapeDtypeStruct((bsz, seq, D_MODEL), F32),
        scratch_shapes=state,
        compiler_params=_params(("parallel", "arbitrary")),
        name="retention_fwd",
    )(log_g, proj3, proj3, proj3, proj3, o_b, x3, w_out)


C_HALF = C_GDIM // 2
C_MAIN = C_GROUPS * C_HALF
C_FOLD = 2 * C_HALF + LANES


def _fold_kernel(w_ref, cs_ref, a_ref, b_ref, sp_ref):
    r = jnp.dot(w_ref[...], cs_ref[0], preferred_element_type=F32,
                precision=lax.Precision.HIGHEST)
    a_ref[...] = r[:, :C_HALF].astype(BF16)
    b_ref[...] = r[:, C_HALF:2 * C_HALF].astype(BF16)

    @pl.when(pl.program_id(0) == 0)
    def _():
        sp_ref[...] = jnp.zeros_like(sp_ref)

    sp_ref[...] += r[:, 2 * C_HALF:]


def _fold_channel_dft(w_u, cs):
    spec = pl.BlockSpec((D_MODEL, C_HALF), lambda g: (0, g))
    return pl.pallas_call(
        _fold_kernel,
        grid=(C_GROUPS,),
        in_specs=[pl.BlockSpec((D_MODEL, C_GDIM), lambda g: (0, g)),
                  pl.BlockSpec((1, C_GDIM, C_FOLD), lambda g: (g, 0, 0))],
        out_specs=[spec, spec, pl.BlockSpec((D_MODEL, LANES), lambda g: (0, 0))],
        out_shape=[jax.ShapeDtypeStruct((D_MODEL, C_MAIN), BF16)] * 2
        + [jax.ShapeDtypeStruct((D_MODEL, LANES), F32)],
        compiler_params=_params(("arbitrary",)),
        name="fold_channel_dft",
    )(w_u, cs)


DFT_ROWS = 512


def _dft_mix_kernel(cm_ref, sm_ref, a_ref, b_ref, gm_ref, gp_ref, sp_ref, gsp_ref,
                    om_ref, op_ref, osp_ref):
    n = cm_ref.shape[0]
    rc = min(n, DFT_ROWS)
    a = a_ref[0]
    b = b_ref[0]
    for r in range(n // rc):
        rows = slice(r * rc, (r + 1) * rc)
        ea = jnp.dot(cm_ref[rows, :], a, preferred_element_type=F32)
        eb = jnp.dot(sm_ref[rows, :], b, preferred_element_type=F32)
        om_ref[0, rows, :] = ((ea - eb) * _silu(gm_ref[0, rows, :].astype(F32))).astype(BF16)
        op_ref[0, rows, :] = ((ea + eb) * _silu(gp_ref[0, rows, :].astype(F32))).astype(BF16)

    @pl.when(pl.program_id(1) == 0)
    def _():
        es = jnp.dot(cm_ref[...], sp_ref[0], preferred_element_type=F32)
        osp_ref[0] = (es * _silu(gsp_ref[0].astype(F32))).astype(BF16)


def _trig(m, period, scale):
    ang = (2.0 * math.pi / period) * (m % period).astype(F32)
    return jnp.cos(ang) * scale, jnp.sin(ang) * scale


def _fourier_mix(proj3):
    bsz, seq, _ = proj3.shape
    idx = jnp.arange(seq, dtype=jnp.int32)
    cm, sm = _trig(idx[:, None] * idx[None, :], seq, seq ** -0.5)
    tc = 512
    nct = C_MAIN // tc
    mspec = pl.BlockSpec((seq, seq), lambda b, c: (0, 0))

    def dspec(width, idx):
        return pl.BlockSpec((1, seq, width), lambda b, c: (b, 0, idx(c)))

    sp_blk = 4 * C_MAIN // LANES
    return pl.pallas_call(
        _dft_mix_kernel,
        grid=(bsz, nct),
        in_specs=[mspec, mspec,
                  dspec(tc, lambda c: c), dspec(tc, lambda c: nct + c),
                  dspec(tc, lambda c: 2 * nct + c), dspec(tc, lambda c: 3 * nct + c),
                  dspec(LANES, lambda c: sp_blk), dspec(LANES, lambda c: sp_blk + 1)],
        out_specs=[dspec(tc, lambda c: c), dspec(tc, lambda c: c), dspec(LANES, lambda c: 0)],
        out_shape=[jax.ShapeDtypeStruct((bsz, seq, C_MAIN), BF16)] * 2
        + [jax.ShapeDtypeStruct((bsz, seq, LANES), BF16)],
        compiler_params=_params(("parallel", "arbitrary")),
        name="dft_mix",
    )(cm.astype(BF16), sm.astype(BF16), proj3, proj3, proj3, proj3, proj3, proj3)


C_GRP = 8
C_SUB = 64
C_ROWS = 256
C_NCHUNK = 1024


def _inproj_perm_kernel(x_ref, g_ref, w_ref, o_ref, xs_ref):
    rh = x_ref.shape[1]
    n = w_ref.shape[1]
    for i in range(C_GRP):
        xs_ref[i * rh:(i + 1) * rh, :] = x_ref[0, :, i, :]
    per_dot = C_ROWS // rh
    for j in range(C_GRP // per_dot):
        h = _rms(xs_ref[j * C_ROWS:(j + 1) * C_ROWS, :], g_ref[...]).astype(BF16)
        for c0 in range(0, n, C_NCHUNK):
            cols = slice(c0, min(c0 + C_NCHUNK, n))
            acc = jnp.dot(h, w_ref[:, cols], preferred_element_type=F32)
            for i in range(per_dot):
                o_ref[0, j * per_dot + i, :, cols] = acc[i * rh:(i + 1) * rh, :].astype(BF16)


def _inproj_perm(x4, g, w):
    bsz, r, _, _ = x4.shape
    n = w.shape[1]
    rh = min(r, C_SUB)
    return pl.pallas_call(
        _inproj_perm_kernel,
        grid=(bsz, r // C_GRP, r // rh),
        in_specs=[
            pl.BlockSpec((1, rh, C_GRP, D_MODEL), lambda b, m, s: (b, s, m, 0)),
            pl.BlockSpec((1, D_MODEL), lambda b, m, s: (0, 0)),
            pl.BlockSpec((D_MODEL, n), lambda b, m, s: (0, 0)),
        ],
        out_specs=pl.BlockSpec((1, C_GRP, rh, n), lambda b, m, s: (b, m, s, 0)),
        out_shape=jax.ShapeDtypeStruct((bsz, r, r, n), BF16),
        scratch_shapes=[pltpu.VMEM((C_GRP * rh, D_MODEL), F32)],
        compiler_params=_params(("parallel", "parallel", "parallel")),
        name="inproj_perm",
    )(x4, g, w)


def _dft_stage1_kernel(d_ref, x_ref, sp_ref, yr_ref, yi_ref, ysr_ref, ysi_ref):
    r = x_ref.shape[2]
    for i in range(C_GRP):
        dc = d_ref[i, :r, :]
        ds = d_ref[i, r:, :]
        yr_ref[0, :, i, :] = jnp.dot(dc, x_ref[0, i], preferred_element_type=F32)
        yi_ref[0, :, i, :] = jnp.dot(ds, x_ref[0, i], preferred_element_type=F32)
        ysr_ref[0, :, i, :] = jnp.dot(dc, sp_ref[0, i], preferred_element_type=F32)
        ysi_ref[0, :, i, :] = jnp.dot(ds, sp_ref[0, i], preferred_element_type=F32)


def _dft_stage1(proj4, dtab):
    bsz, r, _, _ = proj4.shape
    tc = 1024
    nct = 2 * C_MAIN // tc

    def yspec(width, idx):
        return pl.BlockSpec((1, r, C_GRP, width), lambda b, m, c: (b, 0, m, idx(c)))

    return pl.pallas_call(
        _dft_stage1_kernel,
        grid=(bsz, r // C_GRP, nct),
        in_specs=[
            pl.BlockSpec((C_GRP, 2 * r, r), lambda b, m, c: (m, 0, 0)),
            pl.BlockSpec((1, C_GRP, r, tc), lambda b, m, c: (b, m, 0, c)),
            pl.BlockSpec((1, C_GRP, r, LANES), lambda b, m, c: (b, m, 0, 4 * C_MAIN // LANES)),
        ],
        out_specs=[yspec(tc, lambda c: c), yspec(tc, lambda c: c),
                   yspec(LANES, lambda c: 0), yspec(LANES, lambda c: 0)],
        out_shape=[jax.ShapeDtypeStruct((bsz, r, r, 2 * C_MAIN), F32)] * 2
        + [jax.ShapeDtypeStruct((bsz, r, r, LANES), F32)] * 2,
        compiler_params=_params(("parallel", "parallel", "arbitrary")),
        name="dft_stage1",
    )(dtab, proj4, proj4)


def _dft_stage2_kernel(c1_ref, s1_ref, yra_ref, yia_ref, yrb_ref, yib_ref, gm_ref, gp_ref,
                       wm_ref, wp_ref, ysr_ref, ysi_ref, gsp_ref, wsp_ref, x_ref, o_ref, acc_ref):
    c = pl.program_id(2)
    r = c1_ref.shape[0]
    c1 = c1_ref[...]
    s1 = s1_ref[...]

    def dot(m, y):
        return jnp.dot(m, y.astype(BF16), preferred_element_type=F32)

    @pl.when(c == 0)
    def _():
        parts = []
        for i in range(C_GRP):
            es = dot(c1, ysr_ref[0, i]) - dot(s1, ysi_ref[0, i])
            parts.append((es * _silu(gsp_ref[0, i].astype(F32))).astype(BF16))
        nyq = jnp.dot(jnp.concatenate(parts, axis=0), wsp_ref[...], preferred_element_type=F32)
        for i in range(C_GRP):
            acc_ref[i * r:(i + 1) * r, :] = x_ref[0, :, i, :] + nyq[i * r:(i + 1) * r, :]

    minus, plus = [], []
    for i in range(C_GRP):
        ea = dot(c1, yra_ref[0, i]) - dot(s1, yia_ref[0, i])
        eb = dot(c1, yib_ref[0, i]) + dot(s1, yrb_ref[0, i])
        minus.append(((ea - eb) * _silu(gm_ref[0, i].astype(F32))).astype(BF16))
        plus.append(((ea + eb) * _silu(gp_ref[0, i].astype(F32))).astype(BF16))
    acc_ref[...] += (jnp.dot(jnp.concatenate(minus, axis=0), wm_ref[...], preferred_element_type=F32)
                     + jnp.dot(jnp.concatenate(plus, axis=0), wp_ref[...], preferred_element_type=F32))

    @pl.when(c == pl.num_programs(2) - 1)
    def _():
        for i in range(C_GRP):
            o_ref[0, :, i, :] = acc_ref[i * r:(i + 1) * r, :]


def _dft_stage2(c1, s1, y_r, y_i, y_sr, y_si, proj4, w_out, x4):
    bsz, r, _, _ = x4.shape
    tc = 512
    nct = C_MAIN // tc
    mspec = pl.BlockSpec((r, r), lambda b, m, c: (0, 0))

    def slab(width, idx):
        return pl.BlockSpec((1, C_GRP, r, width), lambda b, m, c: (b, m, 0, idx(c)))

    def wspec(rows, idx):
        return pl.BlockSpec((rows, D_MODEL), lambda b, m, c: (idx(c), 0))

    xspec = pl.BlockSpec((1, r, C_GRP, D_MODEL), lambda b, m, c: (b, 0, m, 0))
    sp_blk = 4 * C_MAIN // LANES
    return pl.pallas_call(
        _dft_stage2_kernel,
        grid=(bsz, r // C_GRP, nct),
        in_specs=[
            mspec, mspec,
            slab(tc, lambda c: c), slab(tc, lambda c: c),
            slab(tc, lambda c: nct + c), slab(tc, lambda c: nct + c),
            slab(tc, lambda c: 2 * nct + c), slab(tc, lambda c: 3 * nct + c),
            wspec(tc, lambda c: c), wspec(tc, lambda c: nct + c),
            slab(LANES, lambda c: 0), slab(LANES, lambda c: 0),
            slab(LANES, lambda c: sp_blk + 1),
            wspec(LANES, lambda c: 2 * C_MAIN // LANES),
            xspec,
        ],
        out_specs=xspec,
        out_shape=jax.ShapeDtypeStruct(x4.shape, F32),
        scratch_shapes=[pltpu.VMEM((C_GRP * r, D_MODEL), F32)],
        compiler_params=_params(("parallel", "parallel", "arbitrary")),
        name="dft_stage2",
    )(c1, s1, y_r, y_i, y_r, y_i, proj4, proj4, w_out, w_out, y_sr, y_si, proj4, w_out, x4)


def _fourier_long(x2d, g, wc, w_out, bsz, seq):
    r = math.isqrt(seq)
    assert r * r == seq and r % C_GRP == 0
    x4 = x2d.reshape(bsz, r, r, D_MODEL)
    proj4 = _inproj_perm(x4, g, wc)
    idx = jnp.arange(r, dtype=jnp.int32)
    tok = idx[:, None, None] + r * idx[None, None, :]
    dc, ds = _trig(idx[None, :, None] * tok, seq, r ** -0.5)
    dtab = jnp.concatenate([dc, ds], axis=1).astype(BF16)
    y_r, y_i, y_sr, y_si = _dft_stage1(proj4, dtab)
    c1, s1 = _trig(idx[:, None] * idx[None, :], r, r ** -0.5)
    out = _dft_stage2(c1.astype(BF16), s1.astype(BF16), y_r, y_i, y_sr, y_si, proj4, w_out, x4)
    return out.reshape(bsz * seq, D_MODEL)


OUT_TM = 512


def _outproj_kernel(mm_ref, mp_ref, ms_ref, x_ref, w_ref, o_ref):
    acc = jnp.dot(mm_ref[...], w_ref[:C_MAIN, :], preferred_element_type=F32)
    acc = acc + jnp.dot(mp_ref[...], w_ref[C_MAIN:2 * C_MAIN, :], preferred_element_type=F32)
    acc = acc + jnp.dot(ms_ref[...], w_ref[2 * C_MAIN:, :], preferred_element_type=F32)
    o_ref[...] = x_ref[...] + acc


def _outproj(minus2d, plus2d, nyq2d, x2d, w_out):
    t = x2d.shape[0]
    return pl.pallas_call(
        _outproj_kernel,
        grid=(t // OUT_TM,),
        in_specs=[
            pl.BlockSpec((OUT_TM, C_MAIN), lambda i: (i, 0)),
            pl.BlockSpec((OUT_TM, C_MAIN), lambda i: (i, 0)),
            pl.BlockSpec((OUT_TM, LANES), lambda i: (i, 0)),
            pl.BlockSpec((OUT_TM, D_MODEL), lambda i: (i, 0)),
            pl.BlockSpec((2 * C_MAIN + LANES, D_MODEL), lambda i: (0, 0)),
        ],
        out_specs=pl.BlockSpec((OUT_TM, D_MODEL), lambda i: (i, 0)),
        out_shape=jax.ShapeDtypeStruct((t, D_MODEL), F32),
        compiler_params=_params(("parallel",)),
        name="outproj",
    )(minus2d, plus2d, nyq2d, x2d, w_out)


def _rope_tables_a(seq):
    half = A_ROT // 2
    inv_freq = jnp.exp(-(jnp.arange(half, dtype=F32) * (2.0 / A_ROT)) * math.log(A_THETA))
    ang = jnp.arange(seq, dtype=F32)[:, None] * inv_freq[None, :]
    cos, sin = jnp.cos(ang), jnp.sin(ang)
    d = jnp.arange(LANES) % A_HEAD
    f = d % half
    cm = jnp.where(d[None, :] < A_ROT, cos[:, f], 1.0)
    s1 = jnp.where(d[None, :] < half, -sin[:, f], 0.0)
    s2 = jnp.where((d[None, :] >= half) & (d[None, :] < A_ROT), sin[:, f], 0.0)
    qs = A_HEAD ** -0.5 * LOG2E
    return jnp.stack([cm * qs, s1 * qs, s2 * qs, cm, s1, s2]).astype(F32)


def _rope_tables_b(seq):
    half = B_QK // 2
    inv_freq = jnp.exp(-(jnp.arange(half, dtype=F32) * (2.0 / B_QK)) * math.log(B_THETA))
    ang = jnp.arange(seq, dtype=F32)[:, None] * inv_freq[None, :]
    cos, sin = jnp.cos(ang), jnp.sin(ang)
    ks = B_QK ** -0.5
    return jnp.stack([cos, sin, cos * ks, sin * ks]).astype(F32)


def _weights_a(w_in):
    nq = A_QH * A_HEAD
    nkv = A_KVH * A_HEAD
    q = w_in[:, :nq]
    kv = w_in[:, nq:nq + 2 * nkv]
    gate = w_in[:, nq + 2 * nkv:]
    return jnp.concatenate([q, gate, kv], axis=1).astype(BF16)


def _c_index():
    g = jnp.arange(C_GROUPS, dtype=jnp.int32)[:, None] * C_GDIM
    m = jnp.arange(C_HALF, dtype=jnp.int32)[None, :]
    minus = (g + m).reshape(-1)
    plus = (g + jnp.where(m == 0, 0, C_GDIM - m)).reshape(-1)
    nyq = (g + C_HALF).reshape(-1)
    return minus, plus, nyq, jnp.tile(m == 0, (C_GROUPS, 1)).reshape(-1)


def _weights_c(w_in):
    ch = jnp.arange(C_GDIM, dtype=jnp.int32)
    m = jnp.arange(C_HALF, dtype=jnp.int32)
    cm, sm = _trig(ch[:, None] * m[None, :], C_GDIM, C_GDIM ** -0.5)
    nyq, _ = _trig(ch * C_HALF, C_GDIM, C_GDIM ** -0.5)
    lane = jnp.arange(LANES)[None, None, :] == jnp.arange(C_GROUPS)[:, None, None]
    nyq = jnp.where(lane, nyq[None, :, None], 0.0)
    cs = jnp.concatenate([jnp.broadcast_to(cm, (C_GROUPS,) + cm.shape),
                          jnp.broadcast_to(sm, (C_GROUPS,) + sm.shape), nyq], axis=2)
    w_a, w_b, w_sp = _fold_channel_dft(w_in[:, :BRANCH], cs)
    minus, plus, nyq_idx, _ = _c_index()
    gate = w_in[:, BRANCH:]
    g_sp = jnp.zeros((D_MODEL, LANES), F32).at[:, :C_GROUPS].set(gate[:, nyq_idx])
    return jnp.concatenate([w_a, w_b, gate[:, minus].astype(BF16), gate[:, plus].astype(BF16),
                            w_sp.astype(BF16), g_sp.astype(BF16)], axis=1)


def _weights_c_out(w_out):
    minus, plus, nyq_idx, dup = _c_index()
    w_plus = jnp.where(dup[:, None], 0.0, w_out[plus])
    w_sp = jnp.zeros((LANES, D_MODEL), F32).at[:C_GROUPS].set(w_out[nyq_idx])
    return jnp.concatenate([w_out[minus], w_plus, w_sp], axis=0).astype(BF16)


def _trunk(x, norm_g, fin_g, wa, a_sink, a_w_out, wb, log_g, b_w_out, wc, c_w_out):
    bsz, seq, _ = x.shape
    t = bsz * seq
    x2d = x.reshape(t, D_MODEL)
    tabs_a = _rope_tables_a(seq)
    tabs_b = _rope_tables_b(seq)
    fin = fin_g.reshape(1, D_MODEL)

    def layer_a(x2d, layer, j, final):
        q3, gate3, kk, vt = _inproj_a(x2d, norm_g[layer].reshape(1, D_MODEL), wa[j], tabs_a, bsz, seq)
        return _attention(x2d, q3, gate3, kk, vt, a_sink[j], a_w_out[j], fin, bsz, seq, final)

    x2d = layer_a(x2d, 0, 0, False)

    proj = _inproj(x2d, norm_g[1].reshape(1, D_MODEL), wb, tabs_b, seq, 2 * B_HEADS)
    x2d = _retention(proj.reshape(bsz, seq, -1), log_g, x2d.reshape(bsz, seq, D_MODEL),
                     b_w_out).reshape(t, D_MODEL)

    if seq <= C_DIRECT_MAX:
        proj = _inproj(x2d, norm_g[2].reshape(1, D_MODEL), wc, None, seq, 0)
        minus, plus, nyq = _fourier_mix(proj.reshape(bsz, seq, -1))
        x2d = _outproj(minus.reshape(t, C_MAIN), plus.reshape(t, C_MAIN), nyq.reshape(t, LANES),
                       x2d, c_w_out)
    else:
        x2d = _fourier_long(x2d, norm_g[2].reshape(1, D_MODEL), wc, c_w_out, bsz, seq)

    x2d = layer_a(x2d, 3, 1, True)
    return x2d.reshape(bsz, seq, D_MODEL)


def kernel(x_prompt, x_sample, norm_g, final_norm_g, a_w_in, a_sink, a_w_out, b_w_in, b_decay,
           b_w_out, c_w_in, c_w_out):
    wa = [_weights_a(a_w_in[j]) for j in range(a_w_in.shape[0])]
    a_out = [a_w_out[j].astype(BF16) for j in range(a_w_out.shape[0])]
    wb = b_w_in[0].astype(BF16)
    log_g = jax.nn.log_sigmoid(b_decay[0].astype(F32))
    wc = _weights_c(c_w_in[0])
    args = (norm_g, final_norm_g, wa, a_sink, a_out, wb, log_g, b_w_out[0].astype(BF16),
            wc, _weights_c_out(c_w_out[0]))
    return (_trunk(x_prompt, *args), _trunk(x_sample, *args))
```

```python
import functools
import math

import jax
import jax.numpy as jnp
from jax import lax
from jax.experimental import pallas as pl
from jax.experimental.pallas import tpu as pltpu

D_MODEL = 1024
BRANCH = 2048
NORM_EPS = 1e-6

A_HEAD = 64
A_QH = 32
A_KVH = 4
A_ROT = 16
A_THETA = 500000.0
A_WIN = 128
A_QBLK = 512
A_SPAN = A_QBLK + 2 * A_WIN
A_PAIRS = A_QH // 2
A_LOOKAHEAD = 6
A_OUT_GROUPS = 2
A_VPAD = 16
LOG2E = math.log2(math.e)

B_QK = 256
B_HEADS = 4
B_V = 512
B_CHUNK = 256
B_OUT_HEADS = 4
B_THETA = 10000.0

C_GROUPS = 4
C_GDIM = 512
C_DIRECT_MAX = 2048

LANES = 128
VMEM_LIMIT = 56 * 1024 * 1024

BF16 = jnp.bfloat16
F32 = jnp.float32


def _params(sem):
    return pltpu.CompilerParams(dimension_semantics=sem, vmem_limit_bytes=VMEM_LIMIT)


def _silu(x):
    return x * jax.nn.sigmoid(x)


def _rms(x, g):
    ms = jnp.mean(x * x, axis=-1, keepdims=True)
    return (x * lax.rsqrt(ms + NORM_EPS)) * g


IN_TM = 512
IN_TN = 512


def _inproj_a_kernel(x_ref, g_ref, w_ref, tab_ref, q_ref, gate_ref, kk_ref, vt_ref):
    h = _rms(x_ref[...], g_ref[...]).astype(BF16)

    def rope(a, t0):
        return (a * tab_ref[t0] + pltpu.roll(a, LANES - A_ROT // 2, 1) * tab_ref[t0 + 1]
                + pltpu.roll(a, A_ROT // 2, 1) * tab_ref[t0 + 2])

    for c in range(4):
        acc = jnp.dot(h, w_ref[:, c * IN_TN:(c + 1) * IN_TN], preferred_element_type=F32)
        for s in range(4):
            q_ref[4 * c + s] = rope(acc[:, s * LANES:(s + 1) * LANES], 0).astype(BF16)
    for c in range(4):
        acc = jnp.dot(h, w_ref[:, BRANCH + c * IN_TN:BRANCH + (c + 1) * IN_TN],
                      preferred_element_type=F32)
        for s in range(4):
            gate_ref[4 * c + s] = acc[:, s * LANES:(s + 1) * LANES].astype(BF16)
    acc = jnp.dot(h, w_ref[:, 2 * BRANCH:], preferred_element_type=F32)
    nkv = A_KVH * A_HEAD
    lane = lax.broadcasted_iota(jnp.int32, (IN_TM, LANES), 1)
    for p in range(nkv // LANES):
        a = rope(acc[:, p * LANES:(p + 1) * LANES], 3)
        swapped = pltpu.roll(a, A_HEAD, 1)
        kk_ref[2 * p] = jnp.where(lane < A_HEAD, a, swapped).astype(BF16)
        kk_ref[2 * p + 1] = jnp.where(lane >= A_HEAD, a, swapped).astype(BF16)
    vt_ref[0] = acc[:, nkv:].T.astype(BF16)


def _inproj_a(x2d, g, w, tabs, bsz, seq):
    t = x2d.shape[0]
    nblk = seq // IN_TM
    n = w.shape[1]
    return pl.pallas_call(
        _inproj_a_kernel,
        grid=(t // IN_TM,),
        in_specs=[
            pl.BlockSpec((IN_TM, D_MODEL), lambda i: (i, 0)),
            pl.BlockSpec((1, D_MODEL), lambda i: (0, 0)),
            pl.BlockSpec((D_MODEL, n), lambda i: (0, 0)),
            pl.BlockSpec((6, IN_TM, LANES), lambda i: (0, i % nblk, 0)),
        ],
        out_specs=[
            pl.BlockSpec((A_PAIRS, IN_TM, LANES), lambda i: (0, i, 0)),
            pl.BlockSpec((A_PAIRS, IN_TM, LANES), lambda i: (0, i, 0)),
            pl.BlockSpec((A_KVH, IN_TM, LANES), lambda i: (0, i, 0)),
            pl.BlockSpec((1, A_KVH * A_HEAD, IN_TM), lambda i: (i // nblk, 0, i % nblk)),
        ],
        out_shape=[
            jax.ShapeDtypeStruct((A_PAIRS, t, LANES), BF16),
            jax.ShapeDtypeStruct((A_PAIRS, t, LANES), BF16),
            jax.ShapeDtypeStruct((A_KVH, t, LANES), BF16),
            jax.ShapeDtypeStruct((bsz, A_KVH * A_HEAD, seq), BF16),
        ],
        compiler_params=_params(("parallel",)),
        name="inproj_a",
    )(x2d, g, w, tabs)


def _inproj_kernel(x_ref, g_ref, w_ref, *rest, rope_heads):
    if rope_heads:
        tab_ref, o_ref = rest
    else:
        (o_ref,) = rest
    h = _rms(x_ref[...], g_ref[...]).astype(BF16)
    n = w_ref.shape[1]
    for c0 in range(0, n, IN_TN):
        width = min(IN_TN, n - c0)
        acc = jnp.dot(h, w_ref[:, c0:c0 + width], preferred_element_type=F32)
        for lo in range(0, width, B_QK):
            head = (c0 + lo) // B_QK
            if head < rope_heads:
                t0 = 0 if head < rope_heads // 2 else 2
                cos, sin = tab_ref[t0], tab_ref[t0 + 1]
                x1 = acc[:, lo:lo + LANES]
                x2 = acc[:, lo + LANES:lo + B_QK]
                o_ref[:, c0 + lo:c0 + lo + LANES] = (x1 * cos - x2 * sin).astype(BF16)
                o_ref[:, c0 + lo + LANES:c0 + lo + B_QK] = (x2 * cos + x1 * sin).astype(BF16)
            else:
                o_ref[:, c0 + lo:c0 + lo + B_QK] = acc[:, lo:lo + B_QK].astype(BF16)


def _inproj(x2d, g, w, tabs, seq, rope_heads):
    t = x2d.shape[0]
    n = w.shape[1]
    nblk = seq // IN_TM
    in_specs = [
        pl.BlockSpec((IN_TM, D_MODEL), lambda i: (i, 0)),
        pl.BlockSpec((1, D_MODEL), lambda i: (0, 0)),
        pl.BlockSpec((D_MODEL, n), lambda i: (0, 0)),
    ]
    args = [x2d, g, w]
    if rope_heads:
        in_specs.append(pl.BlockSpec((4, IN_TM, LANES), lambda i: (0, i % nblk, 0)))
        args.append(tabs)
    return pl.pallas_call(
        functools.partial(_inproj_kernel, rope_heads=rope_heads),
        grid=(t // IN_TM,),
        in_specs=in_specs,
        out_specs=pl.BlockSpec((IN_TM, n), lambda i: (i, 0)),
        out_shape=jax.ShapeDtypeStruct((t, n), BF16),
        compiler_params=_params(("parallel",)),
        name="inproj",
    )(*args)


def _attn_kernel(sink_ref, q_ref, gate_ref, kp_ref, kc_ref, kn_ref, vp_ref, vc_ref, vn_ref,
                 x_ref, w_ref, fg_ref, o_ref, k_ref, v_ref, bias_ref, *, final):
    i = pl.program_id(1)
    last = pl.num_programs(1) - 1
    rows = 3 * A_WIN
    nslice = A_QBLK // A_WIN
    per_kv = A_PAIRS // A_KVH

    ones_row = (lax.broadcasted_iota(jnp.int32, (A_VPAD, A_SPAN), 0) == 0).astype(BF16)
    for h in range(A_KVH):
        k_ref[h] = jnp.concatenate([kp_ref[h], kc_ref[h], kn_ref[h]], axis=0)
        v_ref[h, :A_HEAD, :] = jnp.concatenate([vp_ref[0, h], vc_ref[0, h], vn_ref[0, h]], axis=1)
        v_ref[h, A_HEAD:, :] = ones_row

    r = lax.broadcasted_iota(jnp.int32, (A_WIN, LANES), 0)
    c = lax.broadcasted_iota(jnp.int32, (A_WIN, LANES), 1)
    neg = jnp.full((A_WIN, LANES), -1e30, F32)
    top = jnp.where(r >= c, 0.0, neg)
    bot = jnp.where(r <= c, 0.0, neg)
    bias_ref[0] = jnp.where(i == 0, neg, top)
    bias_ref[1] = top
    bias_ref[2] = bot
    bias_ref[3] = jnp.where(i == last, neg, bot)
    qlane = lax.broadcasted_iota(jnp.int32, (LANES, LANES), 1)

    units = [(g, jl, jj) for g in range(A_KVH) for jl in range(per_kv) for jj in range(nslice)]

    def scores(n):
        g, jl, jj = units[n]
        q = q_ref[g * per_kv + jl, jj * LANES:(jj + 1) * LANES, :]
        zero = jnp.zeros_like(q)
        rhs = jnp.concatenate([jnp.where(qlane < A_HEAD, q, zero),
                               jnp.where(qlane >= A_HEAD, q, zero)], axis=0)
        return lax.dot_general(k_ref[g, jj * A_WIN:jj * A_WIN + rows, :], rhs,
                               (((1,), (1,)), ((), ())), preferred_element_type=F32)

    def softmax_pv(s_t, sinks, g, jj):
        ps, extra = [], []
        for a in range(2):
            blk = s_t[:, a * LANES:(a + 1) * LANES]
            parts = [blk[:A_WIN] + bias_ref[0 if jj == 0 else 1], blk[A_WIN:2 * A_WIN],
                     blk[2 * A_WIN:] + bias_ref[3 if jj == nslice - 1 else 2]]
            mx = jnp.full((1, LANES), sinks[a], F32)
            for part in parts:
                mx = jnp.maximum(mx, jnp.max(part, axis=0, keepdims=True))
            ps.append(jnp.concatenate([jnp.exp2(part - mx).astype(BF16) for part in parts], axis=0))
            extra.append(jnp.exp2(sinks[a] - mx))
        o_ext = jnp.dot(v_ref[g, :, jj * A_WIN:jj * A_WIN + rows], jnp.concatenate(ps, axis=1),
                        preferred_element_type=F32)
        den = o_ext[A_HEAD:A_HEAD + 1, :] + jnp.concatenate(extra, axis=1)
        return o_ext[:A_HEAD, :] / den

    pending = [scores(n) for n in range(A_LOOKAHEAD)]
    outs, gated = [], []
    y = x_ref[...]
    for n, (g, jl, jj) in enumerate(units):
        if n + A_LOOKAHEAD < len(units):
            pending.append(scores(n + A_LOOKAHEAD))
        j = g * per_kv + jl
        sinks = (sink_ref[2 * j] * LOG2E, sink_ref[2 * j + 1] * LOG2E)
        outs.append(softmax_pv(pending.pop(0), sinks, g, jj))
        if jj == nslice - 1:
            last_outs = outs[-nslice:]
            o_pair = jnp.concatenate(
                [jnp.concatenate([o[:, :LANES] for o in last_outs], axis=1),
                 jnp.concatenate([o[:, LANES:] for o in last_outs], axis=1)], axis=0).T
            gt = gate_ref[j].astype(F32)
            gated.append((o_pair * _silu(gt)).astype(BF16))
            if jl == per_kv - 1 and (g + 1) % A_OUT_GROUPS == 0:
                npair = per_kv * A_OUT_GROUPS
                wg = w_ref[(j + 1 - npair) * LANES:(j + 1) * LANES, :]
                y = y + jnp.dot(jnp.concatenate(gated[-npair:], axis=1), wg,
                                preferred_element_type=F32)
    if final:
        y = _rms(y, fg_ref[...])
    o_ref[...] = y


def _attention(x2d, q3, gate3, kk, vt, sink, w_out, fin_g, bsz, seq, final):
    t = x2d.shape[0]
    nqb = seq // A_QBLK
    nkb = seq // A_WIN
    nsl = A_QBLK // A_WIN
    vt4 = vt.reshape(bsz, A_KVH, A_HEAD, seq)
    grid_spec = pltpu.PrefetchScalarGridSpec(
        num_scalar_prefetch=1,
        grid=(bsz, nqb),
        in_specs=[
            pl.BlockSpec((A_PAIRS, A_QBLK, LANES), lambda b, i, s: (0, b * nqb + i, 0)),
            pl.BlockSpec((A_PAIRS, A_QBLK, LANES), lambda b, i, s: (0, b * nqb + i, 0)),
            pl.BlockSpec((A_KVH, A_WIN, LANES),
                         lambda b, i, s: (0, b * nkb + jnp.maximum(nsl * i - 1, 0), 0)),
            pl.BlockSpec((A_KVH, A_QBLK, LANES), lambda b, i, s: (0, b * nqb + i, 0)),
            pl.BlockSpec((A_KVH, A_WIN, LANES),
                         lambda b, i, s: (0, b * nkb + jnp.minimum(nsl * i + nsl, nkb - 1), 0)),
            pl.BlockSpec((1, A_KVH, A_HEAD, A_WIN),
                         lambda b, i, s: (b, 0, 0, jnp.maximum(nsl * i - 1, 0))),
            pl.BlockSpec((1, A_KVH, A_HEAD, A_QBLK), lambda b, i, s: (b, 0, 0, i)),
            pl.BlockSpec((1, A_KVH, A_HEAD, A_WIN),
                         lambda b, i, s: (b, 0, 0, jnp.minimum(nsl * i + nsl, nkb - 1))),
            pl.BlockSpec((A_QBLK, D_MODEL), lambda b, i, s: (b * nqb + i, 0)),
            pl.BlockSpec((BRANCH, D_MODEL), lambda b, i, s: (0, 0)),
            pl.BlockSpec((1, D_MODEL), lambda b, i, s: (0, 0)),
        ],
        out_specs=pl.BlockSpec((A_QBLK, D_MODEL), lambda b, i, s: (b * nqb + i, 0)),
        scratch_shapes=[
            pltpu.VMEM((A_KVH, A_SPAN, LANES), BF16),
            pltpu.VMEM((A_KVH, A_HEAD + A_VPAD, A_SPAN), BF16),
            pltpu.VMEM((4, A_WIN, LANES), F32),
        ],
    )
    return pl.pallas_call(
        functools.partial(_attn_kernel, final=final),
        grid_spec=grid_spec,
        out_shape=jax.ShapeDtypeStruct((t, D_MODEL), F32),
        compiler_params=_params(("parallel", "parallel")),
        name="attention",
    )(sink, q3, gate3, kk, kk, kk, vt4, vt4, vt4, x2d, w_out, fin_g)


def _retention_start(lg_ref, st_ref, intra_ref, dec_ref, backward):
    n = B_CHUNK
    ii = lax.broadcasted_iota(jnp.int32, (n, n), 0)
    jj = lax.broadcasted_iota(jnp.int32, (n, n), 1)
    col = lax.broadcasted_iota(jnp.int32, (n, LANES), 0).astype(F32)
    st_ref[...] = jnp.zeros_like(st_ref)
    for h in range(B_HEADS):
        lg = lg_ref[1 if backward else 0, h]
        if backward:
            mask = jj > ii
            dist = (jj - ii).astype(F32)
            dec_ref[h, 0] = jnp.exp((n - col) * lg)
            dec_ref[h, 1] = jnp.exp(col * lg)
        else:
            mask = ii >= jj
            dist = (ii - jj).astype(F32)
            dec_ref[h, 0] = jnp.exp((col + 1.0) * lg)
            dec_ref[h, 1] = jnp.exp((n - 1.0 - col) * lg)
        intra_ref[h] = jnp.where(mask, jnp.exp(jnp.where(mask, dist, 0.0) * lg), 0.0)


def _retention_heads(lg_ref, q_ref, k_ref, v_ref, st_ref, intra_ref, dec_ref, backward, emit):
    heads = range(B_HEADS)
    qs = [q_ref[0, :, h * B_QK:(h + 1) * B_QK] for h in heads]
    ks = [k_ref[0, :, h * B_QK:(h + 1) * B_QK] for h in heads]
    vs = [v_ref[0, :, h * B_V:(h + 1) * B_V] for h in heads]
    sc = [lax.dot_general(qs[h], ks[h], (((1,), (1,)), ((), ())), preferred_element_type=F32)
          for h in heads]
    cross = [jnp.dot(qs[h], st_ref[h].astype(BF16), preferred_element_type=F32) for h in heads]
    for h in heads:
        q_dec = jnp.concatenate([dec_ref[h, 0]] * (B_V // LANES), axis=1)
        k_dec = jnp.concatenate([dec_ref[h, 1]] * (B_QK // LANES), axis=1)
        c_dec = jnp.exp(jnp.full((1, B_V), B_CHUNK * lg_ref[1 if backward else 0, h], F32))
        kd_t = (ks[h].astype(F32) * k_dec).T.astype(BF16)
        st_ref[h] = st_ref[h] * c_dec + jnp.dot(kd_t, vs[h], preferred_element_type=F32)
        inner = jnp.dot((sc[h] * intra_ref[h]).astype(BF16), vs[h], preferred_element_type=F32)
        emit(h, inner + cross[h] * q_dec)


def _retention_bwd_kernel(lg_ref, q_ref, k_ref, v_ref, o_ref, st_ref, intra_ref, dec_ref):
    @pl.when(pl.program_id(1) == 0)
    def _():
        _retention_start(lg_ref, st_ref, intra_ref, dec_ref, True)

    def emit(h, o):
        o_ref[0, :, h * B_V:(h + 1) * B_V] = o.astype(BF16)

    _retention_heads(lg_ref, q_ref, k_ref, v_ref, st_ref, intra_ref, dec_ref, True, emit)


def _retention_fwd_kernel(lg_ref, q_ref, k_ref, v_ref, gate_ref, ob_ref, x_ref, w_ref, o_ref,
                          st_ref, intra_ref, dec_ref):
    @pl.when(pl.program_id(1) == 0)
    def _():
        _retention_start(lg_ref, st_ref, intra_ref, dec_ref, False)

    y = [x_ref[0]]
    parts = []

    def emit(h, o_f):
        sl = slice(h * B_V, (h + 1) * B_V)
        o = o_f + ob_ref[0, :, sl].astype(F32)
        on = o * lax.rsqrt(jnp.mean(o * o, axis=-1, keepdims=True) + NORM_EPS)
        parts.append((on * _silu(gate_ref[0, :, sl].astype(F32))).astype(BF16))
        if len(parts) == B_OUT_HEADS:
            rows = slice((h + 1 - B_OUT_HEADS) * B_V, (h + 1) * B_V)
            y[0] = y[0] + jnp.dot(jnp.concatenate(parts, axis=1), w_ref[rows, :],
                                  preferred_element_type=F32)
            parts.clear()

    _retention_heads(lg_ref, q_ref, k_ref, v_ref, st_ref, intra_ref, dec_ref, False, emit)
    o_ref[0] = y[0]


def _retention(proj3, log_g, x3, w_out):
    bsz, seq, _ = proj3.shape
    nc = seq // B_CHUNK
    nqk = B_HEADS * B_QK
    smem = pl.BlockSpec(memory_space=pltpu.SMEM)
    state = [pltpu.VMEM((B_HEADS, B_QK, B_V), F32),
             pltpu.VMEM((B_HEADS, B_CHUNK, B_CHUNK), F32),
             pltpu.VMEM((B_HEADS, 2, B_CHUNK, LANES), F32)]

    def specs(chunk):
        return [
            pl.BlockSpec((1, B_CHUNK, nqk), lambda b, t: (b, chunk(t), 0)),
            pl.BlockSpec((1, B_CHUNK, nqk), lambda b, t: (b, chunk(t), 1)),
            pl.BlockSpec((1, B_CHUNK, BRANCH), lambda b, t: (b, chunk(t), 1)),
        ]

    def rev(t):
        return nc - 1 - t

    def fwd(t):
        return t

    o_b = pl.pallas_call(
        _retention_bwd_kernel,
        grid=(bsz, nc),
        in_specs=[smem] + specs(rev),
        out_specs=pl.BlockSpec((1, B_CHUNK, BRANCH), lambda b, t: (b, rev(t), 0)),
        out_shape=jax.ShapeDtypeStruct((bsz, seq, BRANCH), BF16),
        scratch_shapes=state,
        compiler_params=_params(("parallel", "arbitrary")),
        name="retention_bwd",
    )(log_g, proj3, proj3, proj3)
    return pl.pallas_call(
        _retention_fwd_kernel,
        grid=(bsz, nc),
        in_specs=[smem] + specs(fwd) + [
            pl.BlockSpec((1, B_CHUNK, BRANCH), lambda b, t: (b, t, 2)),
            pl.BlockSpec((1, B_CHUNK, BRANCH), lambda b, t: (b, t, 0)),
            pl.BlockSpec((1, B_CHUNK, D_MODEL), lambda b, t: (b, t, 0)),
            pl.BlockSpec((BRANCH, D_MODEL), lambda b, t: (0, 0)),
        ],
        out_specs=pl.BlockSpec((1, B_CHUNK, D_MODEL), lambda b, t: (b, t, 0)),
        out_shape=jax.ShapeDtypeStruct((bsz, seq, D_MODEL), F32),
        scratch_shapes=state,
        compiler_params=_params(("parallel", "arbitrary")),
        name="retention_fwd",
    )(log_g, proj3, proj3, proj3, proj3, o_b, x3, w_out)


C_HALF = C_GDIM // 2
C_MAIN = C_GROUPS * C_HALF
C_FOLD = 2 * C_HALF + LANES


def _fold_kernel(w_ref, cs_ref, a_ref, b_ref, sp_ref):
    r = jnp.dot(w_ref[...], cs_ref[0], preferred_element_type=F32,
                precision=lax.Precision.HIGHEST)
    a_ref[...] = r[:, :C_HALF].astype(BF16)
    b_ref[...] = r[:, C_HALF:2 * C_HALF].astype(BF16)

    @pl.when(pl.program_id(0) == 0)
    def _():
        sp_ref[...] = jnp.zeros_like(sp_ref)

    sp_ref[...] += r[:, 2 * C_HALF:]


def _fold_channel_dft(w_u, cs):
    spec = pl.BlockSpec((D_MODEL, C_HALF), lambda g: (0, g))
    return pl.pallas_call(
        _fold_kernel,
        grid=(C_GROUPS,),
        in_specs=[pl.BlockSpec((D_MODEL, C_GDIM), lambda g: (0, g)),
                  pl.BlockSpec((1, C_GDIM, C_FOLD), lambda g: (g, 0, 0))],
        out_specs=[spec, spec, pl.BlockSpec((D_MODEL, LANES), lambda g: (0, 0))],
        out_shape=[jax.ShapeDtypeStruct((D_MODEL, C_MAIN), BF16)] * 2
        + [jax.ShapeDtypeStruct((D_MODEL, LANES), F32)],
        compiler_params=_params(("arbitrary",)),
        name="fold_channel_dft",
    )(w_u, cs)


DFT_ROWS = 512


def _dft_mix_kernel(cm_ref, sm_ref, a_ref, b_ref, gm_ref, gp_ref, sp_ref, gsp_ref,
                    om_ref, op_ref, osp_ref):
    n = cm_ref.shape[0]
    rc = min(n, DFT_ROWS)
    a = a_ref[0]
    b = b_ref[0]
    for r in range(n // rc):
        rows = slice(r * rc, (r + 1) * rc)
        ea = jnp.dot(cm_ref[rows, :], a, preferred_element_type=F32)
        eb = jnp.dot(sm_ref[rows, :], b, preferred_element_type=F32)
        om_ref[0, rows, :] = ((ea - eb) * _silu(gm_ref[0, rows, :].astype(F32))).astype(BF16)
        op_ref[0, rows, :] = ((ea + eb) * _silu(gp_ref[0, rows, :].astype(F32))).astype(BF16)

    @pl.when(pl.program_id(1) == 0)
    def _():
        es = jnp.dot(cm_ref[...], sp_ref[0], preferred_element_type=F32)
        osp_ref[0] = (es * _silu(gsp_ref[0].astype(F32))).astype(BF16)


def _trig(m, period, scale):
    ang = (2.0 * math.pi / period) * (m % period).astype(F32)
    return jnp.cos(ang) * scale, jnp.sin(ang) * scale


def _fourier_mix(proj3):
    bsz, seq, _ = proj3.shape
    idx = jnp.arange(seq, dtype=jnp.int32)
    cm, sm = _trig(idx[:, None] * idx[None, :], seq, seq ** -0.5)
    tc = 512
    nct = C_MAIN // tc
    mspec = pl.BlockSpec((seq, seq), lambda b, c: (0, 0))

    def dspec(width, idx):
        return pl.BlockSpec((1, seq, width), lambda b, c: (b, 0, idx(c)))

    sp_blk = 4 * C_MAIN // LANES
    return pl.pallas_call(
        _dft_mix_kernel,
        grid=(bsz, nct),
        in_specs=[mspec, mspec,
                  dspec(tc, lambda c: c), dspec(tc, lambda c: nct + c),
                  dspec(tc, lambda c: 2 * nct + c), dspec(tc, lambda c: 3 * nct + c),
                  dspec(LANES, lambda c: sp_blk), dspec(LANES, lambda c: sp_blk + 1)],
        out_specs=[dspec(tc, lambda c: c), dspec(tc, lambda c: c), dspec(LANES, lambda c: 0)],
        out_shape=[jax.ShapeDtypeStruct((bsz, seq, C_MAIN), BF16)] * 2
        + [jax.ShapeDtypeStruct((bsz, seq, LANES), BF16)],
        compiler_params=_params(("parallel", "arbitrary")),
        name="dft_mix",
    )(cm.astype(BF16), sm.astype(BF16), proj3, proj3, proj3, proj3, proj3, proj3)


C_GRP = 8
C_SUB = 64
C_ROWS = 256
C_NCHUNK = 1024


def _inproj_perm_kernel(x_ref, g_ref, w_ref, o_ref, xs_ref):
    rh = x_ref.shape[1]
    n = w_ref.shape[1]
    for i in range(C_GRP):
        xs_ref[i * rh:(i + 1) * rh, :] = x_ref[0, :, i, :]
    per_dot = C_ROWS // rh
    for j in range(C_GRP // per_dot):
        h = _rms(xs_ref[j * C_ROWS:(j + 1) * C_ROWS, :], g_ref[...]).astype(BF16)
        for c0 in range(0, n, C_NCHUNK):
            cols = slice(c0, min(c0 + C_NCHUNK, n))
            acc = jnp.dot(h, w_ref[:, cols], preferred_element_type=F32)
            for i in range(per_dot):
                o_ref[0, j * per_dot + i, :, cols] = acc[i * rh:(i + 1) * rh, :].astype(BF16)


def _inproj_perm(x4, g, w):
    bsz, r, _, _ = x4.shape
    n = w.shape[1]
    rh = min(r, C_SUB)
    return pl.pallas_call(
        _inproj_perm_kernel,
        grid=(bsz, r // C_GRP, r // rh),
        in_specs=[
            pl.BlockSpec((1, rh, C_GRP, D_MODEL), lambda b, m, s: (b, s, m, 0)),
            pl.BlockSpec((1, D_MODEL), lambda b, m, s: (0, 0)),
            pl.BlockSpec((D_MODEL, n), lambda b, m, s: (0, 0)),
        ],
        out_specs=pl.BlockSpec((1, C_GRP, rh, n), lambda b, m, s: (b, m, s, 0)),
        out_shape=jax.ShapeDtypeStruct((bsz, r, r, n), BF16),
        scratch_shapes=[pltpu.VMEM((C_GRP * rh, D_MODEL), F32)],
        compiler_params=_params(("parallel", "parallel", "parallel")),
        name="inproj_perm",
    )(x4, g, w)


def _dft_stage1_kernel(d_ref, x_ref, sp_ref, yr_ref, yi_ref, ysr_ref, ysi_ref):
    r = x_ref.shape[2]
    for i in range(C_GRP):
        dc = d_ref[i, :r, :]
        ds = d_ref[i, r:, :]
        yr_ref[0, :, i, :] = jnp.dot(dc, x_ref[0, i], preferred_element_type=F32)
        yi_ref[0, :, i, :] = jnp.dot(ds, x_ref[0, i], preferred_element_type=F32)
        ysr_ref[0, :, i, :] = jnp.dot(dc, sp_ref[0, i], preferred_element_type=F32)
        ysi_ref[0, :, i, :] = jnp.dot(ds, sp_ref[0, i], preferred_element_type=F32)


def _dft_stage1(proj4, dtab):
    bsz, r, _, _ = proj4.shape
    tc = 1024
    nct = 2 * C_MAIN // tc

    def yspec(width, idx):
        return pl.BlockSpec((1, r, C_GRP, width), lambda b, m, c: (b, 0, m, idx(c)))

    return pl.pallas_call(
        _dft_stage1_kernel,
        grid=(bsz, r // C_GRP, nct),
        in_specs=[
            pl.BlockSpec((C_GRP, 2 * r, r), lambda b, m, c: (m, 0, 0)),
            pl.BlockSpec((1, C_GRP, r, tc), lambda b, m, c: (b, m, 0, c)),
            pl.BlockSpec((1, C_GRP, r, LANES), lambda b, m, c: (b, m, 0, 4 * C_MAIN // LANES)),
        ],
        out_specs=[yspec(tc, lambda c: c), yspec(tc, lambda c: c),
                   yspec(LANES, lambda c: 0), yspec(LANES, lambda c: 0)],
        out_shape=[jax.ShapeDtypeStruct((bsz, r, r, 2 * C_MAIN), F32)] * 2
        + [jax.ShapeDtypeStruct((bsz, r, r, LANES), F32)] * 2,
        compiler_params=_params(("parallel", "parallel", "arbitrary")),
        name="dft_stage1",
    )(dtab, proj4, proj4)


def _dft_stage2_kernel(c1_ref, s1_ref, yra_ref, yia_ref, yrb_ref, yib_ref, gm_ref, gp_ref,
                       wm_ref, wp_ref, ysr_ref, ysi_ref, gsp_ref, wsp_ref, x_ref, o_ref, acc_ref):
    c = pl.program_id(2)
    r = c1_ref.shape[0]
    c1 = c1_ref[...]
    s1 = s1_ref[...]

    def dot(m, y):
        return jnp.dot(m, y.astype(BF16), preferred_element_type=F32)

    @pl.when(c == 0)
    def _():
        parts = []
        for i in range(C_GRP):
            es = dot(c1, ysr_ref[0, i]) - dot(s1, ysi_ref[0, i])
            parts.append((es * _silu(gsp_ref[0, i].astype(F32))).astype(BF16))
        nyq = jnp.dot(jnp.concatenate(parts, axis=0), wsp_ref[...], preferred_element_type=F32)
        for i in range(C_GRP):
            acc_ref[i * r:(i + 1) * r, :] = x_ref[0, :, i, :] + nyq[i * r:(i + 1) * r, :]

    minus, plus = [], []
    for i in range(C_GRP):
        ea = dot(c1, yra_ref[0, i]) - dot(s1, yia_ref[0, i])
        eb = dot(c1, yib_ref[0, i]) + dot(s1, yrb_ref[0, i])
        minus.append(((ea - eb) * _silu(gm_ref[0, i].astype(F32))).astype(BF16))
        plus.append(((ea + eb) * _silu(gp_ref[0, i].astype(F32))).astype(BF16))
    acc_ref[...] += (jnp.dot(jnp.concatenate(minus, axis=0), wm_ref[...], preferred_element_type=F32)
                     + jnp.dot(jnp.concatenate(plus, axis=0), wp_ref[...], preferred_element_type=F32))

    @pl.when(c == pl.num_programs(2) - 1)
    def _():
        for i in range(C_GRP):
            o_ref[0, :, i, :] = acc_ref[i * r:(i + 1) * r, :]


def _dft_stage2(c1, s1, y_r, y_i, y_sr, y_si, proj4, w_out, x4):
    bsz, r, _, _ = x4.shape
    tc = 512
    nct = C_MAIN // tc
    mspec = pl.BlockSpec((r, r), lambda b, m, c: (0, 0))

    def slab(width, idx):
        return pl.BlockSpec((1, C_GRP, r, width), lambda b, m, c: (b, m, 0, idx(c)))

    def wspec(rows, idx):
        return pl.BlockSpec((rows, D_MODEL), lambda b, m, c: (idx(c), 0))

    xspec = pl.BlockSpec((1, r, C_GRP, D_MODEL), lambda b, m, c: (b, 0, m, 0))
    sp_blk = 4 * C_MAIN // LANES
    return pl.pallas_call(
        _dft_stage2_kernel,
        grid=(bsz, r // C_GRP, nct),
        in_specs=[
            mspec, mspec,
            slab(tc, lambda c: c), slab(tc, lambda c: c),
            slab(tc, lambda c: nct + c), slab(tc, lambda c: nct + c),
            slab(tc, lambda c: 2 * nct + c), slab(tc, lambda c: 3 * nct + c),
            wspec(tc, lambda c: c), wspec(tc, lambda c: nct + c),
            slab(LANES, lambda c: 0), slab(LANES, lambda c: 0),
            slab(LANES, lambda c: sp_blk + 1),
            wspec(LANES, lambda c: 2 * C_MAIN // LANES),
            xspec,
        ],
        out_specs=xspec,
        out_shape=jax.ShapeDtypeStruct(x4.shape, F32),
        scratch_shapes=[pltpu.VMEM((C_GRP * r, D_MODEL), F32)],
        compiler_params=_params(("parallel", "parallel", "arbitrary")),
        name="dft_stage2",
    )(c1, s1, y_r, y_i, y_r, y_i, proj4, proj4, w_out, w_out, y_sr, y_si, proj4, w_out, x4)


def _fourier_long(x2d, g, wc, w_out, bsz, seq):
    r = math.isqrt(seq)
    assert r * r == seq and r % C_GRP == 0
    x4 = x2d.reshape(bsz, r, r, D_MODEL)
    proj4 = _inproj_perm(x4, g, wc)
    idx = jnp.arange(r, dtype=jnp.int32)
    tok = idx[:, None, None] + r * idx[None, None, :]
    dc, ds = _trig(idx[None, :, None] * tok, seq, r ** -0.5)
    dtab = jnp.concatenate([dc, ds], axis=1).astype(BF16)
    y_r, y_i, y_sr, y_si = _dft_stage1(proj4, dtab)
    c1, s1 = _trig(idx[:, None] * idx[None, :], r, r ** -0.5)
    out = _dft_stage2(c1.astype(BF16), s1.astype(BF16), y_r, y_i, y_sr, y_si, proj4, w_out, x4)
    return out.reshape(bsz * seq, D_MODEL)


OUT_TM = 512


def _outproj_kernel(mm_ref, mp_ref, ms_ref, x_ref, w_ref, o_ref):
    acc = jnp.dot(mm_ref[...], w_ref[:C_MAIN, :], preferred_element_type=F32)
    acc = acc + jnp.dot(mp_ref[...], w_ref[C_MAIN:2 * C_MAIN, :], preferred_element_type=F32)
    acc = acc + jnp.dot(ms_ref[...], w_ref[2 * C_MAIN:, :], preferred_element_type=F32)
    o_ref[...] = x_ref[...] + acc


def _outproj(minus2d, plus2d, nyq2d, x2d, w_out):
    t = x2d.shape[0]
    return pl.pallas_call(
        _outproj_kernel,
        grid=(t // OUT_TM,),
        in_specs=[
            pl.BlockSpec((OUT_TM, C_MAIN), lambda i: (i, 0)),
            pl.BlockSpec((OUT_TM, C_MAIN), lambda i: (i, 0)),
            pl.BlockSpec((OUT_TM, LANES), lambda i: (i, 0)),
            pl.BlockSpec((OUT_TM, D_MODEL), lambda i: (i, 0)),
            pl.BlockSpec((2 * C_MAIN + LANES, D_MODEL), lambda i: (0, 0)),
        ],
        out_specs=pl.BlockSpec((OUT_TM, D_MODEL), lambda i: (i, 0)),
        out_shape=jax.ShapeDtypeStruct((t, D_MODEL), F32),
        compiler_params=_params(("parallel",)),
        name="outproj",
    )(minus2d, plus2d, nyq2d, x2d, w_out)


def _rope_tables_a(seq):
    half = A_ROT // 2
    inv_freq = jnp.exp(-(jnp.arange(half, dtype=F32) * (2.0 / A_ROT)) * math.log(A_THETA))
    ang = jnp.arange(seq, dtype=F32)[:, None] * inv_freq[None, :]
    cos, sin = jnp.cos(ang), jnp.sin(ang)
    d = jnp.arange(LANES) % A_HEAD
    f = d % half
    cm = jnp.where(d[None, :] < A_ROT, cos[:, f], 1.0)
    s1 = jnp.where(d[None, :] < half, -sin[:, f], 0.0)
    s2 = jnp.where((d[None, :] >= half) & (d[None, :] < A_ROT), sin[:, f], 0.0)
    qs = A_HEAD ** -0.5 * LOG2E
    return jnp.stack([cm * qs, s1 * qs, s2 * qs, cm, s1, s2]).astype(F32)


def _rope_tables_b(seq):
    half = B_QK // 2
    inv_freq = jnp.exp(-(jnp.arange(half, dtype=F32) * (2.0 / B_QK)) * math.log(B_THETA))
    ang = jnp.arange(seq, dtype=F32)[:, None] * inv_freq[None, :]
    cos, sin = jnp.cos(ang), jnp.sin(ang)
    ks = B_QK ** -0.5
    return jnp.stack([cos, sin, cos * ks, sin * ks]).astype(F32)


def _weights_a(w_in):
    nq = A_QH * A_HEAD
    nkv = A_KVH * A_HEAD
    q = w_in[:, :nq]
    kv = w_in[:, nq:nq + 2 * nkv]
    gate = w_in[:, nq + 2 * nkv:]
    return jnp.concatenate([q, gate, kv], axis=1).astype(BF16)


def _c_index():
    g = jnp.arange(C_GROUPS, dtype=jnp.int32)[:, None] * C_GDIM
    m = jnp.arange(C_HALF, dtype=jnp.int32)[None, :]
    minus = (g + m).reshape(-1)
    plus = (g + jnp.where(m == 0, 0, C_GDIM - m)).reshape(-1)
    nyq = (g + C_HALF).reshape(-1)
    return minus, plus, nyq, jnp.tile(m == 0, (C_GROUPS, 1)).reshape(-1)


def _weights_c(w_in):
    ch = jnp.arange(C_GDIM, dtype=jnp.int32)
    m = jnp.arange(C_HALF, dtype=jnp.int32)
    cm, sm = _trig(ch[:, None] * m[None, :], C_GDIM, C_GDIM ** -0.5)
    nyq, _ = _trig(ch * C_HALF, C_GDIM, C_GDIM ** -0.5)
    lane = jnp.arange(LANES)[None, None, :] == jnp.arange(C_GROUPS)[:, None, None]
    nyq = jnp.where(lane, nyq[None, :, None], 0.0)
    cs = jnp.concatenate([jnp.broadcast_to(cm, (C_GROUPS,) + cm.shape),
                          jnp.broadcast_to(sm, (C_GROUPS,) + sm.shape), nyq], axis=2)
    w_a, w_b, w_sp = _fold_channel_dft(w_in[:, :BRANCH], cs)
    minus, plus, nyq_idx, _ = _c_index()
    gate = w_in[:, BRANCH:]
    g_sp = jnp.zeros((D_MODEL, LANES), F32).at[:, :C_GROUPS].set(gate[:, nyq_idx])
    return jnp.concatenate([w_a, w_b, gate[:, minus].astype(BF16), gate[:, plus].astype(BF16),
                            w_sp.astype(BF16), g_sp.astype(BF16)], axis=1)


def _weights_c_out(w_out):
    minus, plus, nyq_idx, dup = _c_index()
    w_plus = jnp.where(dup[:, None], 0.0, w_out[plus])
    w_sp = jnp.zeros((LANES, D_MODEL), F32).at[:C_GROUPS].set(w_out[nyq_idx])
    return jnp.concatenate([w_out[minus], w_plus, w_sp], axis=0).astype(BF16)


def _trunk(x, norm_g, fin_g, wa, a_sink, a_w_out, wb, log_g, b_w_out, wc, c_w_out):
    bsz, seq, _ = x.shape
    t = bsz * seq
    x2d = x.reshape(t, D_MODEL)
    tabs_a = _rope_tables_a(seq)
    tabs_b = _rope_tables_b(seq)
    fin = fin_g.reshape(1, D_MODEL)

    def layer_a(x2d, layer, j, final):
        q3, gate3, kk, vt = _inproj_a(x2d, norm_g[layer].reshape(1, D_MODEL), wa[j], tabs_a, bsz, seq)
        return _attention(x2d, q3, gate3, kk, vt, a_sink[j], a_w_out[j], fin, bsz, seq, final)

    x2d = layer_a(x2d, 0, 0, False)

    proj = _inproj(x2d, norm_g[1].reshape(1, D_MODEL), wb, tabs_b, seq, 2 * B_HEADS)
    x2d = _retention(proj.reshape(bsz, seq, -1), log_g, x2d.reshape(bsz, seq, D_MODEL),
                     b_w_out).reshape(t, D_MODEL)

    if seq <= C_DIRECT_MAX:
        proj = _inproj(x2d, norm_g[2].reshape(1, D_MODEL), wc, None, seq, 0)
        minus, plus, nyq = _fourier_mix(proj.reshape(bsz, seq, -1))
        x2d = _outproj(minus.reshape(t, C_MAIN), plus.reshape(t, C_MAIN), nyq.reshape(t, LANES),
                       x2d, c_w_out)
    else:
        x2d = _fourier_long(x2d, norm_g[2].reshape(1, D_MODEL), wc, c_w_out, bsz, seq)

    x2d = layer_a(x2d, 3, 1, True)
    return x2d.reshape(bsz, seq, D_MODEL)


def kernel(x_prompt, x_sample, norm_g, final_norm_g, a_w_in, a_sink, a_w_out, b_w_in, b_decay,
           b_w_out, c_w_in, c_w_out):
    wa = [_weights_a(a_w_in[j]) for j in range(a_w_in.shape[0])]
    a_out = [a_w_out[j].astype(BF16) for j in range(a_w_out.shape[0])]
    wb = b_w_in[0].astype(BF16)
    log_g = jax.nn.log_sigmoid(b_decay[0].astype(F32))
    wc = _weights_c(c_w_in[0])
    args = (norm_g, final_norm_g, wa, a_sink, a_out, wb, log_g, b_w_out[0].astype(BF16),
            wc, _weights_c_out(c_w_out[0]))
    return (_trunk(x_prompt, *args), _trunk(x_sample, *args))
```

```python
import functools
import math

import jax
import jax.numpy as jnp
from jax import lax
from jax.experimental import pallas as pl
from jax.experimental.pallas import tpu as pltpu

D_MODEL = 1024
BRANCH = 2048
NORM_EPS = 1e-6

A_HEAD = 64
A_QH = 32
A_KVH = 4
A_ROT = 16
A_THETA = 500000.0
A_WIN = 128
A_QBLK = 512
A_SPAN = A_QBLK + 2 * A_WIN
A_PAIRS = A_QH // 2
A_LOOKAHEAD = 8
A_OUT_GROUPS = 2
A_VPAD = 16
LOG2E = math.log2(math.e)

B_QK = 256
B_HEADS = 4
B_V = 512
B_CHUNK = 256
B_ROWS = 2
B_THETA = 10000.0

C_GROUPS = 4
C_GDIM = 512
C_DIRECT_MAX = 2048

LANES = 128
VMEM_LIMIT = 56 * 1024 * 1024

BF16 = jnp.bfloat16
F32 = jnp.float32


def _params(sem):
    return pltpu.CompilerParams(dimension_semantics=sem, vmem_limit_bytes=VMEM_LIMIT)


def _silu(x):
    return x * jax.nn.sigmoid(x)


def _rms(x, g):
    ms = jnp.mean(x * x, axis=-1, keepdims=True)
    return (x * lax.rsqrt(ms + NORM_EPS)) * g


IN_TM = 512
IN_TN = 512


def _inproj_a_kernel(x_ref, g_ref, w_ref, tab_ref, q_ref, gate_ref, kk_ref, vt_ref):
    h = _rms(x_ref[...], g_ref[...]).astype(BF16)

    def rope(a, t0):
        return (a * tab_ref[t0] + pltpu.roll(a, LANES - A_ROT // 2, 1) * tab_ref[t0 + 1]
                + pltpu.roll(a, A_ROT // 2, 1) * tab_ref[t0 + 2])

    for c in range(4):
        acc = jnp.dot(h, w_ref[:, c * IN_TN:(c + 1) * IN_TN], preferred_element_type=F32)
        for s in range(4):
            q_ref[4 * c + s] = rope(acc[:, s * LANES:(s + 1) * LANES], 0).astype(BF16)
    for c in range(4):
        acc = jnp.dot(h, w_ref[:, BRANCH + c * IN_TN:BRANCH + (c + 1) * IN_TN],
                      preferred_element_type=F32)
        for s in range(4):
            gate_ref[4 * c + s] = acc[:, s * LANES:(s + 1) * LANES].astype(BF16)
    acc = jnp.dot(h, w_ref[:, 2 * BRANCH:], preferred_element_type=F32)
    nkv = A_KVH * A_HEAD
    lane = lax.broadcasted_iota(jnp.int32, (IN_TM, LANES), 1)
    for p in range(nkv // LANES):
        a = rope(acc[:, p * LANES:(p + 1) * LANES], 3)
        swapped = pltpu.roll(a, A_HEAD, 1)
        kk_ref[2 * p] = jnp.where(lane < A_HEAD, a, swapped).astype(BF16)
        kk_ref[2 * p + 1] = jnp.where(lane >= A_HEAD, a, swapped).astype(BF16)
    vt_ref[0] = acc[:, nkv:].T.astype(BF16)


def _inproj_a(x2d, g, w, tabs, bsz, seq):
    t = x2d.shape[0]
    nblk = seq // IN_TM
    n = w.shape[1]
    return pl.pallas_call(
        _inproj_a_kernel,
        grid=(t // IN_TM,),
        in_specs=[
            pl.BlockSpec((IN_TM, D_MODEL), lambda i: (i, 0)),
            pl.BlockSpec((1, D_MODEL), lambda i: (0, 0)),
            pl.BlockSpec((D_MODEL, n), lambda i: (0, 0)),
            pl.BlockSpec((6, IN_TM, LANES), lambda i: (0, i % nblk, 0)),
        ],
        out_specs=[
            pl.BlockSpec((A_PAIRS, IN_TM, LANES), lambda i: (0, i, 0)),
            pl.BlockSpec((A_PAIRS, IN_TM, LANES), lambda i: (0, i, 0)),
            pl.BlockSpec((A_KVH, IN_TM, LANES), lambda i: (0, i, 0)),
            pl.BlockSpec((1, A_KVH * A_HEAD, IN_TM), lambda i: (i // nblk, 0, i % nblk)),
        ],
        out_shape=[
            jax.ShapeDtypeStruct((A_PAIRS, t, LANES), BF16),
            jax.ShapeDtypeStruct((A_PAIRS, t, LANES), BF16),
            jax.ShapeDtypeStruct((A_KVH, t, LANES), BF16),
            jax.ShapeDtypeStruct((bsz, A_KVH * A_HEAD, seq), BF16),
        ],
        compiler_params=_params(("parallel",)),
        name="inproj_a",
    )(x2d, g, w, tabs)


def _inproj_kernel(x_ref, g_ref, w_ref, *rest, rope_heads):
    if rope_heads:
        tab_ref, o_ref = rest
    else:
        (o_ref,) = rest
    h = _rms(x_ref[...], g_ref[...]).astype(BF16)
    n = w_ref.shape[1]
    for c0 in range(0, n, IN_TN):
        width = min(IN_TN, n - c0)
        acc = jnp.dot(h, w_ref[:, c0:c0 + width], preferred_element_type=F32)
        for lo in range(0, width, B_QK):
            head = (c0 + lo) // B_QK
            if head < rope_heads:
                t0 = 0 if head < rope_heads // 2 else 2
                cos, sin = tab_ref[t0], tab_ref[t0 + 1]
                x1 = acc[:, lo:lo + LANES]
                x2 = acc[:, lo + LANES:lo + B_QK]
                o_ref[:, c0 + lo:c0 + lo + LANES] = (x1 * cos - x2 * sin).astype(BF16)
                o_ref[:, c0 + lo + LANES:c0 + lo + B_QK] = (x2 * cos + x1 * sin).astype(BF16)
            else:
                o_ref[:, c0 + lo:c0 + lo + B_QK] = acc[:, lo:lo + B_QK].astype(BF16)


def _inproj(x2d, g, w, tabs, seq, rope_heads):
    t = x2d.shape[0]
    n = w.shape[1]
    nblk = seq // IN_TM
    in_specs = [
        pl.BlockSpec((IN_TM, D_MODEL), lambda i: (i, 0)),
        pl.BlockSpec((1, D_MODEL), lambda i: (0, 0)),
        pl.BlockSpec((D_MODEL, n), lambda i: (0, 0)),
    ]
    args = [x2d, g, w]
    if rope_heads:
        in_specs.append(pl.BlockSpec((4, IN_TM, LANES), lambda i: (0, i % nblk, 0)))
        args.append(tabs)
    return pl.pallas_call(
        functools.partial(_inproj_kernel, rope_heads=rope_heads),
        grid=(t // IN_TM,),
        in_specs=in_specs,
        out_specs=pl.BlockSpec((IN_TM, n), lambda i: (i, 0)),
        out_shape=jax.ShapeDtypeStruct((t, n), BF16),
        compiler_params=_params(("parallel",)),
        name="inproj",
    )(*args)


def _attn_kernel(sink_ref, q_ref, gate_ref, kp_ref, kc_ref, kn_ref, vp_ref, vc_ref, vn_ref,
                 x_ref, w_ref, fg_ref, o_ref, k_ref, v_ref, bias_ref, *, final):
    i = pl.program_id(1)
    last = pl.num_programs(1) - 1
    rows = 3 * A_WIN
    nslice = A_QBLK // A_WIN
    per_kv = A_PAIRS // A_KVH

    ones_row = (lax.broadcasted_iota(jnp.int32, (A_VPAD, A_SPAN), 0) == 0).astype(BF16)
    for h in range(A_KVH):
        k_ref[h] = jnp.concatenate([kp_ref[h], kc_ref[h], kn_ref[h]], axis=0)
        v_ref[h, :A_HEAD, :] = jnp.concatenate([vp_ref[0, h], vc_ref[0, h], vn_ref[0, h]], axis=1)
        v_ref[h, A_HEAD:, :] = ones_row

    r = lax.broadcasted_iota(jnp.int32, (A_WIN, LANES), 0)
    c = lax.broadcasted_iota(jnp.int32, (A_WIN, LANES), 1)
    neg = jnp.full((A_WIN, LANES), -1e30, F32)
    top = jnp.where(r >= c, 0.0, neg)
    bot = jnp.where(r <= c, 0.0, neg)
    bias_ref[0] = jnp.where(i == 0, neg, top)
    bias_ref[1] = top
    bias_ref[2] = bot
    bias_ref[3] = jnp.where(i == last, neg, bot)
    qlane = lax.broadcasted_iota(jnp.int32, (LANES, LANES), 1)

    units = [(g, jl, jj) for g in range(A_KVH) for jl in range(per_kv) for jj in range(nslice)]

    def scores(n):
        g, jl, jj = units[n]
        q = q_ref[g * per_kv + jl, jj * LANES:(jj + 1) * LANES, :]
        zero = jnp.zeros_like(q)
        rhs = jnp.concatenate([jnp.where(qlane < A_HEAD, q, zero),
                               jnp.where(qlane >= A_HEAD, q, zero)], axis=0)
        return lax.dot_general(k_ref[g, jj * A_WIN:jj * A_WIN + rows, :], rhs,
                               (((1,), (1,)), ((), ())), preferred_element_type=F32)

    def softmax_pv(s_t, sinks, g, jj):
        ps, extra = [], []
        for a in range(2):
            blk = s_t[:, a * LANES:(a + 1) * LANES]
            parts = [blk[:A_WIN] + bias_ref[0 if jj == 0 else 1], blk[A_WIN:2 * A_WIN],
                     blk[2 * A_WIN:] + bias_ref[3 if jj == nslice - 1 else 2]]
            mx = jnp.full((1, LANES), sinks[a], F32)
            for part in parts:
                mx = jnp.maximum(mx, jnp.max(part, axis=0, keepdims=True))
            ps.append(jnp.concatenate([jnp.exp2(part - mx).astype(BF16) for part in parts], axis=0))
            extra.append(jnp.exp2(sinks[a] - mx))
        o_ext = jnp.dot(v_ref[g, :, jj * A_WIN:jj * A_WIN + rows], jnp.concatenate(ps, axis=1),
                        preferred_element_type=F32)
        den = o_ext[A_HEAD:A_HEAD + 1, :] + jnp.concatenate(extra, axis=1)
        return o_ext[:A_HEAD, :] / den

    pending = [scores(n) for n in range(A_LOOKAHEAD)]
    outs, gated = [], []
    y = x_ref[...]
    for n, (g, jl, jj) in enumerate(units):
        if n + A_LOOKAHEAD < len(units):
            pending.append(scores(n + A_LOOKAHEAD))
        j = g * per_kv + jl
        sinks = (sink_ref[2 * j] * LOG2E, sink_ref[2 * j + 1] * LOG2E)
        outs.append(softmax_pv(pending.pop(0), sinks, g, jj))
        if jj == nslice - 1:
            last_outs = outs[-nslice:]
            o_pair = jnp.concatenate(
                [jnp.concatenate([o[:, :LANES] for o in last_outs], axis=1),
                 jnp.concatenate([o[:, LANES:] for o in last_outs], axis=1)], axis=0).T
            gt = gate_ref[j].astype(F32)
            gated.append((o_pair * _silu(gt)).astype(BF16))
            if jl == per_kv - 1 and (g + 1) % A_OUT_GROUPS == 0:
                npair = per_kv * A_OUT_GROUPS
                wg = w_ref[(j + 1 - npair) * LANES:(j + 1) * LANES, :]
                y = y + jnp.dot(jnp.concatenate(gated[-npair:], axis=1), wg,
                                preferred_element_type=F32)
    if final:
        y = _rms(y, fg_ref[...])
    o_ref[...] = y


def _attention(x2d, q3, gate3, kk, vt, sink, w_out, fin_g, bsz, seq, final):
    t = x2d.shape[0]
    nqb = seq // A_QBLK
    nkb = seq // A_WIN
    nsl = A_QBLK // A_WIN
    vt4 = vt.reshape(bsz, A_KVH, A_HEAD, seq)
    grid_spec = pltpu.PrefetchScalarGridSpec(
        num_scalar_prefetch=1,
        grid=(bsz, nqb),
        in_specs=[
            pl.BlockSpec((A_PAIRS, A_QBLK, LANES), lambda b, i, s: (0, b * nqb + i, 0)),
            pl.BlockSpec((A_PAIRS, A_QBLK, LANES), lambda b, i, s: (0, b * nqb + i, 0)),
            pl.BlockSpec((A_KVH, A_WIN, LANES),
                         lambda b, i, s: (0, b * nkb + jnp.maximum(nsl * i - 1, 0), 0)),
            pl.BlockSpec((A_KVH, A_QBLK, LANES), lambda b, i, s: (0, b * nqb + i, 0)),
            pl.BlockSpec((A_KVH, A_WIN, LANES),
                         lambda b, i, s: (0, b * nkb + jnp.minimum(nsl * i + nsl, nkb - 1), 0)),
            pl.BlockSpec((1, A_KVH, A_HEAD, A_WIN),
                         lambda b, i, s: (b, 0, 0, jnp.maximum(nsl * i - 1, 0))),
            pl.BlockSpec((1, A_KVH, A_HEAD, A_QBLK), lambda b, i, s: (b, 0, 0, i)),
            pl.BlockSpec((1, A_KVH, A_HEAD, A_WIN),
                         lambda b, i, s: (b, 0, 0, jnp.minimum(nsl * i + nsl, nkb - 1))),
            pl.BlockSpec((A_QBLK, D_MODEL), lambda b, i, s: (b * nqb + i, 0)),
            pl.BlockSpec((BRANCH, D_MODEL), lambda b, i, s: (0, 0)),
            pl.BlockSpec((1, D_MODEL), lambda b, i, s: (0, 0)),
        ],
        out_specs=pl.BlockSpec((A_QBLK, D_MODEL), lambda b, i, s: (b * nqb + i, 0)),
        scratch_shapes=[
            pltpu.VMEM((A_KVH, A_SPAN, LANES), BF16),
            pltpu.VMEM((A_KVH, A_HEAD + A_VPAD, A_SPAN), BF16),
            pltpu.VMEM((4, A_WIN, LANES), F32),
        ],
    )
    return pl.pallas_call(
        functools.partial(_attn_kernel, final=final),
        grid_spec=grid_spec,
        out_shape=jax.ShapeDtypeStruct((t, D_MODEL), F32),
        compiler_params=_params(("parallel", "parallel")),
        name="attention",
    )(sink, q3, gate3, kk, kk, kk, vt4, vt4, vt4, x2d, w_out, fin_g)


def _retention_start(lg_ref, st_ref, intra_ref, dec_ref, backward):
    n = B_CHUNK
    ii = lax.broadcasted_iota(jnp.int32, (n, n), 0)
    jj = lax.broadcasted_iota(jnp.int32, (n, n), 1)
    col = lax.broadcasted_iota(jnp.int32, (n, LANES), 0).astype(F32)
    st_ref[...] = jnp.zeros_like(st_ref)
    for h in range(B_HEADS):
        lg = lg_ref[1 if backward else 0, h]
        if backward:
            mask = jj > ii
            dist = (jj - ii).astype(F32)
            dec_ref[h, 0] = jnp.exp((n - col) * lg)
            dec_ref[h, 1] = jnp.exp(col * lg)
        else:
            mask = ii >= jj
            dist = (ii - jj).astype(F32)
            dec_ref[h, 0] = jnp.exp((col + 1.0) * lg)
            dec_ref[h, 1] = jnp.exp((n - 1.0 - col) * lg)
        intra_ref[h] = jnp.where(mask, jnp.exp(jnp.where(mask, dist, 0.0) * lg), 0.0)


def _retention_open(q_ref, k_ref, v_ref, row):
    heads = range(B_HEADS)
    qs = [q_ref[row, :, h * B_QK:(h + 1) * B_QK] for h in heads]
    ks = [k_ref[row, :, h * B_QK:(h + 1) * B_QK] for h in heads]
    vs = [v_ref[row, :, h * B_V:(h + 1) * B_V] for h in heads]
    sc = [lax.dot_general(qs[h], ks[h], (((1,), (1,)), ((), ())), preferred_element_type=F32)
          for h in heads]
    return qs, ks, vs, sc


def _retention_heads(lg_ref, opened, st_ref, intra_ref, dec_ref, row, backward, emit):
    qs, ks, vs, sc = opened
    for h in range(B_HEADS):
        q_dec = jnp.concatenate([dec_ref[h, 0]] * (B_QK // LANES), axis=1)
        k_dec = jnp.concatenate([dec_ref[h, 1]] * (B_QK // LANES), axis=1)
        c_dec = jnp.exp(jnp.full((1, B_V), B_CHUNK * lg_ref[1 if backward else 0, h], F32))
        st = st_ref[row, h]
        lhs = jnp.concatenate([(sc[h] * intra_ref[h]).astype(BF16),
                               (qs[h].astype(F32) * q_dec).astype(BF16)], axis=1)
        rhs = jnp.concatenate([vs[h], st.astype(BF16)], axis=0)
        out = jnp.dot(lhs, rhs, preferred_element_type=F32)
        kd_t = (ks[h].astype(F32) * k_dec).T.astype(BF16)
        st_ref[row, h] = st * c_dec + jnp.dot(kd_t, vs[h], preferred_element_type=F32)
        emit(h, out)


def _retention_bwd_kernel(lg_ref, q_ref, k_ref, v_ref, o_ref, st_ref, intra_ref, dec_ref):
    @pl.when(pl.program_id(1) == 0)
    def _():
        _retention_start(lg_ref, st_ref, intra_ref, dec_ref, True)

    rows = range(q_ref.shape[0])
    opened = [_retention_open(q_ref, k_ref, v_ref, row) for row in rows]
    for row in rows:
        def emit(h, o, row=row):
            o_ref[row, :, h * B_V:(h + 1) * B_V] = o.astype(BF16)

        _retention_heads(lg_ref, opened[row], st_ref, intra_ref, dec_ref, row, True, emit)


def _retention_fwd_kernel(lg_ref, q_ref, k_ref, v_ref, gate_ref, ob_ref, x_ref, w_ref, o_ref,
                          st_ref, intra_ref, dec_ref):
    @pl.when(pl.program_id(1) == 0)
    def _():
        _retention_start(lg_ref, st_ref, intra_ref, dec_ref, False)

    rows = range(q_ref.shape[0])
    opened = [_retention_open(q_ref, k_ref, v_ref, row) for row in rows]
    gated = []
    for row in rows:
        parts = []

        def emit(h, o_f, row=row, parts=parts):
            sl = slice(h * B_V, (h + 1) * B_V)
            o = o_f + ob_ref[row, :, sl].astype(F32)
            on = o * lax.rsqrt(jnp.mean(o * o, axis=-1, keepdims=True) + NORM_EPS)
            parts.append((on * _silu(gate_ref[row, :, sl].astype(F32))).astype(BF16))

        _retention_heads(lg_ref, opened[row], st_ref, intra_ref, dec_ref, row, False, emit)
        gated.append(jnp.concatenate(parts, axis=1))
    for row in rows:
        o_ref[row] = x_ref[row] + jnp.dot(gated[row], w_ref[...], preferred_element_type=F32)


def _retention(proj3, log_g, x3, w_out):
    bsz, seq, _ = proj3.shape
    nc = seq // B_CHUNK
    nqk = B_HEADS * B_QK
    rows = B_ROWS if bsz % B_ROWS == 0 else 1
    smem = pl.BlockSpec(memory_space=pltpu.SMEM)
    state = [pltpu.VMEM((rows, B_HEADS, B_QK, B_V), F32),
             pltpu.VMEM((B_HEADS, B_CHUNK, B_CHUNK), F32),
             pltpu.VMEM((B_HEADS, 2, B_CHUNK, LANES), F32)]

    def blk(width, chunk, col):
        return pl.BlockSpec((rows, B_CHUNK, width), lambda b, t: (b, chunk(t), col))

    def rev(t):
        return nc - 1 - t

    def fwd(t):
        return t

    o_b = pl.pallas_call(
        _retention_bwd_kernel,
        grid=(bsz // rows, nc),
        in_specs=[smem, blk(nqk, rev, 0), blk(nqk, rev, 1), blk(BRANCH, rev, 1)],
        out_specs=blk(BRANCH, rev, 0),
        out_shape=jax.ShapeDtypeStruct((bsz, seq, BRANCH), BF16),
        scratch_shapes=state,
        compiler_params=_params(("parallel", "arbitrary")),
        name="retention_bwd",
    )(log_g, proj3, proj3, proj3)
    return pl.pallas_call(
        _retention_fwd_kernel,
        grid=(bsz // rows, nc),
        in_specs=[smem, blk(nqk, fwd, 0), blk(nqk, fwd, 1), blk(BRANCH, fwd, 1),
                  blk(BRANCH, fwd, 2), blk(BRANCH, fwd, 0), blk(D_MODEL, fwd, 0),
                  pl.BlockSpec((BRANCH, D_MODEL), lambda b, t: (0, 0))],
        out_specs=blk(D_MODEL, fwd, 0),
        out_shape=jax.ShapeDtypeStruct((bsz, seq, D_MODEL), F32),
        scratch_shapes=state,
        compiler_params=_params(("parallel", "arbitrary")),
        name="retention_fwd",
    )(log_g, proj3, proj3, proj3, proj3, o_b, x3, w_out)


C_HALF = C_GDIM // 2
C_MAIN = C_GROUPS * C_HALF
C_FOLD = 2 * C_HALF + LANES


def _fold_kernel(w_ref, cs_ref, a_ref, b_ref, sp_ref):
    r = jnp.dot(w_ref[...], cs_ref[0], preferred_element_type=F32,
                precision=lax.Precision.HIGHEST)
    a_ref[...] = r[:, :C_HALF].astype(BF16)
    b_ref[...] = r[:, C_HALF:2 * C_HALF].astype(BF16)

    @pl.when(pl.program_id(0) == 0)
    def _():
        sp_ref[...] = jnp.zeros_like(sp_ref)

    sp_ref[...] += r[:, 2 * C_HALF:]


def _fold_channel_dft(w_u, cs):
    spec = pl.BlockSpec((D_MODEL, C_HALF), lambda g: (0, g))
    return pl.pallas_call(
        _fold_kernel,
        grid=(C_GROUPS,),
        in_specs=[pl.BlockSpec((D_MODEL, C_GDIM), lambda g: (0, g)),
                  pl.BlockSpec((1, C_GDIM, C_FOLD), lambda g: (g, 0, 0))],
        out_specs=[spec, spec, pl.BlockSpec((D_MODEL, LANES), lambda g: (0, 0))],
        out_shape=[jax.ShapeDtypeStruct((D_MODEL, C_MAIN), BF16)] * 2
        + [jax.ShapeDtypeStruct((D_MODEL, LANES), F32)],
        compiler_params=_params(("arbitrary",)),
        name="fold_channel_dft",
    )(w_u, cs)


DFT_ROWS = 512


def _dft_mix_kernel(cm_ref, sm_ref, a_ref, b_ref, gm_ref, gp_ref, sp_ref, gsp_ref,
                    om_ref, op_ref, osp_ref):
    n = cm_ref.shape[0]
    rc = min(n, DFT_ROWS)
    a = a_ref[0]
    b = b_ref[0]
    for r in range(n // rc):
        rows = slice(r * rc, (r + 1) * rc)
        ea = jnp.dot(cm_ref[rows, :], a, preferred_element_type=F32)
        eb = jnp.dot(sm_ref[rows, :], b, preferred_element_type=F32)
        om_ref[0, rows, :] = ((ea - eb) * _silu(gm_ref[0, rows, :].astype(F32))).astype(BF16)
        op_ref[0, rows, :] = ((ea + eb) * _silu(gp_ref[0, rows, :].astype(F32))).astype(BF16)

    @pl.when(pl.program_id(1) == 0)
    def _():
        es = jnp.dot(cm_ref[...], sp_ref[0], preferred_element_type=F32)
        osp_ref[0] = (es * _silu(gsp_ref[0].astype(F32))).astype(BF16)


def _trig(m, period, scale):
    ang = (2.0 * math.pi / period) * (m % period).astype(F32)
    return jnp.cos(ang) * scale, jnp.sin(ang) * scale


def _fourier_mix(proj3):
    bsz, seq, _ = proj3.shape
    idx = jnp.arange(seq, dtype=jnp.int32)
    cm, sm = _trig(idx[:, None] * idx[None, :], seq, seq ** -0.5)
    tc = 512
    nct = C_MAIN // tc
    mspec = pl.BlockSpec((seq, seq), lambda b, c: (0, 0))

    def dspec(width, idx):
        return pl.BlockSpec((1, seq, width), lambda b, c: (b, 0, idx(c)))

    sp_blk = 4 * C_MAIN // LANES
    return pl.pallas_call(
        _dft_mix_kernel,
        grid=(bsz, nct),
        in_specs=[mspec, mspec,
                  dspec(tc, lambda c: c), dspec(tc, lambda c: nct + c),
                  dspec(tc, lambda c: 2 * nct + c), dspec(tc, lambda c: 3 * nct + c),
                  dspec(LANES, lambda c: sp_blk), dspec(LANES, lambda c: sp_blk + 1)],
        out_specs=[dspec(tc, lambda c: c), dspec(tc, lambda c: c), dspec(LANES, lambda c: 0)],
        out_shape=[jax.ShapeDtypeStruct((bsz, seq, C_MAIN), BF16)] * 2
        + [jax.ShapeDtypeStruct((bsz, seq, LANES), BF16)],
        compiler_params=_params(("parallel", "arbitrary")),
        name="dft_mix",
    )(cm.astype(BF16), sm.astype(BF16), proj3, proj3, proj3, proj3, proj3, proj3)


C_GRP = 8
C_SUB = 64
C_ROWS = 256
C_NCHUNK = 1024


def _inproj_perm_kernel(x_ref, g_ref, w_ref, o_ref, xs_ref):
    rh = x_ref.shape[1]
    n = w_ref.shape[1]
    for i in range(C_GRP):
        xs_ref[i * rh:(i + 1) * rh, :] = x_ref[0, :, i, :]
    per_dot = C_ROWS // rh
    for j in range(C_GRP // per_dot):
        h = _rms(xs_ref[j * C_ROWS:(j + 1) * C_ROWS, :], g_ref[...]).astype(BF16)
        for c0 in range(0, n, C_NCHUNK):
            cols = slice(c0, min(c0 + C_NCHUNK, n))
            acc = jnp.dot(h, w_ref[:, cols], preferred_element_type=F32)
            for i in range(per_dot):
                o_ref[0, j * per_dot + i, :, cols] = acc[i * rh:(i + 1) * rh, :].astype(BF16)


def _inproj_perm(x4, g, w):
    bsz, r, _, _ = x4.shape
    n = w.shape[1]
    rh = min(r, C_SUB)
    return pl.pallas_call(
        _inproj_perm_kernel,
        grid=(bsz, r // C_GRP, r // rh),
        in_specs=[
            pl.BlockSpec((1, rh, C_GRP, D_MODEL), lambda b, m, s: (b, s, m, 0)),
            pl.BlockSpec((1, D_MODEL), lambda b, m, s: (0, 0)),
            pl.BlockSpec((D_MODEL, n), lambda b, m, s: (0, 0)),
        ],
        out_specs=pl.BlockSpec((1, C_GRP, rh, n), lambda b, m, s: (b, m, s, 0)),
        out_shape=jax.ShapeDtypeStruct((bsz, r, r, n), BF16),
        scratch_shapes=[pltpu.VMEM((C_GRP * rh, D_MODEL), F32)],
        compiler_params=_params(("parallel", "parallel", "parallel")),
        name="inproj_perm",
    )(x4, g, w)


def _dft_stage1_kernel(d_ref, x_ref, sp_ref, yr_ref, yi_ref, ysr_ref, ysi_ref):
    r = x_ref.shape[2]
    for i in range(C_GRP):
        dc = d_ref[i, :r, :]
        ds = d_ref[i, r:, :]
        yr_ref[0, :, i, :] = jnp.dot(dc, x_ref[0, i], preferred_element_type=F32)
        yi_ref[0, :, i, :] = jnp.dot(ds, x_ref[0, i], preferred_element_type=F32)
        ysr_ref[0, :, i, :] = jnp.dot(dc, sp_ref[0, i], preferred_element_type=F32)
        ysi_ref[0, :, i, :] = jnp.dot(ds, sp_ref[0, i], preferred_element_type=F32)


def _dft_stage1(proj4, dtab):
    bsz, r, _, _ = proj4.shape
    tc = 1024
    nct = 2 * C_MAIN // tc

    def yspec(width, idx):
        return pl.BlockSpec((1, r, C_GRP, width), lambda b, m, c: (b, 0, m, idx(c)))

    return pl.pallas_call(
        _dft_stage1_kernel,
        grid=(bsz, r // C_GRP, nct),
        in_specs=[
            pl.BlockSpec((C_GRP, 2 * r, r), lambda b, m, c: (m, 0, 0)),
            pl.BlockSpec((1, C_GRP, r, tc), lambda b, m, c: (b, m, 0, c)),
            pl.BlockSpec((1, C_GRP, r, LANES), lambda b, m, c: (b, m, 0, 4 * C_MAIN // LANES)),
        ],
        out_specs=[yspec(tc, lambda c: c), yspec(tc, lambda c: c),
                   yspec(LANES, lambda c: 0), yspec(LANES, lambda c: 0)],
        out_shape=[jax.ShapeDtypeStruct((bsz, r, r, 2 * C_MAIN), F32)] * 2
        + [jax.ShapeDtypeStruct((bsz, r, r, LANES), F32)] * 2,
        compiler_params=_params(("parallel", "parallel", "arbitrary")),
        name="dft_stage1",
    )(dtab, proj4, proj4)


def _dft_stage2_kernel(c1_ref, s1_ref, yra_ref, yia_ref, yrb_ref, yib_ref, gm_ref, gp_ref,
                       wm_ref, wp_ref, ysr_ref, ysi_ref, gsp_ref, wsp_ref, x_ref, o_ref, acc_ref):
    c = pl.program_id(2)
    r = c1_ref.shape[0]
    c1 = c1_ref[...]
    s1 = s1_ref[...]

    def dot(m, y):
        return jnp.dot(m, y.astype(BF16), preferred_element_type=F32)

    @pl.when(c == 0)
    def _():
        parts = []
        for i in range(C_GRP):
            es = dot(c1, ysr_ref[0, i]) - dot(s1, ysi_ref[0, i])
            parts.append((es * _silu(gsp_ref[0, i].astype(F32))).astype(BF16))
        nyq = jnp.dot(jnp.concatenate(parts, axis=0), wsp_ref[...], preferred_element_type=F32)
        for i in range(C_GRP):
            acc_ref[i * r:(i + 1) * r, :] = x_ref[0, :, i, :] + nyq[i * r:(i + 1) * r, :]

    minus, plus = [], []
    for i in range(C_GRP):
        ea = dot(c1, yra_ref[0, i]) - dot(s1, yia_ref[0, i])
        eb = dot(c1, yib_ref[0, i]) + dot(s1, yrb_ref[0, i])
        minus.append(((ea - eb) * _silu(gm_ref[0, i].astype(F32))).astype(BF16))
        plus.append(((ea + eb) * _silu(gp_ref[0, i].astype(F32))).astype(BF16))
    acc_ref[...] += (jnp.dot(jnp.concatenate(minus, axis=0), wm_ref[...], preferred_element_type=F32)
                     + jnp.dot(jnp.concatenate(plus, axis=0), wp_ref[...], preferred_element_type=F32))

    @pl.when(c == pl.num_programs(2) - 1)
    def _():
        for i in range(C_GRP):
            o_ref[0, :, i, :] = acc_ref[i * r:(i + 1) * r, :]


def _dft_stage2(c1, s1, y_r, y_i, y_sr, y_si, proj4, w_out, x4):
    bsz, r, _, _ = x4.shape
    tc = 512
    nct = C_MAIN // tc
    mspec = pl.BlockSpec((r, r), lambda b, m, c: (0, 0))

    def slab(width, idx):
        return pl.BlockSpec((1, C_GRP, r, width), lambda b, m, c: (b, m, 0, idx(c)))

    def wspec(rows, idx):
        return pl.BlockSpec((rows, D_MODEL), lambda b, m, c: (idx(c), 0))

    xspec = pl.BlockSpec((1, r, C_GRP, D_MODEL), lambda b, m, c: (b, 0, m, 0))
    sp_blk = 4 * C_MAIN // LANES
    return pl.pallas_call(
        _dft_stage2_kernel,
        grid=(bsz, r // C_GRP, nct),
        in_specs=[
            mspec, mspec,
            slab(tc, lambda c: c), slab(tc, lambda c: c),
            slab(tc, lambda c: nct + c), slab(tc, lambda c: nct + c),
            slab(tc, lambda c: 2 * nct + c), slab(tc, lambda c: 3 * nct + c),
            wspec(tc, lambda c: c), wspec(tc, lambda c: nct + c),
            slab(LANES, lambda c: 0), slab(LANES, lambda c: 0),
            slab(LANES, lambda c: sp_blk + 1),
            wspec(LANES, lambda c: 2 * C_MAIN // LANES),
            xspec,
        ],
        out_specs=xspec,
        out_shape=jax.ShapeDtypeStruct(x4.shape, F32),
        scratch_shapes=[pltpu.VMEM((C_GRP * r, D_MODEL), F32)],
        compiler_params=_params(("parallel", "parallel", "arbitrary")),
        name="dft_stage2",
    )(c1, s1, y_r, y_i, y_r, y_i, proj4, proj4, w_out, w_out, y_sr, y_si, proj4, w_out, x4)


def _fourier_long(x2d, g, wc, w_out, bsz, seq):
    r = math.isqrt(seq)
    assert r * r == seq and r % C_GRP == 0
    x4 = x2d.reshape(bsz, r, r, D_MODEL)
    proj4 = _inproj_perm(x4, g, wc)
    idx = jnp.arange(r, dtype=jnp.int32)
    tok = idx[:, None, None] + r * idx[None, None, :]
    dc, ds = _trig(idx[None, :, None] * tok, seq, r ** -0.5)
    dtab = jnp.concatenate([dc, ds], axis=1).astype(BF16)
    y_r, y_i, y_sr, y_si = _dft_stage1(proj4, dtab)
    c1, s1 = _trig(idx[:, None] * idx[None, :], r, r ** -0.5)
    out = _dft_stage2(c1.astype(BF16), s1.astype(BF16), y_r, y_i, y_sr, y_si, proj4, w_out, x4)
    return out.reshape(bsz * seq, D_MODEL)


OUT_TM = 512


def _outproj_kernel(mm_ref, mp_ref, ms_ref, x_ref, w_ref, o_ref):
    acc = jnp.dot(mm_ref[...], w_ref[:C_MAIN, :], preferred_element_type=F32)
    acc = acc + jnp.dot(mp_ref[...], w_ref[C_MAIN:2 * C_MAIN, :], preferred_element_type=F32)
    acc = acc + jnp.dot(ms_ref[...], w_ref[2 * C_MAIN:, :], preferred_element_type=F32)
    o_ref[...] = x_ref[...] + acc


def _outproj(minus2d, plus2d, nyq2d, x2d, w_out):
    t = x2d.shape[0]
    return pl.pallas_call(
        _outproj_kernel,
        grid=(t // OUT_TM,),
        in_specs=[
            pl.BlockSpec((OUT_TM, C_MAIN), lambda i: (i, 0)),
            pl.BlockSpec((OUT_TM, C_MAIN), lambda i: (i, 0)),
            pl.BlockSpec((OUT_TM, LANES), lambda i: (i, 0)),
            pl.BlockSpec((OUT_TM, D_MODEL), lambda i: (i, 0)),
            pl.BlockSpec((2 * C_MAIN + LANES, D_MODEL), lambda i: (0, 0)),
        ],
        out_specs=pl.BlockSpec((OUT_TM, D_MODEL), lambda i: (i, 0)),
        out_shape=jax.ShapeDtypeStruct((t, D_MODEL), F32),
        compiler_params=_params(("parallel",)),
        name="outproj",
    )(minus2d, plus2d, nyq2d, x2d, w_out)


def _rope_tables_a(seq):
    half = A_ROT // 2
    inv_freq = jnp.exp(-(jnp.arange(half, dtype=F32) * (2.0 / A_ROT)) * math.log(A_THETA))
    ang = jnp.arange(seq, dtype=F32)[:, None] * inv_freq[None, :]
    cos, sin = jnp.cos(ang), jnp.sin(ang)
    d = jnp.arange(LANES) % A_HEAD
    f = d % half
    cm = jnp.where(d[None, :] < A_ROT, cos[:, f], 1.0)
    s1 = jnp.where(d[None, :] < half, -sin[:, f], 0.0)
    s2 = jnp.where((d[None, :] >= half) & (d[None, :] < A_ROT), sin[:, f], 0.0)
    qs = A_HEAD ** -0.5 * LOG2E
    return jnp.stack([cm * qs, s1 * qs, s2 * qs, cm, s1, s2]).astype(F32)


def _rope_tables_b(seq):
    half = B_QK // 2
    inv_freq = jnp.exp(-(jnp.arange(half, dtype=F32) * (2.0 / B_QK)) * math.log(B_THETA))
    ang = jnp.arange(seq, dtype=F32)[:, None] * inv_freq[None, :]
    cos, sin = jnp.cos(ang), jnp.sin(ang)
    ks = B_QK ** -0.5
    return jnp.stack([cos, sin, cos * ks, sin * ks]).astype(F32)


def _weights_a(w_in):
    nq = A_QH * A_HEAD
    nkv = A_KVH * A_HEAD
    q = w_in[:, :nq]
    kv = w_in[:, nq:nq + 2 * nkv]
    gate = w_in[:, nq + 2 * nkv:]
    return jnp.concatenate([q, gate, kv], axis=1).astype(BF16)


def _c_index():
    g = jnp.arange(C_GROUPS, dtype=jnp.int32)[:, None] * C_GDIM
    m = jnp.arange(C_HALF, dtype=jnp.int32)[None, :]
    minus = (g + m).reshape(-1)
    plus = (g + jnp.where(m == 0, 0, C_GDIM - m)).reshape(-1)
    nyq = (g + C_HALF).reshape(-1)
    return minus, plus, nyq, jnp.tile(m == 0, (C_GROUPS, 1)).reshape(-1)


def _weights_c(w_in):
    ch = jnp.arange(C_GDIM, dtype=jnp.int32)
    m = jnp.arange(C_HALF, dtype=jnp.int32)
    cm, sm = _trig(ch[:, None] * m[None, :], C_GDIM, C_GDIM ** -0.5)
    nyq, _ = _trig(ch * C_HALF, C_GDIM, C_GDIM ** -0.5)
    lane = jnp.arange(LANES)[None, None, :] == jnp.arange(C_GROUPS)[:, None, None]
    nyq = jnp.where(lane, nyq[None, :, None], 0.0)
    cs = jnp.concatenate([jnp.broadcast_to(cm, (C_GROUPS,) + cm.shape),
                          jnp.broadcast_to(sm, (C_GROUPS,) + sm.shape), nyq], axis=2)
    w_a, w_b, w_sp = _fold_channel_dft(w_in[:, :BRANCH], cs)
    minus, plus, nyq_idx, _ = _c_index()
    gate = w_in[:, BRANCH:]
    g_sp = jnp.zeros((D_MODEL, LANES), F32).at[:, :C_GROUPS].set(gate[:, nyq_idx])
    return jnp.concatenate([w_a, w_b, gate[:, minus].astype(BF16), gate[:, plus].astype(BF16),
                            w_sp.astype(BF16), g_sp.astype(BF16)], axis=1)


def _weights_c_out(w_out):
    minus, plus, nyq_idx, dup = _c_index()
    w_plus = jnp.where(dup[:, None], 0.0, w_out[plus])
    w_sp = jnp.zeros((LANES, D_MODEL), F32).at[:C_GROUPS].set(w_out[nyq_idx])
    return jnp.concatenate([w_out[minus], w_plus, w_sp], axis=0).astype(BF16)


def _trunk(x, norm_g, fin_g, wa, a_sink, a_w_out, wb, log_g, b_w_out, wc, c_w_out):
    bsz, seq, _ = x.shape
    t = bsz * seq
    x2d = x.reshape(t, D_MODEL)
    tabs_a = _rope_tables_a(seq)
    tabs_b = _rope_tables_b(seq)
    fin = fin_g.reshape(1, D_MODEL)

    def layer_a(x2d, layer, j, final):
        q3, gate3, kk, vt = _inproj_a(x2d, norm_g[layer].reshape(1, D_MODEL), wa[j], tabs_a, bsz, seq)
        return _attention(x2d, q3, gate3, kk, vt, a_sink[j], a_w_out[j], fin, bsz, seq, final)

    x2d = layer_a(x2d, 0, 0, False)

    proj = _inproj(x2d, norm_g[1].reshape(1, D_MODEL), wb, tabs_b, seq, 2 * B_HEADS)
    x2d = _retention(proj.reshape(bsz, seq, -1), log_g, x2d.reshape(bsz, seq, D_MODEL),
                     b_w_out).reshape(t, D_MODEL)

    if seq <= C_DIRECT_MAX:
        proj = _inproj(x2d, norm_g[2].reshape(1, D_MODEL), wc, None, seq, 0)
        minus, plus, nyq = _fourier_mix(proj.reshape(bsz, seq, -1))
        x2d = _outproj(minus.reshape(t, C_MAIN), plus.reshape(t, C_MAIN), nyq.reshape(t, LANES),
                       x2d, c_w_out)
    else:
        x2d = _fourier_long(x2d, norm_g[2].reshape(1, D_MODEL), wc, c_w_out, bsz, seq)

    x2d = layer_a(x2d, 3, 1, True)
    return x2d.reshape(bsz, seq, D_MODEL)


def kernel(x_prompt, x_sample, norm_g, final_norm_g, a_w_in, a_sink, a_w_out, b_w_in, b_decay,
           b_w_out, c_w_in, c_w_out):
    wa = [_weights_a(a_w_in[j]) for j in range(a_w_in.shape[0])]
    a_out = [a_w_out[j].astype(BF16) for j in range(a_w_out.shape[0])]
    wb = b_w_in[0].astype(BF16)
    log_g = jax.nn.log_sigmoid(b_decay[0].astype(F32))
    wc = _weights_c(c_w_in[0])
    args = (norm_g, final_norm_g, wa, a_sink, a_out, wb, log_g, b_w_out[0].astype(BF16),
            wc, _weights_c_out(c_w_out[0]))
    return (_trunk(x_prompt, *args), _trunk(x_sample, *args))
```

```python
import functools
import math

import jax
import jax.numpy as jnp
from jax import lax
from jax.experimental import pallas as pl
from jax.experimental.pallas import tpu as pltpu

D_MODEL = 1024
BRANCH = 2048
NORM_EPS = 1e-6

A_HEAD = 64
A_QH = 32
A_KVH = 4
A_ROT = 16
A_THETA = 500000.0
A_WIN = 128
A_QBLK = 512
A_SPAN = A_QBLK + 2 * A_WIN
A_PAIRS = A_QH // 2
A_LOOKAHEAD = 8
A_OUT_GROUPS = 2
A_VPAD = 16
LOG2E = math.log2(math.e)

B_QK = 256
B_HEADS = 4
B_V = 512
B_CHUNK = 256
B_ROWS = 2
B_THETA = 10000.0

C_GROUPS = 4
C_GDIM = 512
C_DIRECT_MAX = 2048

LANES = 128
VMEM_LIMIT = 56 * 1024 * 1024

BF16 = jnp.bfloat16
F32 = jnp.float32


def _params(sem):
    return pltpu.CompilerParams(dimension_semantics=sem, vmem_limit_bytes=VMEM_LIMIT)


def _silu(x):
    return x * jax.nn.sigmoid(x)


def _rms(x, g):
    ms = jnp.mean(x * x, axis=-1, keepdims=True)
    return (x * lax.rsqrt(ms + NORM_EPS)) * g


IN_TM = 1024
IN_TN = 512


def _inproj_a_kernel(x_ref, g_ref, w_ref, tab_ref, q_ref, gate_ref, kk_ref, vt_ref):
    h = _rms(x_ref[...], g_ref[...]).astype(BF16)

    def rope(a, t0):
        return (a * tab_ref[t0] + pltpu.roll(a, LANES - A_ROT // 2, 1) * tab_ref[t0 + 1]
                + pltpu.roll(a, A_ROT // 2, 1) * tab_ref[t0 + 2])

    for c in range(4):
        acc = jnp.dot(h, w_ref[:, c * IN_TN:(c + 1) * IN_TN], preferred_element_type=F32)
        for s in range(4):
            q_ref[4 * c + s] = rope(acc[:, s * LANES:(s + 1) * LANES], 0).astype(BF16)
    for c in range(4):
        acc = jnp.dot(h, w_ref[:, BRANCH + c * IN_TN:BRANCH + (c + 1) * IN_TN],
                      preferred_element_type=F32)
        for s in range(4):
            gate_ref[4 * c + s] = acc[:, s * LANES:(s + 1) * LANES].astype(BF16)
    acc = jnp.dot(h, w_ref[:, 2 * BRANCH:], preferred_element_type=F32)
    nkv = A_KVH * A_HEAD
    lane = lax.broadcasted_iota(jnp.int32, (IN_TM, LANES), 1)
    for p in range(nkv // LANES):
        a = rope(acc[:, p * LANES:(p + 1) * LANES], 3)
        swapped = pltpu.roll(a, A_HEAD, 1)
        kk_ref[2 * p] = jnp.where(lane < A_HEAD, a, swapped).astype(BF16)
        kk_ref[2 * p + 1] = jnp.where(lane >= A_HEAD, a, swapped).astype(BF16)
    vt_ref[0] = acc[:, nkv:].T.astype(BF16)


def _inproj_a(x2d, g, w, tabs, bsz, seq):
    t = x2d.shape[0]
    nblk = seq // IN_TM
    n = w.shape[1]
    return pl.pallas_call(
        _inproj_a_kernel,
        grid=(t // IN_TM,),
        in_specs=[
            pl.BlockSpec((IN_TM, D_MODEL), lambda i: (i, 0)),
            pl.BlockSpec((1, D_MODEL), lambda i: (0, 0)),
            pl.BlockSpec((D_MODEL, n), lambda i: (0, 0)),
            pl.BlockSpec((6, IN_TM, LANES), lambda i: (0, i % nblk, 0)),
        ],
        out_specs=[
            pl.BlockSpec((A_PAIRS, IN_TM, LANES), lambda i: (0, i, 0)),
            pl.BlockSpec((A_PAIRS, IN_TM, LANES), lambda i: (0, i, 0)),
            pl.BlockSpec((A_KVH, IN_TM, LANES), lambda i: (0, i, 0)),
            pl.BlockSpec((1, A_KVH * A_HEAD, IN_TM), lambda i: (i // nblk, 0, i % nblk)),
        ],
        out_shape=[
            jax.ShapeDtypeStruct((A_PAIRS, t, LANES), BF16),
            jax.ShapeDtypeStruct((A_PAIRS, t, LANES), BF16),
            jax.ShapeDtypeStruct((A_KVH, t, LANES), BF16),
            jax.ShapeDtypeStruct((bsz, A_KVH * A_HEAD, seq), BF16),
        ],
        compiler_params=_params(("parallel",)),
        name="inproj_a",
    )(x2d, g, w, tabs)


def _inproj_kernel(x_ref, g_ref, w_ref, *rest, rope_heads):
    if rope_heads:
        tab_ref, o_ref = rest
    else:
        (o_ref,) = rest
    h = _rms(x_ref[...], g_ref[...]).astype(BF16)
    n = w_ref.shape[1]
    for c0 in range(0, n, IN_TN):
        width = min(IN_TN, n - c0)
        acc = jnp.dot(h, w_ref[:, c0:c0 + width], preferred_element_type=F32)
        for lo in range(0, width, B_QK):
            head = (c0 + lo) // B_QK
            if head < rope_heads:
                t0 = 0 if head < rope_heads // 2 else 2
                cos, sin = tab_ref[t0], tab_ref[t0 + 1]
                x1 = acc[:, lo:lo + LANES]
                x2 = acc[:, lo + LANES:lo + B_QK]
                o_ref[:, c0 + lo:c0 + lo + LANES] = (x1 * cos - x2 * sin).astype(BF16)
                o_ref[:, c0 + lo + LANES:c0 + lo + B_QK] = (x2 * cos + x1 * sin).astype(BF16)
            else:
                o_ref[:, c0 + lo:c0 + lo + B_QK] = acc[:, lo:lo + B_QK].astype(BF16)


def _inproj(x2d, g, w, tabs, seq, rope_heads):
    t = x2d.shape[0]
    n = w.shape[1]
    nblk = seq // IN_TM
    in_specs = [
        pl.BlockSpec((IN_TM, D_MODEL), lambda i: (i, 0)),
        pl.BlockSpec((1, D_MODEL), lambda i: (0, 0)),
        pl.BlockSpec((D_MODEL, n), lambda i: (0, 0), pipeline_mode=pl.Buffered(1)),
    ]
    args = [x2d, g, w]
    if rope_heads:
        in_specs.append(pl.BlockSpec((4, IN_TM, LANES), lambda i: (0, i % nblk, 0)))
        args.append(tabs)
    return pl.pallas_call(
        functools.partial(_inproj_kernel, rope_heads=rope_heads),
        grid=(t // IN_TM,),
        in_specs=in_specs,
        out_specs=pl.BlockSpec((IN_TM, n), lambda i: (i, 0)),
        out_shape=jax.ShapeDtypeStruct((t, n), BF16),
        compiler_params=_params(("parallel",)),
        name="inproj",
    )(*args)


def _attn_kernel(sink_ref, q_ref, gate_ref, kp_ref, kc_ref, kn_ref, vp_ref, vc_ref, vn_ref,
                 x_ref, w_ref, fg_ref, o_ref, k_ref, v_ref, bias_ref, *, final):
    i = pl.program_id(1)
    last = pl.num_programs(1) - 1
    rows = 3 * A_WIN
    nslice = A_QBLK // A_WIN
    per_kv = A_PAIRS // A_KVH

    ones_row = (lax.broadcasted_iota(jnp.int32, (A_VPAD, A_SPAN), 0) == 0).astype(BF16)
    for h in range(A_KVH):
        k_ref[h] = jnp.concatenate([kp_ref[h], kc_ref[h], kn_ref[h]], axis=0)
        v_ref[h, :A_HEAD, :] = jnp.concatenate([vp_ref[0, h], vc_ref[0, h], vn_ref[0, h]], axis=1)
        v_ref[h, A_HEAD:, :] = ones_row

    r = lax.broadcasted_iota(jnp.int32, (A_WIN, LANES), 0)
    c = lax.broadcasted_iota(jnp.int32, (A_WIN, LANES), 1)
    neg = jnp.full((A_WIN, LANES), -1e30, F32)
    top = jnp.where(r >= c, 0.0, neg)
    bot = jnp.where(r <= c, 0.0, neg)
    bias_ref[0] = jnp.where(i == 0, neg, top)
    bias_ref[1] = top
    bias_ref[2] = bot
    bias_ref[3] = jnp.where(i == last, neg, bot)
    qlane = lax.broadcasted_iota(jnp.int32, (LANES, LANES), 1)

    units = [(g, jl, jj) for g in range(A_KVH) for jl in range(per_kv) for jj in range(nslice)]

    def scores(n):
        g, jl, jj = units[n]
        q = q_ref[g * per_kv + jl, jj * LANES:(jj + 1) * LANES, :]
        zero = jnp.zeros_like(q)
        rhs = jnp.concatenate([jnp.where(qlane < A_HEAD, q, zero),
                               jnp.where(qlane >= A_HEAD, q, zero)], axis=0)
        return lax.dot_general(k_ref[g, jj * A_WIN:jj * A_WIN + rows, :], rhs,
                               (((1,), (1,)), ((), ())), preferred_element_type=F32)

    def softmax_pv(s_t, sinks, g, jj):
        ps, extra = [], []
        for a in range(2):
            blk = s_t[:, a * LANES:(a + 1) * LANES]
            parts = [blk[:A_WIN] + bias_ref[0 if jj == 0 else 1], blk[A_WIN:2 * A_WIN],
                     blk[2 * A_WIN:] + bias_ref[3 if jj == nslice - 1 else 2]]
            mx = jnp.full((1, LANES), sinks[a], F32)
            for part in parts:
                mx = jnp.maximum(mx, jnp.max(part, axis=0, keepdims=True))
            ps.append(jnp.concatenate([jnp.exp2(part - mx).astype(BF16) for part in parts], axis=0))
            extra.append(jnp.exp2(sinks[a] - mx))
        o_ext = jnp.dot(v_ref[g, :, jj * A_WIN:jj * A_WIN + rows], jnp.concatenate(ps, axis=1),
                        preferred_element_type=F32)
        den = o_ext[A_HEAD:A_HEAD + 1, :] + jnp.concatenate(extra, axis=1)
        return o_ext[:A_HEAD, :] / den

    pending = [scores(n) for n in range(A_LOOKAHEAD)]
    outs, gated = [], []
    y = x_ref[...]
    for n, (g, jl, jj) in enumerate(units):
        if n + A_LOOKAHEAD < len(units):
            pending.append(scores(n + A_LOOKAHEAD))
        j = g * per_kv + jl
        sinks = (sink_ref[2 * j] * LOG2E, sink_ref[2 * j + 1] * LOG2E)
        outs.append(softmax_pv(pending.pop(0), sinks, g, jj))
        if jj == nslice - 1:
            last_outs = outs[-nslice:]
            o_pair = jnp.concatenate(
                [jnp.concatenate([o[:, :LANES] for o in last_outs], axis=1),
                 jnp.concatenate([o[:, LANES:] for o in last_outs], axis=1)], axis=0).T
            gt = gate_ref[j].astype(F32)
            gated.append((o_pair * _silu(gt)).astype(BF16))
            if jl == per_kv - 1 and (g + 1) % A_OUT_GROUPS == 0:
                npair = per_kv * A_OUT_GROUPS
                wg = w_ref[(j + 1 - npair) * LANES:(j + 1) * LANES, :]
                y = y + jnp.dot(jnp.concatenate(gated[-npair:], axis=1), wg,
                                preferred_element_type=F32)
    if final:
        y = _rms(y, fg_ref[...])
    o_ref[...] = y


def _attention(x2d, q3, gate3, kk, vt, sink, w_out, fin_g, bsz, seq, final):
    t = x2d.shape[0]
    nqb = seq // A_QBLK
    nkb = seq // A_WIN
    nsl = A_QBLK // A_WIN
    vt4 = vt.reshape(bsz, A_KVH, A_HEAD, seq)
    grid_spec = pltpu.PrefetchScalarGridSpec(
        num_scalar_prefetch=1,
        grid=(bsz, nqb),
        in_specs=[
            pl.BlockSpec((A_PAIRS, A_QBLK, LANES), lambda b, i, s: (0, b * nqb + i, 0)),
            pl.BlockSpec((A_PAIRS, A_QBLK, LANES), lambda b, i, s: (0, b * nqb + i, 0)),
            pl.BlockSpec((A_KVH, A_WIN, LANES),
                         lambda b, i, s: (0, b * nkb + jnp.maximum(nsl * i - 1, 0), 0)),
            pl.BlockSpec((A_KVH, A_QBLK, LANES), lambda b, i, s: (0, b * nqb + i, 0)),
            pl.BlockSpec((A_KVH, A_WIN, LANES),
                         lambda b, i, s: (0, b * nkb + jnp.minimum(nsl * i + nsl, nkb - 1), 0)),
            pl.BlockSpec((1, A_KVH, A_HEAD, A_WIN),
                         lambda b, i, s: (b, 0, 0, jnp.maximum(nsl * i - 1, 0))),
            pl.BlockSpec((1, A_KVH, A_HEAD, A_QBLK), lambda b, i, s: (b, 0, 0, i)),
            pl.BlockSpec((1, A_KVH, A_HEAD, A_WIN),
                         lambda b, i, s: (b, 0, 0, jnp.minimum(nsl * i + nsl, nkb - 1))),
            pl.BlockSpec((A_QBLK, D_MODEL), lambda b, i, s: (b * nqb + i, 0)),
            pl.BlockSpec((BRANCH, D_MODEL), lambda b, i, s: (0, 0)),
            pl.BlockSpec((1, D_MODEL), lambda b, i, s: (0, 0)),
        ],
        out_specs=pl.BlockSpec((A_QBLK, D_MODEL), lambda b, i, s: (b * nqb + i, 0)),
        scratch_shapes=[
            pltpu.VMEM((A_KVH, A_SPAN, LANES), BF16),
            pltpu.VMEM((A_KVH, A_HEAD + A_VPAD, A_SPAN), BF16),
            pltpu.VMEM((4, A_WIN, LANES), F32),
        ],
    )
    return pl.pallas_call(
        functools.partial(_attn_kernel, final=final),
        grid_spec=grid_spec,
        out_shape=jax.ShapeDtypeStruct((t, D_MODEL), F32),
        compiler_params=_params(("parallel", "parallel")),
        name="attention",
    )(sink, q3, gate3, kk, kk, kk, vt4, vt4, vt4, x2d, w_out, fin_g)


def _retention_start(lg_ref, st_ref, intra_ref, dec_ref, backward):
    n = B_CHUNK
    ii = lax.broadcasted_iota(jnp.int32, (n, n), 0)
    jj = lax.broadcasted_iota(jnp.int32, (n, n), 1)
    col = lax.broadcasted_iota(jnp.int32, (n, LANES), 0).astype(F32)
    st_ref[...] = jnp.zeros_like(st_ref)
    for h in range(B_HEADS):
        lg = lg_ref[1 if backward else 0, h]
        if backward:
            mask = jj > ii
            dist = (jj - ii).astype(F32)
            dec_ref[h, 0] = jnp.exp((n - col) * lg)
            dec_ref[h, 1] = jnp.exp(col * lg)
        else:
            mask = ii >= jj
            dist = (ii - jj).astype(F32)
            dec_ref[h, 0] = jnp.exp((col + 1.0) * lg)
            dec_ref[h, 1] = jnp.exp((n - 1.0 - col) * lg)
        intra_ref[h] = jnp.where(mask, jnp.exp(jnp.where(mask, dist, 0.0) * lg), 0.0)


def _retention_open(q_ref, k_ref, v_ref, row):
    heads = range(B_HEADS)
    qs = [q_ref[row, :, h * B_QK:(h + 1) * B_QK] for h in heads]
    ks = [k_ref[row, :, h * B_QK:(h + 1) * B_QK] for h in heads]
    vs = [v_ref[row, :, h * B_V:(h + 1) * B_V] for h in heads]
    sc = [lax.dot_general(qs[h], ks[h], (((1,), (1,)), ((), ())), preferred_element_type=F32)
          for h in heads]
    return qs, ks, vs, sc


def _retention_heads(lg_ref, opened, st_ref, intra_ref, dec_ref, row, backward, emit):
    qs, ks, vs, sc = opened
    for h in range(B_HEADS):
        q_dec = jnp.concatenate([dec_ref[h, 0]] * (B_QK // LANES), axis=1)
        k_dec = jnp.concatenate([dec_ref[h, 1]] * (B_QK // LANES), axis=1)
        c_dec = jnp.exp(jnp.full((1, B_V), B_CHUNK * lg_ref[1 if backward else 0, h], F32))
        st = st_ref[row, h]
        lhs = jnp.concatenate([(sc[h] * intra_ref[h]).astype(BF16),
                               (qs[h].astype(F32) * q_dec).astype(BF16)], axis=1)
        rhs = jnp.concatenate([vs[h], st.astype(BF16)], axis=0)
        out = jnp.dot(lhs, rhs, preferred_element_type=F32)
        kd_t = (ks[h].astype(F32) * k_dec).T.astype(BF16)
        st_ref[row, h] = st * c_dec + jnp.dot(kd_t, vs[h], preferred_element_type=F32)
        emit(h, out)


def _retention_bwd_kernel(lg_ref, q_ref, k_ref, v_ref, o_ref, st_ref, intra_ref, dec_ref):
    @pl.when(pl.program_id(1) == 0)
    def _():
        _retention_start(lg_ref, st_ref, intra_ref, dec_ref, True)

    rows = range(q_ref.shape[0])
    opened = [_retention_open(q_ref, k_ref, v_ref, row) for row in rows]
    for row in rows:
        def emit(h, o, row=row):
            o_ref[row, :, h * B_V:(h + 1) * B_V] = o.astype(BF16)

        _retention_heads(lg_ref, opened[row], st_ref, intra_ref, dec_ref, row, True, emit)


def _retention_fwd_kernel(lg_ref, q_ref, k_ref, v_ref, gate_ref, ob_ref, x_ref, w_ref, o_ref,
                          st_ref, intra_ref, dec_ref):
    @pl.when(pl.program_id(1) == 0)
    def _():
        _retention_start(lg_ref, st_ref, intra_ref, dec_ref, False)

    rows = range(q_ref.shape[0])
    opened = [_retention_open(q_ref, k_ref, v_ref, row) for row in rows]
    gated = []
    for row in rows:
        parts = []

        def emit(h, o_f, row=row, parts=parts):
            sl = slice(h * B_V, (h + 1) * B_V)
            o = o_f + ob_ref[row, :, sl].astype(F32)
            on = o * lax.rsqrt(jnp.mean(o * o, axis=-1, keepdims=True) + NORM_EPS)
            parts.append((on * _silu(gate_ref[row, :, sl].astype(F32))).astype(BF16))

        _retention_heads(lg_ref, opened[row], st_ref, intra_ref, dec_ref, row, False, emit)
        gated.append(jnp.concatenate(parts, axis=1))
    for row in rows:
        o_ref[row] = x_ref[row] + jnp.dot(gated[row], w_ref[...], preferred_element_type=F32)


def _retention(proj3, log_g, x3, w_out):
    bsz, seq, _ = proj3.shape
    nc = seq // B_CHUNK
    nqk = B_HEADS * B_QK
    rows = B_ROWS if bsz % B_ROWS == 0 else 1
    smem = pl.BlockSpec(memory_space=pltpu.SMEM)
    state = [pltpu.VMEM((rows, B_HEADS, B_QK, B_V), F32),
             pltpu.VMEM((B_HEADS, B_CHUNK, B_CHUNK), F32),
             pltpu.VMEM((B_HEADS, 2, B_CHUNK, LANES), F32)]

    def blk(width, chunk, col):
        return pl.BlockSpec((rows, B_CHUNK, width), lambda b, t: (b, chunk(t), col))

    def rev(t):
        return nc - 1 - t

    def fwd(t):
        return t

    o_b = pl.pallas_call(
        _retention_bwd_kernel,
        grid=(bsz // rows, nc),
        in_specs=[smem, blk(nqk, rev, 0), blk(nqk, rev, 1), blk(BRANCH, rev, 1)],
        out_specs=blk(BRANCH, rev, 0),
        out_shape=jax.ShapeDtypeStruct((bsz, seq, BRANCH), BF16),
        scratch_shapes=state,
        compiler_params=_params(("parallel", "arbitrary")),
        name="retention_bwd",
    )(log_g, proj3, proj3, proj3)
    return pl.pallas_call(
        _retention_fwd_kernel,
        grid=(bsz // rows, nc),
        in_specs=[smem, blk(nqk, fwd, 0), blk(nqk, fwd, 1), blk(BRANCH, fwd, 1),
                  blk(BRANCH, fwd, 2), blk(BRANCH, fwd, 0), blk(D_MODEL, fwd, 0),
                  pl.BlockSpec((BRANCH, D_MODEL), lambda b, t: (0, 0))],
        out_specs=blk(D_MODEL, fwd, 0),
        out_shape=jax.ShapeDtypeStruct((bsz, seq, D_MODEL), F32),
        scratch_shapes=state,
        compiler_params=_params(("parallel", "arbitrary")),
        name="retention_fwd",
    )(log_g, proj3, proj3, proj3, proj3, o_b, x3, w_out)


C_HALF = C_GDIM // 2
C_MAIN = C_GROUPS * C_HALF
C_FOLD = 2 * C_HALF + LANES
C_NYQ_ROWS = 16


def _fold_kernel(w_ref, cs_ref, a_ref, b_ref, sp_ref):
    r = jnp.dot(w_ref[...], cs_ref[0], preferred_element_type=F32,
                precision=lax.Precision.HIGHEST)
    a_ref[...] = r[:, :C_HALF].astype(BF16)
    b_ref[...] = r[:, C_HALF:2 * C_HALF].astype(BF16)

    @pl.when(pl.program_id(0) == 0)
    def _():
        sp_ref[...] = jnp.zeros_like(sp_ref)

    sp_ref[...] += r[:, 2 * C_HALF:]


def _fold_channel_dft(w_u, cs):
    spec = pl.BlockSpec((D_MODEL, C_HALF), lambda g: (0, g))
    return pl.pallas_call(
        _fold_kernel,
        grid=(C_GROUPS,),
        in_specs=[pl.BlockSpec((D_MODEL, C_GDIM), lambda g: (0, g)),
                  pl.BlockSpec((1, C_GDIM, C_FOLD), lambda g: (g, 0, 0))],
        out_specs=[spec, spec, pl.BlockSpec((D_MODEL, LANES), lambda g: (0, 0))],
        out_shape=[jax.ShapeDtypeStruct((D_MODEL, C_MAIN), BF16)] * 2
        + [jax.ShapeDtypeStruct((D_MODEL, LANES), F32)],
        compiler_params=_params(("arbitrary",)),
        name="fold_channel_dft",
    )(w_u, cs)


DFT_ROWS = 512


def _dft_mix_kernel(cm_ref, sm_ref, a_ref, b_ref, gm_ref, gp_ref, sp_ref, gsp_ref,
                    om_ref, op_ref, osp_ref):
    n = cm_ref.shape[0]
    rc = min(n, DFT_ROWS)
    a = a_ref[0]
    b = b_ref[0]
    for r in range(n // rc):
        rows = slice(r * rc, (r + 1) * rc)
        ea = jnp.dot(cm_ref[rows, :], a, preferred_element_type=F32)
        eb = jnp.dot(sm_ref[rows, :], b, preferred_element_type=F32)
        om_ref[0, rows, :] = ((ea - eb) * _silu(gm_ref[0, rows, :].astype(F32))).astype(BF16)
        op_ref[0, rows, :] = ((ea + eb) * _silu(gp_ref[0, rows, :].astype(F32))).astype(BF16)

    @pl.when(pl.program_id(1) == 0)
    def _():
        sp_t = sp_ref[0].astype(F32).T[:C_NYQ_ROWS, :].astype(BF16)
        es_t = jnp.dot(sp_t, cm_ref[...], preferred_element_type=F32)
        es = jnp.concatenate([es_t, jnp.zeros((LANES - C_NYQ_ROWS, n), F32)], axis=0).T
        osp_ref[0] = (es * _silu(gsp_ref[0].astype(F32))).astype(BF16)


def _trig(m, period, scale):
    ang = (2.0 * math.pi / period) * (m % period).astype(F32)
    return jnp.cos(ang) * scale, jnp.sin(ang) * scale


def _fourier_mix(proj3):
    bsz, seq, _ = proj3.shape
    idx = jnp.arange(seq, dtype=jnp.int32)
    cm, sm = _trig(idx[:, None] * idx[None, :], seq, seq ** -0.5)
    tc = 512
    nct = C_MAIN // tc
    mspec = pl.BlockSpec((seq, seq), lambda b, c: (0, 0))

    def dspec(width, idx):
        return pl.BlockSpec((1, seq, width), lambda b, c: (b, 0, idx(c)))

    sp_blk = 4 * C_MAIN // LANES
    return pl.pallas_call(
        _dft_mix_kernel,
        grid=(bsz, nct),
        in_specs=[mspec, mspec,
                  dspec(tc, lambda c: c), dspec(tc, lambda c: nct + c),
                  dspec(tc, lambda c: 2 * nct + c), dspec(tc, lambda c: 3 * nct + c),
                  dspec(LANES, lambda c: sp_blk), dspec(LANES, lambda c: sp_blk + 1)],
        out_specs=[dspec(tc, lambda c: c), dspec(tc, lambda c: c), dspec(LANES, lambda c: 0)],
        out_shape=[jax.ShapeDtypeStruct((bsz, seq, C_MAIN), BF16)] * 2
        + [jax.ShapeDtypeStruct((bsz, seq, LANES), BF16)],
        compiler_params=_params(("parallel", "arbitrary")),
        name="dft_mix",
    )(cm.astype(BF16), sm.astype(BF16), proj3, proj3, proj3, proj3, proj3, proj3)


C_GRP = 8
C_SUB = 64
C_ROWS = 256
C_NCHUNK = 1024


def _inproj_perm_kernel(x_ref, g_ref, w_ref, o_ref, xs_ref):
    rh = x_ref.shape[1]
    n = w_ref.shape[1]
    for i in range(C_GRP):
        xs_ref[i * rh:(i + 1) * rh, :] = x_ref[0, :, i, :]
    per_dot = C_ROWS // rh
    for j in range(C_GRP // per_dot):
        h = _rms(xs_ref[j * C_ROWS:(j + 1) * C_ROWS, :], g_ref[...]).astype(BF16)
        for c0 in range(0, n, C_NCHUNK):
            cols = slice(c0, min(c0 + C_NCHUNK, n))
            acc = jnp.dot(h, w_ref[:, cols], preferred_element_type=F32)
            for i in range(per_dot):
                o_ref[0, j * per_dot + i, :, cols] = acc[i * rh:(i + 1) * rh, :].astype(BF16)


def _inproj_perm(x4, g, w):
    bsz, r, _, _ = x4.shape
    n = w.shape[1]
    rh = min(r, C_SUB)
    return pl.pallas_call(
        _inproj_perm_kernel,
        grid=(bsz, r // C_GRP, r // rh),
        in_specs=[
            pl.BlockSpec((1, rh, C_GRP, D_MODEL), lambda b, m, s: (b, s, m, 0)),
            pl.BlockSpec((1, D_MODEL), lambda b, m, s: (0, 0)),
            pl.BlockSpec((D_MODEL, n), lambda b, m, s: (0, 0)),
        ],
        out_specs=pl.BlockSpec((1, C_GRP, rh, n), lambda b, m, s: (b, m, s, 0)),
        out_shape=jax.ShapeDtypeStruct((bsz, r, r, n), BF16),
        scratch_shapes=[pltpu.VMEM((C_GRP * rh, D_MODEL), F32)],
        compiler_params=_params(("parallel", "parallel", "parallel")),
        name="inproj_perm",
    )(x4, g, w)


def _dft_stage1_kernel(d_ref, x_ref, sp_ref, yr_ref, yi_ref, ysr_ref, ysi_ref):
    r = x_ref.shape[2]
    for i in range(C_GRP):
        dc = d_ref[i, :r, :]
        ds = d_ref[i, r:, :]
        yr_ref[0, :, i, :] = jnp.dot(dc, x_ref[0, i], preferred_element_type=F32)
        yi_ref[0, :, i, :] = jnp.dot(ds, x_ref[0, i], preferred_element_type=F32)
        ysr_ref[0, :, i, :] = jnp.dot(dc, sp_ref[0, i], preferred_element_type=F32)
        ysi_ref[0, :, i, :] = jnp.dot(ds, sp_ref[0, i], preferred_element_type=F32)


def _dft_stage1(proj4, dtab):
    bsz, r, _, _ = proj4.shape
    tc = 1024
    nct = 2 * C_MAIN // tc

    def yspec(width, idx):
        return pl.BlockSpec((1, r, C_GRP, width), lambda b, m, c: (b, 0, m, idx(c)))

    return pl.pallas_call(
        _dft_stage1_kernel,
        grid=(bsz, r // C_GRP, nct),
        in_specs=[
            pl.BlockSpec((C_GRP, 2 * r, r), lambda b, m, c: (m, 0, 0)),
            pl.BlockSpec((1, C_GRP, r, tc), lambda b, m, c: (b, m, 0, c)),
            pl.BlockSpec((1, C_GRP, r, LANES), lambda b, m, c: (b, m, 0, 4 * C_MAIN // LANES)),
        ],
        out_specs=[yspec(tc, lambda c: c), yspec(tc, lambda c: c),
                   yspec(LANES, lambda c: 0), yspec(LANES, lambda c: 0)],
        out_shape=[jax.ShapeDtypeStruct((bsz, r, r, 2 * C_MAIN), F32)] * 2
        + [jax.ShapeDtypeStruct((bsz, r, r, LANES), F32)] * 2,
        compiler_params=_params(("parallel", "parallel", "arbitrary")),
        name="dft_stage1",
    )(dtab, proj4, proj4)


def _dft_stage2_kernel(c1_ref, s1_ref, yra_ref, yia_ref, yrb_ref, yib_ref, gm_ref, gp_ref,
                       wm_ref, wp_ref, ysr_ref, ysi_ref, gsp_ref, wsp_ref, x_ref, o_ref, acc_ref):
    c = pl.program_id(2)
    r = c1_ref.shape[0]
    c1 = c1_ref[...]
    s1 = s1_ref[...]

    def dot(m, y):
        return jnp.dot(m, y.astype(BF16), preferred_element_type=F32)

    @pl.when(c == 0)
    def _():
        parts = []
        for i in range(C_GRP):
            es = dot(c1, ysr_ref[0, i]) - dot(s1, ysi_ref[0, i])
            parts.append((es * _silu(gsp_ref[0, i].astype(F32))).astype(BF16))
        nyq = jnp.dot(jnp.concatenate(parts, axis=0), wsp_ref[...], preferred_element_type=F32)
        for i in range(C_GRP):
            acc_ref[i * r:(i + 1) * r, :] = x_ref[0, :, i, :] + nyq[i * r:(i + 1) * r, :]

    minus, plus = [], []
    for i in range(C_GRP):
        ea = dot(c1, yra_ref[0, i]) - dot(s1, yia_ref[0, i])
        eb = dot(c1, yib_ref[0, i]) + dot(s1, yrb_ref[0, i])
        minus.append(((ea - eb) * _silu(gm_ref[0, i].astype(F32))).astype(BF16))
        plus.append(((ea + eb) * _silu(gp_ref[0, i].astype(F32))).astype(BF16))
    acc_ref[...] += (jnp.dot(jnp.concatenate(minus, axis=0), wm_ref[...], preferred_element_type=F32)
                     + jnp.dot(jnp.concatenate(plus, axis=0), wp_ref[...], preferred_element_type=F32))

    @pl.when(c == pl.num_programs(2) - 1)
    def _():
        for i in range(C_GRP):
            o_ref[0, :, i, :] = acc_ref[i * r:(i + 1) * r, :]


def _dft_stage2(c1, s1, y_r, y_i, y_sr, y_si, proj4, w_out, x4):
    bsz, r, _, _ = x4.shape
    tc = 512
    nct = C_MAIN // tc
    mspec = pl.BlockSpec((r, r), lambda b, m, c: (0, 0))

    def slab(width, idx):
        return pl.BlockSpec((1, C_GRP, r, width), lambda b, m, c: (b, m, 0, idx(c)))

    def wspec(rows, idx):
        return pl.BlockSpec((rows, D_MODEL), lambda b, m, c: (idx(c), 0))

    xspec = pl.BlockSpec((1, r, C_GRP, D_MODEL), lambda b, m, c: (b, 0, m, 0))
    sp_blk = 4 * C_MAIN // LANES
    return pl.pallas_call(
        _dft_stage2_kernel,
        grid=(bsz, r // C_GRP, nct),
        in_specs=[
            mspec, mspec,
            slab(tc, lambda c: c), slab(tc, lambda c: c),
            slab(tc, lambda c: nct + c), slab(tc, lambda c: nct + c),
            slab(tc, lambda c: 2 * nct + c), slab(tc, lambda c: 3 * nct + c),
            wspec(tc, lambda c: c), wspec(tc, lambda c: nct + c),
            slab(LANES, lambda c: 0), slab(LANES, lambda c: 0),
            slab(LANES, lambda c: sp_blk + 1),
            wspec(LANES, lambda c: 2 * C_MAIN // LANES),
            xspec,
        ],
        out_specs=xspec,
        out_shape=jax.ShapeDtypeStruct(x4.shape, F32),
        scratch_shapes=[pltpu.VMEM((C_GRP * r, D_MODEL), F32)],
        compiler_params=_params(("parallel", "parallel", "arbitrary")),
        name="dft_stage2",
    )(c1, s1, y_r, y_i, y_r, y_i, proj4, proj4, w_out, w_out, y_sr, y_si, proj4, w_out, x4)


def _fourier_long(x2d, g, wc, w_out, bsz, seq):
    r = math.isqrt(seq)
    assert r * r == seq and r % C_GRP == 0
    x4 = x2d.reshape(bsz, r, r, D_MODEL)
    proj4 = _inproj_perm(x4, g, wc)
    idx = jnp.arange(r, dtype=jnp.int32)
    tok = idx[:, None, None] + r * idx[None, None, :]
    dc, ds = _trig(idx[None, :, None] * tok, seq, r ** -0.5)
    dtab = jnp.concatenate([dc, ds], axis=1).astype(BF16)
    y_r, y_i, y_sr, y_si = _dft_stage1(proj4, dtab)
    c1, s1 = _trig(idx[:, None] * idx[None, :], r, r ** -0.5)
    out = _dft_stage2(c1.astype(BF16), s1.astype(BF16), y_r, y_i, y_sr, y_si, proj4, w_out, x4)
    return out.reshape(bsz * seq, D_MODEL)


OUT_TM = 512


def _outproj_kernel(mm_ref, mp_ref, ms_ref, x_ref, w_ref, o_ref):
    acc = jnp.dot(mm_ref[...], w_ref[:C_MAIN, :], preferred_element_type=F32)
    acc = acc + jnp.dot(mp_ref[...], w_ref[C_MAIN:2 * C_MAIN, :], preferred_element_type=F32)
    acc = acc + jnp.dot(ms_ref[...], w_ref[2 * C_MAIN:, :], preferred_element_type=F32)
    o_ref[...] = x_ref[...] + acc


def _outproj(minus2d, plus2d, nyq2d, x2d, w_out):
    t = x2d.shape[0]
    return pl.pallas_call(
        _outproj_kernel,
        grid=(t // OUT_TM,),
        in_specs=[
            pl.BlockSpec((OUT_TM, C_MAIN), lambda i: (i, 0)),
            pl.BlockSpec((OUT_TM, C_MAIN), lambda i: (i, 0)),
            pl.BlockSpec((OUT_TM, LANES), lambda i: (i, 0)),
            pl.BlockSpec((OUT_TM, D_MODEL), lambda i: (i, 0)),
            pl.BlockSpec((2 * C_MAIN + LANES, D_MODEL), lambda i: (0, 0)),
        ],
        out_specs=pl.BlockSpec((OUT_TM, D_MODEL), lambda i: (i, 0)),
        out_shape=jax.ShapeDtypeStruct((t, D_MODEL), F32),
        compiler_params=_params(("parallel",)),
        name="outproj",
    )(minus2d, plus2d, nyq2d, x2d, w_out)


def _rope_tables_a(seq):
    half = A_ROT // 2
    inv_freq = jnp.exp(-(jnp.arange(half, dtype=F32) * (2.0 / A_ROT)) * math.log(A_THETA))
    ang = jnp.arange(seq, dtype=F32)[:, None] * inv_freq[None, :]
    cos, sin = jnp.cos(ang), jnp.sin(ang)
    d = jnp.arange(LANES) % A_HEAD
    f = d % half
    cm = jnp.where(d[None, :] < A_ROT, cos[:, f], 1.0)
    s1 = jnp.where(d[None, :] < half, -sin[:, f], 0.0)
    s2 = jnp.where((d[None, :] >= half) & (d[None, :] < A_ROT), sin[:, f], 0.0)
    qs = A_HEAD ** -0.5 * LOG2E
    return jnp.stack([cm * qs, s1 * qs, s2 * qs, cm, s1, s2]).astype(F32)


def _rope_tables_b(seq):
    half = B_QK // 2
    inv_freq = jnp.exp(-(jnp.arange(half, dtype=F32) * (2.0 / B_QK)) * math.log(B_THETA))
    ang = jnp.arange(seq, dtype=F32)[:, None] * inv_freq[None, :]
    cos, sin = jnp.cos(ang), jnp.sin(ang)
    ks = B_QK ** -0.5
    return jnp.stack([cos, sin, cos * ks, sin * ks]).astype(F32)


def _weights_a(w_in):
    nq = A_QH * A_HEAD
    nkv = A_KVH * A_HEAD
    q = w_in[:, :nq]
    kv = w_in[:, nq:nq + 2 * nkv]
    gate = w_in[:, nq + 2 * nkv:]
    return jnp.concatenate([q, gate, kv], axis=1).astype(BF16)


def _c_index():
    g = jnp.arange(C_GROUPS, dtype=jnp.int32)[:, None] * C_GDIM
    m = jnp.arange(C_HALF, dtype=jnp.int32)[None, :]
    minus = (g + m).reshape(-1)
    plus = (g + jnp.where(m == 0, 0, C_GDIM - m)).reshape(-1)
    nyq = (g + C_HALF).reshape(-1)
    return minus, plus, nyq, jnp.tile(m == 0, (C_GROUPS, 1)).reshape(-1)


def _weights_c(w_in):
    ch = jnp.arange(C_GDIM, dtype=jnp.int32)
    m = jnp.arange(C_HALF, dtype=jnp.int32)
    cm, sm = _trig(ch[:, None] * m[None, :], C_GDIM, C_GDIM ** -0.5)
    nyq, _ = _trig(ch * C_HALF, C_GDIM, C_GDIM ** -0.5)
    lane = jnp.arange(LANES)[None, None, :] == jnp.arange(C_GROUPS)[:, None, None]
    nyq = jnp.where(lane, nyq[None, :, None], 0.0)
    cs = jnp.concatenate([jnp.broadcast_to(cm, (C_GROUPS,) + cm.shape),
                          jnp.broadcast_to(sm, (C_GROUPS,) + sm.shape), nyq], axis=2)
    w_a, w_b, w_sp = _fold_channel_dft(w_in[:, :BRANCH], cs)
    minus, plus, nyq_idx, _ = _c_index()
    gate = w_in[:, BRANCH:]
    g_sp = jnp.zeros((D_MODEL, LANES), F32).at[:, :C_GROUPS].set(gate[:, nyq_idx])
    return jnp.concatenate([w_a, w_b, gate[:, minus].astype(BF16), gate[:, plus].astype(BF16),
                            w_sp.astype(BF16), g_sp.astype(BF16)], axis=1)


def _weights_c_out(w_out):
    minus, plus, nyq_idx, dup = _c_index()
    w_plus = jnp.where(dup[:, None], 0.0, w_out[plus])
    w_sp = jnp.zeros((LANES, D_MODEL), F32).at[:C_GROUPS].set(w_out[nyq_idx])
    return jnp.concatenate([w_out[minus], w_plus, w_sp], axis=0).astype(BF16)


def _trunk(x, norm_g, fin_g, wa, a_sink, a_w_out, wb, log_g, b_w_out, wc, c_w_out):
    bsz, seq, _ = x.shape
    t = bsz * seq
    x2d = x.reshape(t, D_MODEL)
    tabs_a = _rope_tables_a(seq)
    tabs_b = _rope_tables_b(seq)
    fin = fin_g.reshape(1, D_MODEL)

    def layer_a(x2d, layer, j, final):
        q3, gate3, kk, vt = _inproj_a(x2d, norm_g[layer].reshape(1, D_MODEL), wa[j], tabs_a, bsz, seq)
        return _attention(x2d, q3, gate3, kk, vt, a_sink[j], a_w_out[j], fin, bsz, seq, final)

    x2d = layer_a(x2d, 0, 0, False)

    proj = _inproj(x2d, norm_g[1].reshape(1, D_MODEL), wb, tabs_b, seq, 2 * B_HEADS)
    x2d = _retention(proj.reshape(bsz, seq, -1), log_g, x2d.reshape(bsz, seq, D_MODEL),
                     b_w_out).reshape(t, D_MODEL)

    if seq <= C_DIRECT_MAX:
        proj = _inproj(x2d, norm_g[2].reshape(1, D_MODEL), wc, None, seq, 0)
        minus, plus, nyq = _fourier_mix(proj.reshape(bsz, seq, -1))
        x2d = _outproj(minus.reshape(t, C_MAIN), plus.reshape(t, C_MAIN), nyq.reshape(t, LANES),
                       x2d, c_w_out)
    else:
        x2d = _fourier_long(x2d, norm_g[2].reshape(1, D_MODEL), wc, c_w_out, bsz, seq)

    x2d = layer_a(x2d, 3, 1, True)
    return x2d.reshape(bsz, seq, D_MODEL)


def kernel(x_prompt, x_sample, norm_g, final_norm_g, a_w_in, a_sink, a_w_out, b_w_in, b_decay,
           b_w_out, c_w_in, c_w_out):
    wa = [_weights_a(a_w_in[j]) for j in range(a_w_in.shape[0])]
    a_out = [a_w_out[j].astype(BF16) for j in range(a_w_out.shape[0])]
    wb = b_w_in[0].astype(BF16)
    log_g = jax.nn.log_sigmoid(b_decay[0].astype(F32))
    wc = _weights_c(c_w_in[0])
    args = (norm_g, final_norm_g, wa, a_sink, a_out, wb, log_g, b_w_out[0].astype(BF16),
            wc, _weights_c_out(c_w_out[0]))
    return (_trunk(x_prompt, *args), _trunk(x_sample, *args))
```

```python
import functools
import math

import jax
import jax.numpy as jnp
from jax import lax
from jax.experimental import pallas as pl
from jax.experimental.pallas import tpu as pltpu

D_MODEL = 1024
BRANCH = 2048
NORM_EPS = 1e-6

A_HEAD = 64
A_QH = 32
A_KVH = 4
A_ROT = 16
A_THETA = 500000.0
A_WIN = 128
A_QBLK = 1024
A_SPAN = A_QBLK + 2 * A_WIN
A_PAIRS = A_QH // 2
A_LOOKAHEAD = 8
A_OUT_GROUPS = 2
A_VPAD = 16
LOG2E = math.log2(math.e)

B_QK = 256
B_HEADS = 4
B_V = 512
B_CHUNK = 256
B_ROWS = 2
B_THETA = 10000.0

C_GROUPS = 4
C_GDIM = 512
C_DIRECT_MAX = 2048

LANES = 128
VMEM_LIMIT = 56 * 1024 * 1024

BF16 = jnp.bfloat16
F32 = jnp.float32


def _params(sem):
    return pltpu.CompilerParams(dimension_semantics=sem, vmem_limit_bytes=VMEM_LIMIT)


def _silu(x):
    return x * jax.nn.sigmoid(x)


def _rms(x, g):
    ms = jnp.mean(x * x, axis=-1, keepdims=True)
    return (x * lax.rsqrt(ms + NORM_EPS)) * g


IN_TM = 1024
IN_TN = 512


def _inproj_a_kernel(x_ref, g_ref, w_ref, tab_ref, q_ref, gate_ref, kk_ref, vt_ref):
    h = _rms(x_ref[...], g_ref[...]).astype(BF16)

    def rope(a, t0):
        return (a * tab_ref[t0] + pltpu.roll(a, LANES - A_ROT // 2, 1) * tab_ref[t0 + 1]
                + pltpu.roll(a, A_ROT // 2, 1) * tab_ref[t0 + 2])

    for c in range(4):
        acc = jnp.dot(h, w_ref[:, c * IN_TN:(c + 1) * IN_TN], preferred_element_type=F32)
        for s in range(4):
            q_ref[4 * c + s] = rope(acc[:, s * LANES:(s + 1) * LANES], 0).astype(BF16)
    for c in range(4):
        acc = jnp.dot(h, w_ref[:, BRANCH + c * IN_TN:BRANCH + (c + 1) * IN_TN],
                      preferred_element_type=F32)
        for s in range(4):
            gate_ref[4 * c + s] = acc[:, s * LANES:(s + 1) * LANES].astype(BF16)
    acc = jnp.dot(h, w_ref[:, 2 * BRANCH:], preferred_element_type=F32)
    nkv = A_KVH * A_HEAD
    lane = lax.broadcasted_iota(jnp.int32, (IN_TM, LANES), 1)
    for p in range(nkv // LANES):
        a = rope(acc[:, p * LANES:(p + 1) * LANES], 3)
        swapped = pltpu.roll(a, A_HEAD, 1)
        kk_ref[2 * p] = jnp.where(lane < A_HEAD, a, swapped).astype(BF16)
        kk_ref[2 * p + 1] = jnp.where(lane >= A_HEAD, a, swapped).astype(BF16)
    vt_ref[0] = acc[:, nkv:].T.astype(BF16)


def _inproj_a(x2d, g, w, tabs, bsz, seq):
    t = x2d.shape[0]
    nblk = seq // IN_TM
    n = w.shape[1]
    return pl.pallas_call(
        _inproj_a_kernel,
        grid=(t // IN_TM,),
        in_specs=[
            pl.BlockSpec((IN_TM, D_MODEL), lambda i: (i, 0)),
            pl.BlockSpec((1, D_MODEL), lambda i: (0, 0)),
            pl.BlockSpec((D_MODEL, n), lambda i: (0, 0)),
            pl.BlockSpec((6, IN_TM, LANES), lambda i: (0, i % nblk, 0)),
        ],
        out_specs=[
            pl.BlockSpec((A_PAIRS, IN_TM, LANES), lambda i: (0, i, 0)),
            pl.BlockSpec((A_PAIRS, IN_TM, LANES), lambda i: (0, i, 0)),
            pl.BlockSpec((A_KVH, IN_TM, LANES), lambda i: (0, i, 0)),
            pl.BlockSpec((1, A_KVH * A_HEAD, IN_TM), lambda i: (i // nblk, 0, i % nblk)),
        ],
        out_shape=[
            jax.ShapeDtypeStruct((A_PAIRS, t, LANES), BF16),
            jax.ShapeDtypeStruct((A_PAIRS, t, LANES), BF16),
            jax.ShapeDtypeStruct((A_KVH, t, LANES), BF16),
            jax.ShapeDtypeStruct((bsz, A_KVH * A_HEAD, seq), BF16),
        ],
        compiler_params=_params(("parallel",)),
        name="inproj_a",
    )(x2d, g, w, tabs)


def _inproj_kernel(x_ref, g_ref, w_ref, *rest, rope_heads):
    if rope_heads:
        tab_ref, o_ref = rest
    else:
        (o_ref,) = rest
    h = _rms(x_ref[...], g_ref[...]).astype(BF16)
    n = w_ref.shape[1]
    for c0 in range(0, n, IN_TN):
        width = min(IN_TN, n - c0)
        acc = jnp.dot(h, w_ref[:, c0:c0 + width], preferred_element_type=F32)
        for lo in range(0, width, B_QK):
            head = (c0 + lo) // B_QK
            if head < rope_heads:
                t0 = 0 if head < rope_heads // 2 else 2
                cos, sin = tab_ref[t0], tab_ref[t0 + 1]
                x1 = acc[:, lo:lo + LANES]
                x2 = acc[:, lo + LANES:lo + B_QK]
                o_ref[:, c0 + lo:c0 + lo + LANES] = (x1 * cos - x2 * sin).astype(BF16)
                o_ref[:, c0 + lo + LANES:c0 + lo + B_QK] = (x2 * cos + x1 * sin).astype(BF16)
            else:
                o_ref[:, c0 + lo:c0 + lo + B_QK] = acc[:, lo:lo + B_QK].astype(BF16)


def _inproj(x2d, g, w, tabs, seq, rope_heads):
    t = x2d.shape[0]
    n = w.shape[1]
    nblk = seq // IN_TM
    in_specs = [
        pl.BlockSpec((IN_TM, D_MODEL), lambda i: (i, 0)),
        pl.BlockSpec((1, D_MODEL), lambda i: (0, 0)),
        pl.BlockSpec((D_MODEL, n), lambda i: (0, 0), pipeline_mode=pl.Buffered(1)),
    ]
    args = [x2d, g, w]
    if rope_heads:
        in_specs.append(pl.BlockSpec((4, IN_TM, LANES), lambda i: (0, i % nblk, 0)))
        args.append(tabs)
    return pl.pallas_call(
        functools.partial(_inproj_kernel, rope_heads=rope_heads),
        grid=(t // IN_TM,),
        in_specs=in_specs,
        out_specs=pl.BlockSpec((IN_TM, n), lambda i: (i, 0)),
        out_shape=jax.ShapeDtypeStruct((t, n), BF16),
        compiler_params=_params(("parallel",)),
        name="inproj",
    )(*args)


def _attn_kernel(sink_ref, q_ref, gate_ref, kp_ref, kc_ref, kn_ref, vp_ref, vc_ref, vn_ref,
                 x_ref, w_ref, fg_ref, o_ref, k_ref, v_ref, bias_ref, *, final):
    i = pl.program_id(1)
    last = pl.num_programs(1) - 1
    rows = 3 * A_WIN
    nslice = A_QBLK // A_WIN
    per_kv = A_PAIRS // A_KVH

    ones_row = (lax.broadcasted_iota(jnp.int32, (A_VPAD, A_SPAN), 0) == 0).astype(BF16)
    for h in range(A_KVH):
        k_ref[h] = jnp.concatenate([kp_ref[h], kc_ref[h], kn_ref[h]], axis=0)
        v_ref[h, :A_HEAD, :] = jnp.concatenate([vp_ref[0, h], vc_ref[0, h], vn_ref[0, h]], axis=1)
        v_ref[h, A_HEAD:, :] = ones_row

    r = lax.broadcasted_iota(jnp.int32, (A_WIN, LANES), 0)
    c = lax.broadcasted_iota(jnp.int32, (A_WIN, LANES), 1)
    neg = jnp.full((A_WIN, LANES), -1e30, F32)
    top = jnp.where(r >= c, 0.0, neg)
    bot = jnp.where(r <= c, 0.0, neg)
    bias_ref[0] = jnp.where(i == 0, neg, top)
    bias_ref[1] = top
    bias_ref[2] = bot
    bias_ref[3] = jnp.where(i == last, neg, bot)
    qlane = lax.broadcasted_iota(jnp.int32, (LANES, LANES), 1)

    units = [(g, jl, jj) for g in range(A_KVH) for jl in range(per_kv) for jj in range(nslice)]

    def scores(n):
        g, jl, jj = units[n]
        q = q_ref[g * per_kv + jl, jj * LANES:(jj + 1) * LANES, :]
        zero = jnp.zeros_like(q)
        rhs = jnp.concatenate([jnp.where(qlane < A_HEAD, q, zero),
                               jnp.where(qlane >= A_HEAD, q, zero)], axis=0)
        return lax.dot_general(k_ref[g, jj * A_WIN:jj * A_WIN + rows, :], rhs,
                               (((1,), (1,)), ((), ())), preferred_element_type=F32)

    def softmax_pv(s_t, sinks, g, jj):
        ps, extra = [], []
        for a in range(2):
            blk = s_t[:, a * LANES:(a + 1) * LANES]
            parts = [blk[:A_WIN] + bias_ref[0 if jj == 0 else 1], blk[A_WIN:2 * A_WIN],
                     blk[2 * A_WIN:] + bias_ref[3 if jj == nslice - 1 else 2]]
            mx = jnp.full((1, LANES), sinks[a], F32)
            for part in parts:
                mx = jnp.maximum(mx, jnp.max(part, axis=0, keepdims=True))
            ps.append(jnp.concatenate([jnp.exp2(part - mx).astype(BF16) for part in parts], axis=0))
            extra.append(jnp.exp2(sinks[a] - mx))
        o_ext = jnp.dot(v_ref[g, :, jj * A_WIN:jj * A_WIN + rows], jnp.concatenate(ps, axis=1),
                        preferred_element_type=F32)
        den = o_ext[A_HEAD:A_HEAD + 1, :] + jnp.concatenate(extra, axis=1)
        return o_ext[:A_HEAD, :] / den

    pending = [scores(n) for n in range(A_LOOKAHEAD)]
    outs, gated = [], []
    y = x_ref[...]
    for n, (g, jl, jj) in enumerate(units):
        if n + A_LOOKAHEAD < len(units):
            pending.append(scores(n + A_LOOKAHEAD))
        j = g * per_kv + jl
        sinks = (sink_ref[2 * j] * LOG2E, sink_ref[2 * j + 1] * LOG2E)
        outs.append(softmax_pv(pending.pop(0), sinks, g, jj))
        if jj == nslice - 1:
            last_outs = outs[-nslice:]
            o_pair = jnp.concatenate(
                [jnp.concatenate([o[:, :LANES] for o in last_outs], axis=1),
                 jnp.concatenate([o[:, LANES:] for o in last_outs], axis=1)], axis=0).T
            gt = gate_ref[j].astype(F32)
            gated.append((o_pair * _silu(gt)).astype(BF16))
            if jl == per_kv - 1 and (g + 1) % A_OUT_GROUPS == 0:
                npair = per_kv * A_OUT_GROUPS
                wg = w_ref[(j + 1 - npair) * LANES:(j + 1) * LANES, :]
                y = y + jnp.dot(jnp.concatenate(gated[-npair:], axis=1), wg,
                                preferred_element_type=F32)
    if final:
        y = _rms(y, fg_ref[...])
    o_ref[...] = y


def _attention(x2d, q3, gate3, kk, vt, sink, w_out, fin_g, bsz, seq, final):
    t = x2d.shape[0]
    nqb = seq // A_QBLK
    nkb = seq // A_WIN
    nsl = A_QBLK // A_WIN
    vt4 = vt.reshape(bsz, A_KVH, A_HEAD, seq)
    grid_spec = pltpu.PrefetchScalarGridSpec(
        num_scalar_prefetch=1,
        grid=(bsz, nqb),
        in_specs=[
            pl.BlockSpec((A_PAIRS, A_QBLK, LANES), lambda b, i, s: (0, b * nqb + i, 0)),
            pl.BlockSpec((A_PAIRS, A_QBLK, LANES), lambda b, i, s: (0, b * nqb + i, 0)),
            pl.BlockSpec((A_KVH, A_WIN, LANES),
                         lambda b, i, s: (0, b * nkb + jnp.maximum(nsl * i - 1, 0), 0)),
            pl.BlockSpec((A_KVH, A_QBLK, LANES), lambda b, i, s: (0, b * nqb + i, 0)),
            pl.BlockSpec((A_KVH, A_WIN, LANES),
                         lambda b, i, s: (0, b * nkb + jnp.minimum(nsl * i + nsl, nkb - 1), 0)),
            pl.BlockSpec((1, A_KVH, A_HEAD, A_WIN),
                         lambda b, i, s: (b, 0, 0, jnp.maximum(nsl * i - 1, 0))),
            pl.BlockSpec((1, A_KVH, A_HEAD, A_QBLK), lambda b, i, s: (b, 0, 0, i)),
            pl.BlockSpec((1, A_KVH, A_HEAD, A_WIN),
                         lambda b, i, s: (b, 0, 0, jnp.minimum(nsl * i + nsl, nkb - 1))),
            pl.BlockSpec((A_QBLK, D_MODEL), lambda b, i, s: (b * nqb + i, 0)),
            pl.BlockSpec((BRANCH, D_MODEL), lambda b, i, s: (0, 0)),
            pl.BlockSpec((1, D_MODEL), lambda b, i, s: (0, 0)),
        ],
        out_specs=pl.BlockSpec((A_QBLK, D_MODEL), lambda b, i, s: (b * nqb + i, 0)),
        scratch_shapes=[
            pltpu.VMEM((A_KVH, A_SPAN, LANES), BF16),
            pltpu.VMEM((A_KVH, A_HEAD + A_VPAD, A_SPAN), BF16),
            pltpu.VMEM((4, A_WIN, LANES), F32),
        ],
    )
    return pl.pallas_call(
        functools.partial(_attn_kernel, final=final),
        grid_spec=grid_spec,
        out_shape=jax.ShapeDtypeStruct((t, D_MODEL), F32),
        compiler_params=_params(("parallel", "parallel")),
        name="attention",
    )(sink, q3, gate3, kk, kk, kk, vt4, vt4, vt4, x2d, w_out, fin_g)


def _retention_start(lg_ref, st_ref, intra_ref, dec_ref, backward):
    n = B_CHUNK
    ii = lax.broadcasted_iota(jnp.int32, (n, n), 0)
    jj = lax.broadcasted_iota(jnp.int32, (n, n), 1)
    col = lax.broadcasted_iota(jnp.int32, (n, LANES), 0).astype(F32)
    st_ref[...] = jnp.zeros_like(st_ref)
    for h in range(B_HEADS):
        lg = lg_ref[1 if backward else 0, h]
        if backward:
            mask = jj > ii
            dist = (jj - ii).astype(F32)
            dec_ref[h, 0] = jnp.exp((n - col) * lg)
            dec_ref[h, 1] = jnp.exp(col * lg)
        else:
            mask = ii >= jj
            dist = (ii - jj).astype(F32)
            dec_ref[h, 0] = jnp.exp((col + 1.0) * lg)
            dec_ref[h, 1] = jnp.exp((n - 1.0 - col) * lg)
        intra_ref[h] = jnp.where(mask, jnp.exp(jnp.where(mask, dist, 0.0) * lg), 0.0)


def _retention_open(q_ref, k_ref, v_ref, row):
    heads = range(B_HEADS)
    qs = [q_ref[row, :, h * B_QK:(h + 1) * B_QK] for h in heads]
    ks = [k_ref[row, :, h * B_QK:(h + 1) * B_QK] for h in heads]
    vs = [v_ref[row, :, h * B_V:(h + 1) * B_V] for h in heads]
    sc = [lax.dot_general(qs[h], ks[h], (((1,), (1,)), ((), ())), preferred_element_type=F32)
          for h in heads]
    return qs, ks, vs, sc


def _retention_heads(lg_ref, opened, st_ref, intra_ref, dec_ref, row, backward, emit):
    qs, ks, vs, sc = opened
    for h in range(B_HEADS):
        q_dec = jnp.concatenate([dec_ref[h, 0]] * (B_QK // LANES), axis=1)
        k_dec = jnp.concatenate([dec_ref[h, 1]] * (B_QK // LANES), axis=1)
        c_dec = jnp.exp(jnp.full((1, B_V), B_CHUNK * lg_ref[1 if backward else 0, h], F32))
        st = st_ref[row, h]
        lhs = jnp.concatenate([(sc[h] * intra_ref[h]).astype(BF16),
                               (qs[h].astype(F32) * q_dec).astype(BF16)], axis=1)
        rhs = jnp.concatenate([vs[h], st.astype(BF16)], axis=0)
        out = jnp.dot(lhs, rhs, preferred_element_type=F32)
        kd_t = (ks[h].astype(F32) * k_dec).T.astype(BF16)
        st_ref[row, h] = st * c_dec + jnp.dot(kd_t, vs[h], preferred_element_type=F32)
        emit(h, out)


def _retention_bwd_kernel(lg_ref, q_ref, k_ref, v_ref, o_ref, st_ref, intra_ref, dec_ref):
    @pl.when(pl.program_id(1) == 0)
    def _():
        _retention_start(lg_ref, st_ref, intra_ref, dec_ref, True)

    rows = range(q_ref.shape[0])
    opened = [_retention_open(q_ref, k_ref, v_ref, row) for row in rows]
    for row in rows:
        def emit(h, o, row=row):
            o_ref[row, :, h * B_V:(h + 1) * B_V] = o.astype(BF16)

        _retention_heads(lg_ref, opened[row], st_ref, intra_ref, dec_ref, row, True, emit)


def _retention_fwd_kernel(lg_ref, q_ref, k_ref, v_ref, gate_ref, ob_ref, x_ref, w_ref, o_ref,
                          st_ref, intra_ref, dec_ref):
    @pl.when(pl.program_id(1) == 0)
    def _():
        _retention_start(lg_ref, st_ref, intra_ref, dec_ref, False)

    rows = range(q_ref.shape[0])
    opened = [_retention_open(q_ref, k_ref, v_ref, row) for row in rows]
    gated = []
    for row in rows:
        parts = []

        def emit(h, o_f, row=row, parts=parts):
            sl = slice(h * B_V, (h + 1) * B_V)
            o = o_f + ob_ref[row, :, sl].astype(F32)
            on = o * lax.rsqrt(jnp.mean(o * o, axis=-1, keepdims=True) + NORM_EPS)
            parts.append((on * _silu(gate_ref[row, :, sl].astype(F32))).astype(BF16))

        _retention_heads(lg_ref, opened[row], st_ref, intra_ref, dec_ref, row, False, emit)
        gated.append(jnp.concatenate(parts, axis=1))
    for row in rows:
        o_ref[row] = x_ref[row] + jnp.dot(gated[row], w_ref[...], preferred_element_type=F32)


def _retention(proj3, log_g, x3, w_out):
    bsz, seq, _ = proj3.shape
    nc = seq // B_CHUNK
    nqk = B_HEADS * B_QK
    rows = B_ROWS if bsz % B_ROWS == 0 else 1
    smem = pl.BlockSpec(memory_space=pltpu.SMEM)
    state = [pltpu.VMEM((rows, B_HEADS, B_QK, B_V), F32),
             pltpu.VMEM((B_HEADS, B_CHUNK, B_CHUNK), F32),
             pltpu.VMEM((B_HEADS, 2, B_CHUNK, LANES), F32)]

    def blk(width, chunk, col):
        return pl.BlockSpec((rows, B_CHUNK, width), lambda b, t: (b, chunk(t), col))

    def rev(t):
        return nc - 1 - t

    def fwd(t):
        return t

    o_b = pl.pallas_call(
        _retention_bwd_kernel,
        grid=(bsz // rows, nc),
        in_specs=[smem, blk(nqk, rev, 0), blk(nqk, rev, 1), blk(BRANCH, rev, 1)],
        out_specs=blk(BRANCH, rev, 0),
        out_shape=jax.ShapeDtypeStruct((bsz, seq, BRANCH), BF16),
        scratch_shapes=state,
        compiler_params=_params(("parallel", "arbitrary")),
        name="retention_bwd",
    )(log_g, proj3, proj3, proj3)
    return pl.pallas_call(
        _retention_fwd_kernel,
        grid=(bsz // rows, nc),
        in_specs=[smem, blk(nqk, fwd, 0), blk(nqk, fwd, 1), blk(BRANCH, fwd, 1),
                  blk(BRANCH, fwd, 2), blk(BRANCH, fwd, 0), blk(D_MODEL, fwd, 0),
                  pl.BlockSpec((BRANCH, D_MODEL), lambda b, t: (0, 0))],
        out_specs=blk(D_MODEL, fwd, 0),
        out_shape=jax.ShapeDtypeStruct((bsz, seq, D_MODEL), F32),
        scratch_shapes=state,
        compiler_params=_params(("parallel", "arbitrary")),
        name="retention_fwd",
    )(log_g, proj3, proj3, proj3, proj3, o_b, x3, w_out)


C_HALF = C_GDIM // 2
C_MAIN = C_GROUPS * C_HALF
C_FOLD = 2 * C_HALF + LANES
C_NYQ_ROWS = 16


def _fold_kernel(w_ref, cs_ref, a_ref, b_ref, sp_ref):
    r = jnp.dot(w_ref[...], cs_ref[0], preferred_element_type=F32,
                precision=lax.Precision.HIGHEST)
    a_ref[...] = r[:, :C_HALF].astype(BF16)
    b_ref[...] = r[:, C_HALF:2 * C_HALF].astype(BF16)

    @pl.when(pl.program_id(0) == 0)
    def _():
        sp_ref[...] = jnp.zeros_like(sp_ref)

    sp_ref[...] += r[:, 2 * C_HALF:]


def _fold_channel_dft(w_u, cs):
    spec = pl.BlockSpec((D_MODEL, C_HALF), lambda g: (0, g))
    return pl.pallas_call(
        _fold_kernel,
        grid=(C_GROUPS,),
        in_specs=[pl.BlockSpec((D_MODEL, C_GDIM), lambda g: (0, g)),
                  pl.BlockSpec((1, C_GDIM, C_FOLD), lambda g: (g, 0, 0))],
        out_specs=[spec, spec, pl.BlockSpec((D_MODEL, LANES), lambda g: (0, 0))],
        out_shape=[jax.ShapeDtypeStruct((D_MODEL, C_MAIN), BF16)] * 2
        + [jax.ShapeDtypeStruct((D_MODEL, LANES), F32)],
        compiler_params=_params(("arbitrary",)),
        name="fold_channel_dft",
    )(w_u, cs)


DFT_ROWS = 512


def _dft_mix_kernel(cm_ref, sm_ref, a_ref, b_ref, gm_ref, gp_ref, sp_ref, gsp_ref,
                    om_ref, op_ref, osp_ref):
    n = cm_ref.shape[0]
    rc = min(n, DFT_ROWS)
    a = a_ref[0]
    b = b_ref[0]
    for r in range(n // rc):
        rows = slice(r * rc, (r + 1) * rc)
        ea = jnp.dot(cm_ref[rows, :], a, preferred_element_type=F32)
        eb = jnp.dot(sm_ref[rows, :], b, preferred_element_type=F32)
        om_ref[0, rows, :] = ((ea - eb) * _silu(gm_ref[0, rows, :].astype(F32))).astype(BF16)
        op_ref[0, rows, :] = ((ea + eb) * _silu(gp_ref[0, rows, :].astype(F32))).astype(BF16)

    @pl.when(pl.program_id(1) == 0)
    def _():
        sp_t = sp_ref[0].astype(F32).T[:C_NYQ_ROWS, :].astype(BF16)
        es_t = jnp.dot(sp_t, cm_ref[...], preferred_element_type=F32)
        es = jnp.concatenate([es_t, jnp.zeros((LANES - C_NYQ_ROWS, n), F32)], axis=0).T
        osp_ref[0] = (es * _silu(gsp_ref[0].astype(F32))).astype(BF16)


def _trig(m, period, scale):
    ang = (2.0 * math.pi / period) * (m % period).astype(F32)
    return jnp.cos(ang) * scale, jnp.sin(ang) * scale


def _fourier_mix(proj3):
    bsz, seq, _ = proj3.shape
    idx = jnp.arange(seq, dtype=jnp.int32)
    cm, sm = _trig(idx[:, None] * idx[None, :], seq, seq ** -0.5)
    tc = 512
    nct = C_MAIN // tc
    mspec = pl.BlockSpec((seq, seq), lambda b, c: (0, 0))

    def dspec(width, idx):
        return pl.BlockSpec((1, seq, width), lambda b, c: (b, 0, idx(c)))

    sp_blk = 4 * C_MAIN // LANES
    return pl.pallas_call(
        _dft_mix_kernel,
        grid=(bsz, nct),
        in_specs=[mspec, mspec,
                  dspec(tc, lambda c: c), dspec(tc, lambda c: nct + c),
                  dspec(tc, lambda c: 2 * nct + c), dspec(tc, lambda c: 3 * nct + c),
                  dspec(LANES, lambda c: sp_blk), dspec(LANES, lambda c: sp_blk + 1)],
        out_specs=[dspec(tc, lambda c: c), dspec(tc, lambda c: c), dspec(LANES, lambda c: 0)],
        out_shape=[jax.ShapeDtypeStruct((bsz, seq, C_MAIN), BF16)] * 2
        + [jax.ShapeDtypeStruct((bsz, seq, LANES), BF16)],
        compiler_params=_params(("parallel", "arbitrary")),
        name="dft_mix",
    )(cm.astype(BF16), sm.astype(BF16), proj3, proj3, proj3, proj3, proj3, proj3)


C_GRP = 8
C_SUB = 64
C_ROWS = 256
C_NCHUNK = 1024


def _inproj_perm_kernel(x_ref, g_ref, w_ref, o_ref, xs_ref):
    rh = x_ref.shape[1]
    n = w_ref.shape[1]
    for i in range(C_GRP):
        xs_ref[i * rh:(i + 1) * rh, :] = x_ref[0, :, i, :]
    per_dot = C_ROWS // rh
    for j in range(C_GRP // per_dot):
        h = _rms(xs_ref[j * C_ROWS:(j + 1) * C_ROWS, :], g_ref[...]).astype(BF16)
        for c0 in range(0, n, C_NCHUNK):
            cols = slice(c0, min(c0 + C_NCHUNK, n))
            acc = jnp.dot(h, w_ref[:, cols], preferred_element_type=F32)
            for i in range(per_dot):
                o_ref[0, j * per_dot + i, :, cols] = acc[i * rh:(i + 1) * rh, :].astype(BF16)


def _inproj_perm(x4, g, w):
    bsz, r, _, _ = x4.shape
    n = w.shape[1]
    rh = min(r, C_SUB)
    return pl.pallas_call(
        _inproj_perm_kernel,
        grid=(bsz, r // C_GRP, r // rh),
        in_specs=[
            pl.BlockSpec((1, rh, C_GRP, D_MODEL), lambda b, m, s: (b, s, m, 0)),
            pl.BlockSpec((1, D_MODEL), lambda b, m, s: (0, 0)),
            pl.BlockSpec((D_MODEL, n), lambda b, m, s: (0, 0)),
        ],
        out_specs=pl.BlockSpec((1, C_GRP, rh, n), lambda b, m, s: (b, m, s, 0)),
        out_shape=jax.ShapeDtypeStruct((bsz, r, r, n), BF16),
        scratch_shapes=[pltpu.VMEM((C_GRP * rh, D_MODEL), F32)],
        compiler_params=_params(("parallel", "parallel", "parallel")),
        name="inproj_perm",
    )(x4, g, w)


def _dft_stage1_kernel(d_ref, x_ref, sp_ref, yr_ref, yi_ref, ysr_ref, ysi_ref):
    r = x_ref.shape[2]
    for i in range(C_GRP):
        dc = d_ref[i, :r, :]
        ds = d_ref[i, r:, :]
        yr_ref[0, :, i, :] = jnp.dot(dc, x_ref[0, i], preferred_element_type=F32)
        yi_ref[0, :, i, :] = jnp.dot(ds, x_ref[0, i], preferred_element_type=F32)
        ysr_ref[0, :, i, :] = jnp.dot(dc, sp_ref[0, i], preferred_element_type=F32)
        ysi_ref[0, :, i, :] = jnp.dot(ds, sp_ref[0, i], preferred_element_type=F32)


def _dft_stage1(proj4, dtab):
    bsz, r, _, _ = proj4.shape
    tc = 1024
    nct = 2 * C_MAIN // tc

    def yspec(width, idx):
        return pl.BlockSpec((1, r, C_GRP, width), lambda b, m, c: (b, 0, m, idx(c)))

    return pl.pallas_call(
        _dft_stage1_kernel,
        grid=(bsz, r // C_GRP, nct),
        in_specs=[
            pl.BlockSpec((C_GRP, 2 * r, r), lambda b, m, c: (m, 0, 0)),
            pl.BlockSpec((1, C_GRP, r, tc), lambda b, m, c: (b, m, 0, c)),
            pl.BlockSpec((1, C_GRP, r, LANES), lambda b, m, c: (b, m, 0, 4 * C_MAIN // LANES)),
        ],
        out_specs=[yspec(tc, lambda c: c), yspec(tc, lambda c: c),
                   yspec(LANES, lambda c: 0), yspec(LANES, lambda c: 0)],
        out_shape=[jax.ShapeDtypeStruct((bsz, r, r, 2 * C_MAIN), F32)] * 2
        + [jax.ShapeDtypeStruct((bsz, r, r, LANES), F32)] * 2,
        compiler_params=_params(("parallel", "parallel", "arbitrary")),
        name="dft_stage1",
    )(dtab, proj4, proj4)


def _dft_stage2_kernel(c1_ref, s1_ref, yra_ref, yia_ref, yrb_ref, yib_ref, gm_ref, gp_ref,
                       wm_ref, wp_ref, ysr_ref, ysi_ref, gsp_ref, wsp_ref, x_ref, o_ref, acc_ref):
    c = pl.program_id(2)
    r = c1_ref.shape[0]
    c1 = c1_ref[...]
    s1 = s1_ref[...]

    def dot(m, y):
        return jnp.dot(m, y.astype(BF16), preferred_element_type=F32)

    @pl.when(c == 0)
    def _():
        parts = []
        for i in range(C_GRP):
            es = dot(c1, ysr_ref[0, i]) - dot(s1, ysi_ref[0, i])
            parts.append((es * _silu(gsp_ref[0, i].astype(F32))).astype(BF16))
        nyq = jnp.dot(jnp.concatenate(parts, axis=0), wsp_ref[...], preferred_element_type=F32)
        for i in range(C_GRP):
            acc_ref[i * r:(i + 1) * r, :] = x_ref[0, :, i, :] + nyq[i * r:(i + 1) * r, :]

    minus, plus = [], []
    for i in range(C_GRP):
        ea = dot(c1, yra_ref[0, i]) - dot(s1, yia_ref[0, i])
        eb = dot(c1, yib_ref[0, i]) + dot(s1, yrb_ref[0, i])
        minus.append(((ea - eb) * _silu(gm_ref[0, i].astype(F32))).astype(BF16))
        plus.append(((ea + eb) * _silu(gp_ref[0, i].astype(F32))).astype(BF16))
    acc_ref[...] += (jnp.dot(jnp.concatenate(minus, axis=0), wm_ref[...], preferred_element_type=F32)
                     + jnp.dot(jnp.concatenate(plus, axis=0), wp_ref[...], preferred_element_type=F32))

    @pl.when(c == pl.num_programs(2) - 1)
    def _():
        for i in range(C_GRP):
            o_ref[0, :, i, :] = acc_ref[i * r:(i + 1) * r, :]


def _dft_stage2(c1, s1, y_r, y_i, y_sr, y_si, proj4, w_out, x4):
    bsz, r, _, _ = x4.shape
    tc = 512
    nct = C_MAIN // tc
    mspec = pl.BlockSpec((r, r), lambda b, m, c: (0, 0))

    def slab(width, idx):
        return pl.BlockSpec((1, C_GRP, r, width), lambda b, m, c: (b, m, 0, idx(c)))

    def wspec(rows, idx):
        return pl.BlockSpec((rows, D_MODEL), lambda b, m, c: (idx(c), 0))

    xspec = pl.BlockSpec((1, r, C_GRP, D_MODEL), lambda b, m, c: (b, 0, m, 0))
    sp_blk = 4 * C_MAIN // LANES
    return pl.pallas_call(
        _dft_stage2_kernel,
        grid=(bsz, r // C_GRP, nct),
        in_specs=[
            mspec, mspec,
            slab(tc, lambda c: c), slab(tc, lambda c: c),
            slab(tc, lambda c: nct + c), slab(tc, lambda c: nct + c),
            slab(tc, lambda c: 2 * nct + c), slab(tc, lambda c: 3 * nct + c),
            wspec(tc, lambda c: c), wspec(tc, lambda c: nct + c),
            slab(LANES, lambda c: 0), slab(LANES, lambda c: 0),
            slab(LANES, lambda c: sp_blk + 1),
            wspec(LANES, lambda c: 2 * C_MAIN // LANES),
            xspec,
        ],
        out_specs=xspec,
        out_shape=jax.ShapeDtypeStruct(x4.shape, F32),
        scratch_shapes=[pltpu.VMEM((C_GRP * r, D_MODEL), F32)],
        compiler_params=_params(("parallel", "parallel", "arbitrary")),
        name="dft_stage2",
    )(c1, s1, y_r, y_i, y_r, y_i, proj4, proj4, w_out, w_out, y_sr, y_si, proj4, w_out, x4)


def _fourier_long(x2d, g, wc, w_out, bsz, seq):
    r = math.isqrt(seq)
    assert r * r == seq and r % C_GRP == 0
    x4 = x2d.reshape(bsz, r, r, D_MODEL)
    proj4 = _inproj_perm(x4, g, wc)
    idx = jnp.arange(r, dtype=jnp.int32)
    k2 = idx[None, :, None]
    ca, sa = _trig(k2 * idx[:, None, None], seq, r ** -0.5)
    cb, sb = _trig(k2 * (r * idx[None, None, :]), seq, 1.0)
    dc, ds = ca * cb - sa * sb, sa * cb + ca * sb
    dtab = jnp.concatenate([dc, ds], axis=1).astype(BF16)
    y_r, y_i, y_sr, y_si = _dft_stage1(proj4, dtab)
    c1, s1 = _trig(idx[:, None] * idx[None, :], r, r ** -0.5)
    out = _dft_stage2(c1.astype(BF16), s1.astype(BF16), y_r, y_i, y_sr, y_si, proj4, w_out, x4)
    return out.reshape(bsz * seq, D_MODEL)


OUT_TM = 512


def _outproj_kernel(mm_ref, mp_ref, ms_ref, x_ref, w_ref, o_ref):
    acc = jnp.dot(mm_ref[...], w_ref[:C_MAIN, :], preferred_element_type=F32)
    acc = acc + jnp.dot(mp_ref[...], w_ref[C_MAIN:2 * C_MAIN, :], preferred_element_type=F32)
    acc = acc + jnp.dot(ms_ref[...], w_ref[2 * C_MAIN:, :], preferred_element_type=F32)
    o_ref[...] = x_ref[...] + acc


def _outproj(minus2d, plus2d, nyq2d, x2d, w_out):
    t = x2d.shape[0]
    return pl.pallas_call(
        _outproj_kernel,
        grid=(t // OUT_TM,),
        in_specs=[
            pl.BlockSpec((OUT_TM, C_MAIN), lambda i: (i, 0)),
            pl.BlockSpec((OUT_TM, C_MAIN), lambda i: (i, 0)),
            pl.BlockSpec((OUT_TM, LANES), lambda i: (i, 0)),
            pl.BlockSpec((OUT_TM, D_MODEL), lambda i: (i, 0)),
            pl.BlockSpec((2 * C_MAIN + LANES, D_MODEL), lambda i: (0, 0)),
        ],
        out_specs=pl.BlockSpec((OUT_TM, D_MODEL), lambda i: (i, 0)),
        out_shape=jax.ShapeDtypeStruct((t, D_MODEL), F32),
        compiler_params=_params(("parallel",)),
        name="outproj",
    )(minus2d, plus2d, nyq2d, x2d, w_out)


def _rope_tables_a(seq):
    half = A_ROT // 2
    inv_freq = jnp.exp(-(jnp.arange(half, dtype=F32) * (2.0 / A_ROT)) * math.log(A_THETA))
    ang = jnp.arange(seq, dtype=F32)[:, None] * inv_freq[None, :]
    cos, sin = jnp.cos(ang), jnp.sin(ang)
    d = jnp.arange(LANES) % A_HEAD
    f = d % half
    cm = jnp.where(d[None, :] < A_ROT, cos[:, f], 1.0)
    s1 = jnp.where(d[None, :] < half, -sin[:, f], 0.0)
    s2 = jnp.where((d[None, :] >= half) & (d[None, :] < A_ROT), sin[:, f], 0.0)
    qs = A_HEAD ** -0.5 * LOG2E
    return jnp.stack([cm * qs, s1 * qs, s2 * qs, cm, s1, s2]).astype(F32)


def _rope_tables_b(seq):
    half = B_QK // 2
    inv_freq = jnp.exp(-(jnp.arange(half, dtype=F32) * (2.0 / B_QK)) * math.log(B_THETA))
    ang = jnp.arange(seq, dtype=F32)[:, None] * inv_freq[None, :]
    cos, sin = jnp.cos(ang), jnp.sin(ang)
    ks = B_QK ** -0.5
    return jnp.stack([cos, sin, cos * ks, sin * ks]).astype(F32)


def _weights_a(w_in):
    nq = A_QH * A_HEAD
    nkv = A_KVH * A_HEAD
    q = w_in[:, :nq]
    kv = w_in[:, nq:nq + 2 * nkv]
    gate = w_in[:, nq + 2 * nkv:]
    return jnp.concatenate([q, gate, kv], axis=1).astype(BF16)


def _c_index():
    g = jnp.arange(C_GROUPS, dtype=jnp.int32)[:, None] * C_GDIM
    m = jnp.arange(C_HALF, dtype=jnp.int32)[None, :]
    minus = (g + m).reshape(-1)
    plus = (g + jnp.where(m == 0, 0, C_GDIM - m)).reshape(-1)
    nyq = (g + C_HALF).reshape(-1)
    return minus, plus, nyq, jnp.tile(m == 0, (C_GROUPS, 1)).reshape(-1)


def _weights_c(w_in):
    ch = jnp.arange(C_GDIM, dtype=jnp.int32)
    m = jnp.arange(C_HALF, dtype=jnp.int32)
    cm, sm = _trig(ch[:, None] * m[None, :], C_GDIM, C_GDIM ** -0.5)
    nyq, _ = _trig(ch * C_HALF, C_GDIM, C_GDIM ** -0.5)
    lane = jnp.arange(LANES)[None, None, :] == jnp.arange(C_GROUPS)[:, None, None]
    nyq = jnp.where(lane, nyq[None, :, None], 0.0)
    cs = jnp.concatenate([jnp.broadcast_to(cm, (C_GROUPS,) + cm.shape),
                          jnp.broadcast_to(sm, (C_GROUPS,) + sm.shape), nyq], axis=2)
    w_a, w_b, w_sp = _fold_channel_dft(w_in[:, :BRANCH], cs)
    minus, plus, nyq_idx, _ = _c_index()
    gate = w_in[:, BRANCH:]
    g_sp = jnp.zeros((D_MODEL, LANES), F32).at[:, :C_GROUPS].set(gate[:, nyq_idx])
    return jnp.concatenate([w_a, w_b, gate[:, minus].astype(BF16), gate[:, plus].astype(BF16),
                            w_sp.astype(BF16), g_sp.astype(BF16)], axis=1)


def _weights_c_out(w_out):
    minus, plus, nyq_idx, dup = _c_index()
    w_plus = jnp.where(dup[:, None], 0.0, w_out[plus])
    w_sp = jnp.zeros((LANES, D_MODEL), F32).at[:C_GROUPS].set(w_out[nyq_idx])
    return jnp.concatenate([w_out[minus], w_plus, w_sp], axis=0).astype(BF16)


def _trunk(x, norm_g, fin_g, wa, a_sink, a_w_out, wb, log_g, b_w_out, wc, c_w_out):
    bsz, seq, _ = x.shape
    t = bsz * seq
    x2d = x.reshape(t, D_MODEL)
    tabs_a = _rope_tables_a(seq)
    tabs_b = _rope_tables_b(seq)
    fin = fin_g.reshape(1, D_MODEL)

    def layer_a(x2d, layer, j, final):
        q3, gate3, kk, vt = _inproj_a(x2d, norm_g[layer].reshape(1, D_MODEL), wa[j], tabs_a, bsz, seq)
        return _attention(x2d, q3, gate3, kk, vt, a_sink[j], a_w_out[j], fin, bsz, seq, final)

    x2d = layer_a(x2d, 0, 0, False)

    proj = _inproj(x2d, norm_g[1].reshape(1, D_MODEL), wb, tabs_b, seq, 2 * B_HEADS)
    x2d = _retention(proj.reshape(bsz, seq, -1), log_g, x2d.reshape(bsz, seq, D_MODEL),
                     b_w_out).reshape(t, D_MODEL)

    if seq <= C_DIRECT_MAX:
        proj = _inproj(x2d, norm_g[2].reshape(1, D_MODEL), wc, None, seq, 0)
        minus, plus, nyq = _fourier_mix(proj.reshape(bsz, seq, -1))
        x2d = _outproj(minus.reshape(t, C_MAIN), plus.reshape(t, C_MAIN), nyq.reshape(t, LANES),
                       x2d, c_w_out)
    else:
        x2d = _fourier_long(x2d, norm_g[2].reshape(1, D_MODEL), wc, c_w_out, bsz, seq)

    x2d = layer_a(x2d, 3, 1, True)
    return x2d.reshape(bsz, seq, D_MODEL)


def kernel(x_prompt, x_sample, norm_g, final_norm_g, a_w_in, a_sink, a_w_out, b_w_in, b_decay,
           b_w_out, c_w_in, c_w_out):
    wa = [_weights_a(a_w_in[j]) for j in range(a_w_in.shape[0])]
    a_out = [a_w_out[j].astype(BF16) for j in range(a_w_out.shape[0])]
    wb = b_w_in[0].astype(BF16)
    log_g = jax.nn.log_sigmoid(b_decay[0].astype(F32))
    wc = _weights_c(c_w_in[0])
    args = (norm_g, final_norm_g, wa, a_sink, a_out, wb, log_g, b_w_out[0].astype(BF16),
            wc, _weights_c_out(c_w_out[0]))
    return (_trunk(x_prompt, *args), _trunk(x_sample, *args))
```

```python
import functools
import math

import jax
import jax.numpy as jnp
from jax import lax
from jax.experimental import pallas as pl
from jax.experimental.pallas import tpu as pltpu

D_MODEL = 1024
BRANCH = 2048
NORM_EPS = 1e-6

A_HEAD = 64
A_QH = 32
A_KVH = 4
A_ROT = 16
A_THETA = 500000.0
A_WIN = 128
A_QBLK = 1024
A_SPAN = A_QBLK + 2 * A_WIN
A_PAIRS = A_QH // 2
A_LOOKAHEAD = 8
A_OUT_GROUPS = 2
A_VPAD = 16
LOG2E = math.log2(math.e)

B_QK = 256
B_HEADS = 4
B_V = 512
B_CHUNK = 256
B_ROWS = 2
B_THETA = 10000.0

C_GROUPS = 4
C_GDIM = 512
C_DIRECT_MAX = 2048

LANES = 128
VMEM_LIMIT = 56 * 1024 * 1024

BF16 = jnp.bfloat16
F32 = jnp.float32


def _params(sem):
    return pltpu.CompilerParams(dimension_semantics=sem, vmem_limit_bytes=VMEM_LIMIT)


def _silu(x):
    return x * jax.nn.sigmoid(x)


def _rms(x, g):
    ms = jnp.mean(x * x, axis=-1, keepdims=True)
    return (x * lax.rsqrt(ms + NORM_EPS)) * g


IN_TM = 1024
IN_TN = 512


def _inproj_a_kernel(x_ref, g_ref, w_ref, tab_ref, q_ref, gate_ref, kk_ref, vt_ref):
    h = _rms(x_ref[...], g_ref[...]).astype(BF16)

    def rope(a, t0):
        return (a * tab_ref[t0] + pltpu.roll(a, LANES - A_ROT // 2, 1) * tab_ref[t0 + 1]
                + pltpu.roll(a, A_ROT // 2, 1) * tab_ref[t0 + 2])

    for c in range(4):
        acc = jnp.dot(h, w_ref[:, c * IN_TN:(c + 1) * IN_TN], preferred_element_type=F32)
        for s in range(4):
            q_ref[4 * c + s] = rope(acc[:, s * LANES:(s + 1) * LANES], 0).astype(BF16)
    for c in range(4):
        acc = jnp.dot(h, w_ref[:, BRANCH + c * IN_TN:BRANCH + (c + 1) * IN_TN],
                      preferred_element_type=F32)
        for s in range(4):
            gate_ref[4 * c + s] = acc[:, s * LANES:(s + 1) * LANES].astype(BF16)
    acc = jnp.dot(h, w_ref[:, 2 * BRANCH:], preferred_element_type=F32)
    nkv = A_KVH * A_HEAD
    lane = lax.broadcasted_iota(jnp.int32, (IN_TM, LANES), 1)
    for p in range(nkv // LANES):
        a = rope(acc[:, p * LANES:(p + 1) * LANES], 3)
        swapped = pltpu.roll(a, A_HEAD, 1)
        kk_ref[2 * p] = jnp.where(lane < A_HEAD, a, swapped).astype(BF16)
        kk_ref[2 * p + 1] = jnp.where(lane >= A_HEAD, a, swapped).astype(BF16)
    vt_ref[0] = acc[:, nkv:].T.astype(BF16)


def _inproj_a(x2d, g, w, tabs, bsz, seq):
    t = x2d.shape[0]
    nblk = seq // IN_TM
    n = w.shape[1]
    return pl.pallas_call(
        _inproj_a_kernel,
        grid=(t // IN_TM,),
        in_specs=[
            pl.BlockSpec((IN_TM, D_MODEL), lambda i: (i, 0)),
            pl.BlockSpec((1, D_MODEL), lambda i: (0, 0)),
            pl.BlockSpec((D_MODEL, n), lambda i: (0, 0)),
            pl.BlockSpec((6, IN_TM, LANES), lambda i: (0, i % nblk, 0)),
        ],
        out_specs=[
            pl.BlockSpec((A_PAIRS, IN_TM, LANES), lambda i: (0, i, 0)),
            pl.BlockSpec((A_PAIRS, IN_TM, LANES), lambda i: (0, i, 0)),
            pl.BlockSpec((A_KVH, IN_TM, LANES), lambda i: (0, i, 0)),
            pl.BlockSpec((1, A_KVH * A_HEAD, IN_TM), lambda i: (i // nblk, 0, i % nblk)),
        ],
        out_shape=[
            jax.ShapeDtypeStruct((A_PAIRS, t, LANES), BF16),
            jax.ShapeDtypeStruct((A_PAIRS, t, LANES), BF16),
            jax.ShapeDtypeStruct((A_KVH, t, LANES), BF16),
            jax.ShapeDtypeStruct((bsz, A_KVH * A_HEAD, seq), BF16),
        ],
        compiler_params=_params(("parallel",)),
        name="inproj_a",
    )(x2d, g, w, tabs)


def _inproj_kernel(x_ref, g_ref, w_ref, *rest, rope_heads):
    if rope_heads:
        tab_ref, o_ref = rest
    else:
        (o_ref,) = rest
    h = _rms(x_ref[...], g_ref[...]).astype(BF16)
    n = w_ref.shape[1]
    for c0 in range(0, n, IN_TN):
        width = min(IN_TN, n - c0)
        acc = jnp.dot(h, w_ref[:, c0:c0 + width], preferred_element_type=F32)
        for lo in range(0, width, B_QK):
            head = (c0 + lo) // B_QK
            if head < rope_heads:
                t0 = 0 if head < rope_heads // 2 else 2
                cos, sin = tab_ref[t0], tab_ref[t0 + 1]
                x1 = acc[:, lo:lo + LANES]
                x2 = acc[:, lo + LANES:lo + B_QK]
                o_ref[:, c0 + lo:c0 + lo + LANES] = (x1 * cos - x2 * sin).astype(BF16)
                o_ref[:, c0 + lo + LANES:c0 + lo + B_QK] = (x2 * cos + x1 * sin).astype(BF16)
            else:
                o_ref[:, c0 + lo:c0 + lo + B_QK] = acc[:, lo:lo + B_QK].astype(BF16)


def _inproj(x2d, g, w, tabs, seq, rope_heads):
    t = x2d.shape[0]
    n = w.shape[1]
    nblk = seq // IN_TM
    in_specs = [
        pl.BlockSpec((IN_TM, D_MODEL), lambda i: (i, 0)),
        pl.BlockSpec((1, D_MODEL), lambda i: (0, 0)),
        pl.BlockSpec((D_MODEL, n), lambda i: (0, 0), pipeline_mode=pl.Buffered(1)),
    ]
    args = [x2d, g, w]
    if rope_heads:
        in_specs.append(pl.BlockSpec((4, IN_TM, LANES), lambda i: (0, i % nblk, 0)))
        args.append(tabs)
    return pl.pallas_call(
        functools.partial(_inproj_kernel, rope_heads=rope_heads),
        grid=(t // IN_TM,),
        in_specs=in_specs,
        out_specs=pl.BlockSpec((IN_TM, n), lambda i: (i, 0)),
        out_shape=jax.ShapeDtypeStruct((t, n), BF16),
        compiler_params=_params(("parallel",)),
        name="inproj",
    )(*args)


def _attn_kernel(sink_ref, q_ref, gate_ref, kp_ref, kc_ref, kn_ref, vp_ref, vc_ref, vn_ref,
                 x_ref, w_ref, fg_ref, o_ref, k_ref, v_ref, bias_ref, *, final):
    i = pl.program_id(1)
    last = pl.num_programs(1) - 1
    rows = 3 * A_WIN
    nslice = A_QBLK // A_WIN
    per_kv = A_PAIRS // A_KVH

    ones_row = (lax.broadcasted_iota(jnp.int32, (A_VPAD, A_SPAN), 0) == 0).astype(BF16)
    for h in range(A_KVH):
        k_ref[h] = jnp.concatenate([kp_ref[h], kc_ref[h], kn_ref[h]], axis=0)
        v_ref[h, :A_HEAD, :] = jnp.concatenate([vp_ref[0, h], vc_ref[0, h], vn_ref[0, h]], axis=1)
        v_ref[h, A_HEAD:, :] = ones_row

    r = lax.broadcasted_iota(jnp.int32, (A_WIN, LANES), 0)
    c = lax.broadcasted_iota(jnp.int32, (A_WIN, LANES), 1)
    neg = jnp.full((A_WIN, LANES), -1e30, F32)
    top = jnp.where(r >= c, 0.0, neg)
    bot = jnp.where(r <= c, 0.0, neg)
    bias_ref[0] = jnp.where(i == 0, neg, top)
    bias_ref[1] = top
    bias_ref[2] = bot
    bias_ref[3] = jnp.where(i == last, neg, bot)
    qlane = lax.broadcasted_iota(jnp.int32, (LANES, LANES), 1)

    units = [(g, jl, jj) for g in range(A_KVH) for jl in range(per_kv) for jj in range(nslice)]

    def scores(n):
        g, jl, jj = units[n]
        q = q_ref[g * per_kv + jl, jj * LANES:(jj + 1) * LANES, :]
        zero = jnp.zeros_like(q)
        rhs = jnp.concatenate([jnp.where(qlane < A_HEAD, q, zero),
                               jnp.where(qlane >= A_HEAD, q, zero)], axis=0)
        return lax.dot_general(k_ref[g, jj * A_WIN:jj * A_WIN + rows, :], rhs,
                               (((1,), (1,)), ((), ())), preferred_element_type=F32)

    def softmax_pv(s_t, sinks, g, jj):
        ps, extra = [], []
        for a in range(2):
            blk = s_t[:, a * LANES:(a + 1) * LANES]
            parts = [blk[:A_WIN] + bias_ref[0 if jj == 0 else 1], blk[A_WIN:2 * A_WIN],
                     blk[2 * A_WIN:] + bias_ref[3 if jj == nslice - 1 else 2]]
            mx = jnp.full((1, LANES), sinks[a], F32)
            for part in parts:
                mx = jnp.maximum(mx, jnp.max(part, axis=0, keepdims=True))
            ps.append(jnp.concatenate([jnp.exp2(part - mx).astype(BF16) for part in parts], axis=0))
            extra.append(jnp.exp2(sinks[a] - mx))
        o_ext = jnp.dot(v_ref[g, :, jj * A_WIN:jj * A_WIN + rows], jnp.concatenate(ps, axis=1),
                        preferred_element_type=F32)
        den = o_ext[A_HEAD:A_HEAD + 1, :] + jnp.concatenate(extra, axis=1)
        return o_ext[:A_HEAD, :] / den

    pending = [scores(n) for n in range(A_LOOKAHEAD)]
    outs, gated = [], []
    y = x_ref[...]
    for n, (g, jl, jj) in enumerate(units):
        if n + A_LOOKAHEAD < len(units):
            pending.append(scores(n + A_LOOKAHEAD))
        j = g * per_kv + jl
        sinks = (sink_ref[2 * j] * LOG2E, sink_ref[2 * j + 1] * LOG2E)
        outs.append(softmax_pv(pending.pop(0), sinks, g, jj))
        if jj == nslice - 1:
            last_outs = outs[-nslice:]
            o_pair = jnp.concatenate(
                [jnp.concatenate([o[:, :LANES] for o in last_outs], axis=1),
                 jnp.concatenate([o[:, LANES:] for o in last_outs], axis=1)], axis=0).T
            gt = gate_ref[j].astype(F32)
            gated.append((o_pair * _silu(gt)).astype(BF16))
            if jl == per_kv - 1 and (g + 1) % A_OUT_GROUPS == 0:
                npair = per_kv * A_OUT_GROUPS
                wg = w_ref[(j + 1 - npair) * LANES:(j + 1) * LANES, :]
                y = y + jnp.dot(jnp.concatenate(gated[-npair:], axis=1), wg,
                                preferred_element_type=F32)
    if final:
        y = _rms(y, fg_ref[...])
    o_ref[...] = y


def _attention(x2d, q3, gate3, kk, vt, sink, w_out, fin_g, bsz, seq, final):
    t = x2d.shape[0]
    nqb = seq // A_QBLK
    nkb = seq // A_WIN
    nsl = A_QBLK // A_WIN
    vt4 = vt.reshape(bsz, A_KVH, A_HEAD, seq)
    grid_spec = pltpu.PrefetchScalarGridSpec(
        num_scalar_prefetch=1,
        grid=(bsz, nqb),
        in_specs=[
            pl.BlockSpec((A_PAIRS, A_QBLK, LANES), lambda b, i, s: (0, b * nqb + i, 0)),
            pl.BlockSpec((A_PAIRS, A_QBLK, LANES), lambda b, i, s: (0, b * nqb + i, 0)),
            pl.BlockSpec((A_KVH, A_WIN, LANES),
                         lambda b, i, s: (0, b * nkb + jnp.maximum(nsl * i - 1, 0), 0)),
            pl.BlockSpec((A_KVH, A_QBLK, LANES), lambda b, i, s: (0, b * nqb + i, 0)),
            pl.BlockSpec((A_KVH, A_WIN, LANES),
                         lambda b, i, s: (0, b * nkb + jnp.minimum(nsl * i + nsl, nkb - 1), 0)),
            pl.BlockSpec((1, A_KVH, A_HEAD, A_WIN),
                         lambda b, i, s: (b, 0, 0, jnp.maximum(nsl * i - 1, 0))),
            pl.BlockSpec((1, A_KVH, A_HEAD, A_QBLK), lambda b, i, s: (b, 0, 0, i)),
            pl.BlockSpec((1, A_KVH, A_HEAD, A_WIN),
                         lambda b, i, s: (b, 0, 0, jnp.minimum(nsl * i + nsl, nkb - 1))),
            pl.BlockSpec((A_QBLK, D_MODEL), lambda b, i, s: (b * nqb + i, 0)),
            pl.BlockSpec((BRANCH, D_MODEL), lambda b, i, s: (0, 0)),
            pl.BlockSpec((1, D_MODEL), lambda b, i, s: (0, 0)),
        ],
        out_specs=pl.BlockSpec((A_QBLK, D_MODEL), lambda b, i, s: (b * nqb + i, 0)),
        scratch_shapes=[
            pltpu.VMEM((A_KVH, A_SPAN, LANES), BF16),
            pltpu.VMEM((A_KVH, A_HEAD + A_VPAD, A_SPAN), BF16),
            pltpu.VMEM((4, A_WIN, LANES), F32),
        ],
    )
    return pl.pallas_call(
        functools.partial(_attn_kernel, final=final),
        grid_spec=grid_spec,
        out_shape=jax.ShapeDtypeStruct((t, D_MODEL), F32),
        compiler_params=_params(("parallel", "parallel")),
        name="attention",
    )(sink, q3, gate3, kk, kk, kk, vt4, vt4, vt4, x2d, w_out, fin_g)


def _retention_start(lg_ref, st_ref, intra_ref, dec_ref, backward):
    n = B_CHUNK
    ii = lax.broadcasted_iota(jnp.int32, (n, n), 0)
    jj = lax.broadcasted_iota(jnp.int32, (n, n), 1)
    col = lax.broadcasted_iota(jnp.int32, (n, LANES), 0).astype(F32)
    st_ref[...] = jnp.zeros_like(st_ref)
    for h in range(B_HEADS):
        lg = lg_ref[1 if backward else 0, h]
        if backward:
            mask = jj > ii
            dist = (jj - ii).astype(F32)
            dec_ref[h, 0] = jnp.exp((n - col) * lg)
            dec_ref[h, 1] = jnp.exp(col * lg)
        else:
            mask = ii >= jj
            dist = (ii - jj).astype(F32)
            dec_ref[h, 0] = jnp.exp((col + 1.0) * lg)
            dec_ref[h, 1] = jnp.exp((n - 1.0 - col) * lg)
        intra_ref[h] = jnp.where(mask, jnp.exp(jnp.where(mask, dist, 0.0) * lg), 0.0)


def _retention_open(q_ref, k_ref, v_ref, row):
    heads = range(B_HEADS)
    qs = [q_ref[row, :, h * B_QK:(h + 1) * B_QK] for h in heads]
    ks = [k_ref[row, :, h * B_QK:(h + 1) * B_QK] for h in heads]
    vs = [v_ref[row, :, h * B_V:(h + 1) * B_V] for h in heads]
    sc = [lax.dot_general(qs[h], ks[h], (((1,), (1,)), ((), ())), preferred_element_type=F32)
          for h in heads]
    return qs, ks, vs, sc


def _retention_heads(lg_ref, opened, st_ref, intra_ref, dec_ref, row, backward, emit):
    qs, ks, vs, sc = opened
    for h in range(B_HEADS):
        q_dec = jnp.concatenate([dec_ref[h, 0]] * (B_QK // LANES), axis=1)
        k_dec = jnp.concatenate([dec_ref[h, 1]] * (B_QK // LANES), axis=1)
        c_dec = jnp.exp(jnp.full((1, B_V), B_CHUNK * lg_ref[1 if backward else 0, h], F32))
        st = st_ref[row, h]
        lhs = jnp.concatenate([(sc[h] * intra_ref[h]).astype(BF16),
                               (qs[h].astype(F32) * q_dec).astype(BF16)], axis=1)
        rhs = jnp.concatenate([vs[h], st.astype(BF16)], axis=0)
        out = jnp.dot(lhs, rhs, preferred_element_type=F32)
        kd_t = (ks[h].astype(F32) * k_dec).T.astype(BF16)
        st_ref[row, h] = st * c_dec + jnp.dot(kd_t, vs[h], preferred_element_type=F32)
        emit(h, out)


def _retention_bwd_kernel(lg_ref, q_ref, k_ref, v_ref, o_ref, st_ref, intra_ref, dec_ref):
    @pl.when(pl.program_id(1) == 0)
    def _():
        _retention_start(lg_ref, st_ref, intra_ref, dec_ref, True)

    rows = range(q_ref.shape[0])
    opened = [_retention_open(q_ref, k_ref, v_ref, row) for row in rows]
    for row in rows:
        def emit(h, o, row=row):
            o_ref[row, :, h * B_V:(h + 1) * B_V] = o.astype(BF16)

        _retention_heads(lg_ref, opened[row], st_ref, intra_ref, dec_ref, row, True, emit)


def _retention_fwd_kernel(lg_ref, q_ref, k_ref, v_ref, gate_ref, ob_ref, x_ref, w_ref, o_ref,
                          st_ref, intra_ref, dec_ref):
    @pl.when(pl.program_id(1) == 0)
    def _():
        _retention_start(lg_ref, st_ref, intra_ref, dec_ref, False)

    rows = range(q_ref.shape[0])
    opened = [_retention_open(q_ref, k_ref, v_ref, row) for row in rows]
    gated = []
    for row in rows:
        parts = []

        def emit(h, o_f, row=row, parts=parts):
            sl = slice(h * B_V, (h + 1) * B_V)
            o = o_f + ob_ref[row, :, sl].astype(F32)
            on = o * lax.rsqrt(jnp.mean(o * o, axis=-1, keepdims=True) + NORM_EPS)
            parts.append((on * _silu(gate_ref[row, :, sl].astype(F32))).astype(BF16))

        _retention_heads(lg_ref, opened[row], st_ref, intra_ref, dec_ref, row, False, emit)
        gated.append(jnp.concatenate(parts, axis=1))
    for row in rows:
        o_ref[row] = x_ref[row] + jnp.dot(gated[row], w_ref[...], preferred_element_type=F32)


def _retention(proj3, log_g, x3, w_out):
    bsz, seq, _ = proj3.shape
    nc = seq // B_CHUNK
    nqk = B_HEADS * B_QK
    rows = B_ROWS if bsz % B_ROWS == 0 else 1
    smem = pl.BlockSpec(memory_space=pltpu.SMEM)
    state = [pltpu.VMEM((rows, B_HEADS, B_QK, B_V), F32),
             pltpu.VMEM((B_HEADS, B_CHUNK, B_CHUNK), F32),
             pltpu.VMEM((B_HEADS, 2, B_CHUNK, LANES), F32)]

    def blk(width, chunk, col):
        return pl.BlockSpec((rows, B_CHUNK, width), lambda b, t: (b, chunk(t), col))

    def rev(t):
        return nc - 1 - t

    def fwd(t):
        return t

    o_b = pl.pallas_call(
        _retention_bwd_kernel,
        grid=(bsz // rows, nc),
        in_specs=[smem, blk(nqk, rev, 0), blk(nqk, rev, 1), blk(BRANCH, rev, 1)],
        out_specs=blk(BRANCH, rev, 0),
        out_shape=jax.ShapeDtypeStruct((bsz, seq, BRANCH), BF16),
        scratch_shapes=state,
        compiler_params=_params(("parallel", "arbitrary")),
        name="retention_bwd",
    )(log_g, proj3, proj3, proj3)
    return pl.pallas_call(
        _retention_fwd_kernel,
        grid=(bsz // rows, nc),
        in_specs=[smem, blk(nqk, fwd, 0), blk(nqk, fwd, 1), blk(BRANCH, fwd, 1),
                  blk(BRANCH, fwd, 2), blk(BRANCH, fwd, 0), blk(D_MODEL, fwd, 0),
                  pl.BlockSpec((BRANCH, D_MODEL), lambda b, t: (0, 0))],
        out_specs=blk(D_MODEL, fwd, 0),
        out_shape=jax.ShapeDtypeStruct((bsz, seq, D_MODEL), F32),
        scratch_shapes=state,
        compiler_params=_params(("parallel", "arbitrary")),
        name="retention_fwd",
    )(log_g, proj3, proj3, proj3, proj3, o_b, x3, w_out)


C_HALF = C_GDIM // 2
C_MAIN = C_GROUPS * C_HALF
C_FOLD = 2 * C_HALF + LANES
C_NYQ_ROWS = 16


def _fold_kernel(w_ref, cs_ref, a_ref, b_ref, sp_ref):
    r = jnp.dot(w_ref[...], cs_ref[0], preferred_element_type=F32,
                precision=lax.Precision.HIGHEST)
    a_ref[...] = r[:, :C_HALF].astype(BF16)
    b_ref[...] = r[:, C_HALF:2 * C_HALF].astype(BF16)

    @pl.when(pl.program_id(0) == 0)
    def _():
        sp_ref[...] = jnp.zeros_like(sp_ref)

    sp_ref[...] += r[:, 2 * C_HALF:]


def _fold_channel_dft(w_u, cs):
    spec = pl.BlockSpec((D_MODEL, C_HALF), lambda g: (0, g))
    return pl.pallas_call(
        _fold_kernel,
        grid=(C_GROUPS,),
        in_specs=[pl.BlockSpec((D_MODEL, C_GDIM), lambda g: (0, g)),
                  pl.BlockSpec((1, C_GDIM, C_FOLD), lambda g: (g, 0, 0))],
        out_specs=[spec, spec, pl.BlockSpec((D_MODEL, LANES), lambda g: (0, 0))],
        out_shape=[jax.ShapeDtypeStruct((D_MODEL, C_MAIN), BF16)] * 2
        + [jax.ShapeDtypeStruct((D_MODEL, LANES), F32)],
        compiler_params=_params(("arbitrary",)),
        name="fold_channel_dft",
    )(w_u, cs)


DFT_ROWS = 512


def _dft_mix_kernel(cm_ref, sm_ref, a_ref, b_ref, gm_ref, gp_ref, sp_ref, gsp_ref,
                    om_ref, op_ref, osp_ref):
    n = cm_ref.shape[0]
    rc = min(n, DFT_ROWS)
    a = a_ref[0]
    b = b_ref[0]
    for r in range(n // rc):
        rows = slice(r * rc, (r + 1) * rc)
        ea = jnp.dot(cm_ref[rows, :], a, preferred_element_type=F32)
        eb = jnp.dot(sm_ref[rows, :], b, preferred_element_type=F32)
        om_ref[0, rows, :] = ((ea - eb) * _silu(gm_ref[0, rows, :].astype(F32))).astype(BF16)
        op_ref[0, rows, :] = ((ea + eb) * _silu(gp_ref[0, rows, :].astype(F32))).astype(BF16)

    @pl.when(pl.program_id(1) == 0)
    def _():
        sp_t = sp_ref[0].astype(F32).T[:C_NYQ_ROWS, :].astype(BF16)
        es_t = jnp.dot(sp_t, cm_ref[...], preferred_element_type=F32)
        es = jnp.concatenate([es_t, jnp.zeros((LANES - C_NYQ_ROWS, n), F32)], axis=0).T
        osp_ref[0] = (es * _silu(gsp_ref[0].astype(F32))).astype(BF16)


def _trig(m, period, scale):
    ang = (2.0 * math.pi / period) * (m % period).astype(F32)
    return jnp.cos(ang) * scale, jnp.sin(ang) * scale


def _fourier_mix(proj3):
    bsz, seq, _ = proj3.shape
    n = jnp.arange(seq, dtype=jnp.int32)[None, None, :]
    hi = jnp.arange(seq // LANES, dtype=jnp.int32)[:, None, None] * LANES
    lo = jnp.arange(LANES, dtype=jnp.int32)[None, :, None]
    ca, sa = _trig(hi * n, seq, seq ** -0.5)
    cb, sb = _trig(lo * n, seq, 1.0)
    cm = (ca * cb - sa * sb).reshape(seq, seq)
    sm = (sa * cb + ca * sb).reshape(seq, seq)
    tc = 512
    nct = C_MAIN // tc
    mspec = pl.BlockSpec((seq, seq), lambda b, c: (0, 0))

    def dspec(width, idx):
        return pl.BlockSpec((1, seq, width), lambda b, c: (b, 0, idx(c)))

    sp_blk = 4 * C_MAIN // LANES
    return pl.pallas_call(
        _dft_mix_kernel,
        grid=(bsz, nct),
        in_specs=[mspec, mspec,
                  dspec(tc, lambda c: c), dspec(tc, lambda c: nct + c),
                  dspec(tc, lambda c: 2 * nct + c), dspec(tc, lambda c: 3 * nct + c),
                  dspec(LANES, lambda c: sp_blk), dspec(LANES, lambda c: sp_blk + 1)],
        out_specs=[dspec(tc, lambda c: c), dspec(tc, lambda c: c), dspec(LANES, lambda c: 0)],
        out_shape=[jax.ShapeDtypeStruct((bsz, seq, C_MAIN), BF16)] * 2
        + [jax.ShapeDtypeStruct((bsz, seq, LANES), BF16)],
        compiler_params=_params(("parallel", "arbitrary")),
        name="dft_mix",
    )(cm.astype(BF16), sm.astype(BF16), proj3, proj3, proj3, proj3, proj3, proj3)


C_GRP = 8
C_SUB = 64
C_ROWS = 256
C_NCHUNK = 1024


def _inproj_perm_kernel(x_ref, g_ref, w_ref, o_ref, xs_ref):
    rh = x_ref.shape[1]
    n = w_ref.shape[1]
    for i in range(C_GRP):
        xs_ref[i * rh:(i + 1) * rh, :] = x_ref[0, :, i, :]
    per_dot = C_ROWS // rh
    for j in range(C_GRP // per_dot):
        h = _rms(xs_ref[j * C_ROWS:(j + 1) * C_ROWS, :], g_ref[...]).astype(BF16)
        for c0 in range(0, n, C_NCHUNK):
            cols = slice(c0, min(c0 + C_NCHUNK, n))
            acc = jnp.dot(h, w_ref[:, cols], preferred_element_type=F32)
            for i in range(per_dot):
                o_ref[0, j * per_dot + i, :, cols] = acc[i * rh:(i + 1) * rh, :].astype(BF16)


def _inproj_perm(x4, g, w):
    bsz, r, _, _ = x4.shape
    n = w.shape[1]
    rh = min(r, C_SUB)
    return pl.pallas_call(
        _inproj_perm_kernel,
        grid=(bsz, r // C_GRP, r // rh),
        in_specs=[
            pl.BlockSpec((1, rh, C_GRP, D_MODEL), lambda b, m, s: (b, s, m, 0)),
            pl.BlockSpec((1, D_MODEL), lambda b, m, s: (0, 0)),
            pl.BlockSpec((D_MODEL, n), lambda b, m, s: (0, 0)),
        ],
        out_specs=pl.BlockSpec((1, C_GRP, rh, n), lambda b, m, s: (b, m, s, 0)),
        out_shape=jax.ShapeDtypeStruct((bsz, r, r, n), BF16),
        scratch_shapes=[pltpu.VMEM((C_GRP * rh, D_MODEL), F32)],
        compiler_params=_params(("parallel", "parallel", "parallel")),
        name="inproj_perm",
    )(x4, g, w)


def _dft_stage1_kernel(d_ref, x_ref, sp_ref, yr_ref, yi_ref, ysr_ref, ysi_ref):
    r = x_ref.shape[2]
    for i in range(C_GRP):
        dc = d_ref[i, :r, :]
        ds = d_ref[i, r:, :]
        yr_ref[0, :, i, :] = jnp.dot(dc, x_ref[0, i], preferred_element_type=F32)
        yi_ref[0, :, i, :] = jnp.dot(ds, x_ref[0, i], preferred_element_type=F32)
        ysr_ref[0, :, i, :] = jnp.dot(dc, sp_ref[0, i], preferred_element_type=F32)
        ysi_ref[0, :, i, :] = jnp.dot(ds, sp_ref[0, i], preferred_element_type=F32)


def _dft_stage1(proj4, dtab):
    bsz, r, _, _ = proj4.shape
    tc = 2 * C_MAIN
    nct = 2 * C_MAIN // tc

    def yspec(width, idx):
        return pl.BlockSpec((1, r, C_GRP, width), lambda b, m, c: (b, 0, m, idx(c)))

    return pl.pallas_call(
        _dft_stage1_kernel,
        grid=(bsz, r // C_GRP, nct),
        in_specs=[
            pl.BlockSpec((C_GRP, 2 * r, r), lambda b, m, c: (m, 0, 0)),
            pl.BlockSpec((1, C_GRP, r, tc), lambda b, m, c: (b, m, 0, c)),
            pl.BlockSpec((1, C_GRP, r, LANES), lambda b, m, c: (b, m, 0, 4 * C_MAIN // LANES)),
        ],
        out_specs=[yspec(tc, lambda c: c), yspec(tc, lambda c: c),
                   yspec(LANES, lambda c: 0), yspec(LANES, lambda c: 0)],
        out_shape=[jax.ShapeDtypeStruct((bsz, r, r, 2 * C_MAIN), F32)] * 2
        + [jax.ShapeDtypeStruct((bsz, r, r, LANES), F32)] * 2,
        compiler_params=_params(("parallel", "parallel", "arbitrary")),
        name="dft_stage1",
    )(dtab, proj4, proj4)


def _dft_stage2_kernel(c1_ref, s1_ref, yra_ref, yia_ref, yrb_ref, yib_ref, gm_ref, gp_ref,
                       wm_ref, wp_ref, ysr_ref, ysi_ref, gsp_ref, wsp_ref, x_ref, o_ref, acc_ref):
    c = pl.program_id(2)
    r = c1_ref.shape[0]
    c1 = c1_ref[...]
    s1 = s1_ref[...]

    def dot(m, y):
        return jnp.dot(m, y.astype(BF16), preferred_element_type=F32)

    @pl.when(c == 0)
    def _():
        parts = []
        for i in range(C_GRP):
            es = dot(c1, ysr_ref[0, i]) - dot(s1, ysi_ref[0, i])
            parts.append((es * _silu(gsp_ref[0, i].astype(F32))).astype(BF16))
        nyq = jnp.dot(jnp.concatenate(parts, axis=0), wsp_ref[...], preferred_element_type=F32)
        for i in range(C_GRP):
            acc_ref[i * r:(i + 1) * r, :] = x_ref[0, :, i, :] + nyq[i * r:(i + 1) * r, :]

    minus, plus = [], []
    for i in range(C_GRP):
        ea = dot(c1, yra_ref[0, i]) - dot(s1, yia_ref[0, i])
        eb = dot(c1, yib_ref[0, i]) + dot(s1, yrb_ref[0, i])
        minus.append(((ea - eb) * _silu(gm_ref[0, i].astype(F32))).astype(BF16))
        plus.append(((ea + eb) * _silu(gp_ref[0, i].astype(F32))).astype(BF16))
    acc_ref[...] += (jnp.dot(jnp.concatenate(minus, axis=0), wm_ref[...], preferred_element_type=F32)
                     + jnp.dot(jnp.concatenate(plus, axis=0), wp_ref[...], preferred_element_type=F32))

    @pl.when(c == pl.num_programs(2) - 1)
    def _():
        for i in range(C_GRP):
            o_ref[0, :, i, :] = acc_ref[i * r:(i + 1) * r, :]


def _dft_stage2(c1, s1, y_r, y_i, y_sr, y_si, proj4, w_out, x4):
    bsz, r, _, _ = x4.shape
    tc = 512
    nct = C_MAIN // tc
    mspec = pl.BlockSpec((r, r), lambda b, m, c: (0, 0))

    def slab(width, idx):
        return pl.BlockSpec((1, C_GRP, r, width), lambda b, m, c: (b, m, 0, idx(c)))

    def wspec(rows, idx):
        return pl.BlockSpec((rows, D_MODEL), lambda b, m, c: (idx(c), 0))

    xspec = pl.BlockSpec((1, r, C_GRP, D_MODEL), lambda b, m, c: (b, 0, m, 0))
    sp_blk = 4 * C_MAIN // LANES
    return pl.pallas_call(
        _dft_stage2_kernel,
        grid=(bsz, r // C_GRP, nct),
        in_specs=[
            mspec, mspec,
            slab(tc, lambda c: c), slab(tc, lambda c: c),
            slab(tc, lambda c: nct + c), slab(tc, lambda c: nct + c),
            slab(tc, lambda c: 2 * nct + c), slab(tc, lambda c: 3 * nct + c),
            wspec(tc, lambda c: c), wspec(tc, lambda c: nct + c),
            slab(LANES, lambda c: 0), slab(LANES, lambda c: 0),
            slab(LANES, lambda c: sp_blk + 1),
            wspec(LANES, lambda c: 2 * C_MAIN // LANES),
            xspec,
        ],
        out_specs=xspec,
        out_shape=jax.ShapeDtypeStruct(x4.shape, F32),
        scratch_shapes=[pltpu.VMEM((C_GRP * r, D_MODEL), F32)],
        compiler_params=_params(("parallel", "parallel", "arbitrary")),
        name="dft_stage2",
    )(c1, s1, y_r, y_i, y_r, y_i, proj4, proj4, w_out, w_out, y_sr, y_si, proj4, w_out, x4)


def _fourier_long(x2d, g, wc, w_out, bsz, seq):
    r = math.isqrt(seq)
    assert r * r == seq and r % C_GRP == 0
    x4 = x2d.reshape(bsz, r, r, D_MODEL)
    proj4 = _inproj_perm(x4, g, wc)
    idx = jnp.arange(r, dtype=jnp.int32)
    k2 = idx[None, :, None]
    ca, sa = _trig(k2 * idx[:, None, None], seq, r ** -0.5)
    cb, sb = _trig(k2 * (r * idx[None, None, :]), seq, 1.0)
    dc, ds = ca * cb - sa * sb, sa * cb + ca * sb
    dtab = jnp.concatenate([dc, ds], axis=1).astype(BF16)
    y_r, y_i, y_sr, y_si = _dft_stage1(proj4, dtab)
    c1, s1 = _trig(idx[:, None] * idx[None, :], r, r ** -0.5)
    out = _dft_stage2(c1.astype(BF16), s1.astype(BF16), y_r, y_i, y_sr, y_si, proj4, w_out, x4)
    return out.reshape(bsz * seq, D_MODEL)


OUT_TM = 512


def _outproj_kernel(mm_ref, mp_ref, ms_ref, x_ref, w_ref, o_ref):
    acc = jnp.dot(mm_ref[...], w_ref[:C_MAIN, :], preferred_element_type=F32)
    acc = acc + jnp.dot(mp_ref[...], w_ref[C_MAIN:2 * C_MAIN, :], preferred_element_type=F32)
    acc = acc + jnp.dot(ms_ref[...], w_ref[2 * C_MAIN:, :], preferred_element_type=F32)
    o_ref[...] = x_ref[...] + acc


def _outproj(minus2d, plus2d, nyq2d, x2d, w_out):
    t = x2d.shape[0]
    return pl.pallas_call(
        _outproj_kernel,
        grid=(t // OUT_TM,),
        in_specs=[
            pl.BlockSpec((OUT_TM, C_MAIN), lambda i: (i, 0)),
            pl.BlockSpec((OUT_TM, C_MAIN), lambda i: (i, 0)),
            pl.BlockSpec((OUT_TM, LANES), lambda i: (i, 0)),
            pl.BlockSpec((OUT_TM, D_MODEL), lambda i: (i, 0)),
            pl.BlockSpec((2 * C_MAIN + LANES, D_MODEL), lambda i: (0, 0)),
        ],
        out_specs=pl.BlockSpec((OUT_TM, D_MODEL), lambda i: (i, 0)),
        out_shape=jax.ShapeDtypeStruct((t, D_MODEL), F32),
        compiler_params=_params(("parallel",)),
        name="outproj",
    )(minus2d, plus2d, nyq2d, x2d, w_out)


def _rope_tables_a(seq):
    half = A_ROT // 2
    inv_freq = jnp.exp(-(jnp.arange(half, dtype=F32) * (2.0 / A_ROT)) * math.log(A_THETA))
    ang = jnp.arange(seq, dtype=F32)[:, None] * inv_freq[None, :]
    cos, sin = jnp.cos(ang), jnp.sin(ang)
    d = jnp.arange(LANES) % A_HEAD
    f = d % half
    cm = jnp.where(d[None, :] < A_ROT, cos[:, f], 1.0)
    s1 = jnp.where(d[None, :] < half, -sin[:, f], 0.0)
    s2 = jnp.where((d[None, :] >= half) & (d[None, :] < A_ROT), sin[:, f], 0.0)
    qs = A_HEAD ** -0.5 * LOG2E
    return jnp.stack([cm * qs, s1 * qs, s2 * qs, cm, s1, s2]).astype(F32)


def _rope_tables_b(seq):
    half = B_QK // 2
    inv_freq = jnp.exp(-(jnp.arange(half, dtype=F32) * (2.0 / B_QK)) * math.log(B_THETA))
    ang = jnp.arange(seq, dtype=F32)[:, None] * inv_freq[None, :]
    cos, sin = jnp.cos(ang), jnp.sin(ang)
    ks = B_QK ** -0.5
    return jnp.stack([cos, sin, cos * ks, sin * ks]).astype(F32)


def _weights_a(w_in):
    nq = A_QH * A_HEAD
    nkv = A_KVH * A_HEAD
    q = w_in[:, :nq]
    kv = w_in[:, nq:nq + 2 * nkv]
    gate = w_in[:, nq + 2 * nkv:]
    return jnp.concatenate([q, gate, kv], axis=1).astype(BF16)


def _c_index():
    g = jnp.arange(C_GROUPS, dtype=jnp.int32)[:, None] * C_GDIM
    m = jnp.arange(C_HALF, dtype=jnp.int32)[None, :]
    minus = (g + m).reshape(-1)
    plus = (g + jnp.where(m == 0, 0, C_GDIM - m)).reshape(-1)
    nyq = (g + C_HALF).reshape(-1)
    return minus, plus, nyq, jnp.tile(m == 0, (C_GROUPS, 1)).reshape(-1)


def _weights_c(w_in):
    ch = jnp.arange(C_GDIM, dtype=jnp.int32)
    m = jnp.arange(C_HALF, dtype=jnp.int32)
    cm, sm = _trig(ch[:, None] * m[None, :], C_GDIM, C_GDIM ** -0.5)
    nyq, _ = _trig(ch * C_HALF, C_GDIM, C_GDIM ** -0.5)
    lane = jnp.arange(LANES)[None, None, :] == jnp.arange(C_GROUPS)[:, None, None]
    nyq = jnp.where(lane, nyq[None, :, None], 0.0)
    cs = jnp.concatenate([jnp.broadcast_to(cm, (C_GROUPS,) + cm.shape),
                          jnp.broadcast_to(sm, (C_GROUPS,) + sm.shape), nyq], axis=2)
    w_a, w_b, w_sp = _fold_channel_dft(w_in[:, :BRANCH], cs)
    minus, plus, nyq_idx, _ = _c_index()
    gate = w_in[:, BRANCH:]
    g_sp = jnp.zeros((D_MODEL, LANES), F32).at[:, :C_GROUPS].set(gate[:, nyq_idx])
    return jnp.concatenate([w_a, w_b, gate[:, minus].astype(BF16), gate[:, plus].astype(BF16),
                            w_sp.astype(BF16), g_sp.astype(BF16)], axis=1)


def _weights_c_out(w_out):
    minus, plus, nyq_idx, dup = _c_index()
    w_plus = jnp.where(dup[:, None], 0.0, w_out[plus])
    w_sp = jnp.zeros((LANES, D_MODEL), F32).at[:C_GROUPS].set(w_out[nyq_idx])
    return jnp.concatenate([w_out[minus], w_plus, w_sp], axis=0).astype(BF16)


def _trunk(x, norm_g, fin_g, wa, a_sink, a_w_out, wb, log_g, b_w_out, wc, c_w_out):
    bsz, seq, _ = x.shape
    t = bsz * seq
    x2d = x.reshape(t, D_MODEL)
    tabs_a = _rope_tables_a(seq)
    tabs_b = _rope_tables_b(seq)
    fin = fin_g.reshape(1, D_MODEL)

    def layer_a(x2d, layer, j, final):
        q3, gate3, kk, vt = _inproj_a(x2d, norm_g[layer].reshape(1, D_MODEL), wa[j], tabs_a, bsz, seq)
        return _attention(x2d, q3, gate3, kk, vt, a_sink[j], a_w_out[j], fin, bsz, seq, final)

    x2d = layer_a(x2d, 0, 0, False)

    proj = _inproj(x2d, norm_g[1].reshape(1, D_MODEL), wb, tabs_b, seq, 2 * B_HEADS)
    x2d = _retention(proj.reshape(bsz, seq, -1), log_g, x2d.reshape(bsz, seq, D_MODEL),
                     b_w_out).reshape(t, D_MODEL)

    if seq <= C_DIRECT_MAX:
        proj = _inproj(x2d, norm_g[2].reshape(1, D_MODEL), wc, None, seq, 0)
        minus, plus, nyq = _fourier_mix(proj.reshape(bsz, seq, -1))
        x2d = _outproj(minus.reshape(t, C_MAIN), plus.reshape(t, C_MAIN), nyq.reshape(t, LANES),
                       x2d, c_w_out)
    else:
        x2d = _fourier_long(x2d, norm_g[2].reshape(1, D_MODEL), wc, c_w_out, bsz, seq)

    x2d = layer_a(x2d, 3, 1, True)
    return x2d.reshape(bsz, seq, D_MODEL)


def kernel(x_prompt, x_sample, norm_g, final_norm_g, a_w_in, a_sink, a_w_out, b_w_in, b_decay,
           b_w_out, c_w_in, c_w_out):
    wa = [_weights_a(a_w_in[j]) for j in range(a_w_in.shape[0])]
    a_out = [a_w_out[j].astype(BF16) for j in range(a_w_out.shape[0])]
    wb = b_w_in[0].astype(BF16)
    log_g = jax.nn.log_sigmoid(b_decay[0].astype(F32))
    wc = _weights_c(c_w_in[0])
    args = (norm_g, final_norm_g, wa, a_sink, a_out, wb, log_g, b_w_out[0].astype(BF16),
            wc, _weights_c_out(c_w_out[0]))
    return (_trunk(x_prompt, *args), _trunk(x_sample, *args))
```

```python
import functools
import math

import jax
import jax.numpy as jnp
from jax import lax
from jax.experimental import pallas as pl
from jax.experimental.pallas import tpu as pltpu

D_MODEL = 1024
BRANCH = 2048
NORM_EPS = 1e-6

A_HEAD = 64
A_QH = 32
A_KVH = 4
A_ROT = 16
A_THETA = 500000.0
A_WIN = 128
A_QBLK = 1024
A_SPAN = A_QBLK + 2 * A_WIN
A_PAIRS = A_QH // 2
A_LOOKAHEAD = 8
A_OUT_GROUPS = 2
A_VPAD = 16
LOG2E = math.log2(math.e)

B_QK = 256
B_HEADS = 4
B_V = 512
B_CHUNK = 256
B_ROWS = 2
B_THETA = 10000.0

C_GROUPS = 4
C_GDIM = 512
C_DIRECT_MAX = 2048

LANES = 128
VMEM_LIMIT = 56 * 1024 * 1024

BF16 = jnp.bfloat16
F32 = jnp.float32


def _params(sem):
    return pltpu.CompilerParams(dimension_semantics=sem, vmem_limit_bytes=VMEM_LIMIT)


def _silu(x):
    half = 0.5 * x
    return half + half * jnp.tanh(half)


def _rms(x, g):
    ms = jnp.mean(x * x, axis=-1, keepdims=True)
    return (x * lax.rsqrt(ms + NORM_EPS)) * g


IN_TM = 1024
IN_TN = 512


def _inproj_a_kernel(x_ref, g_ref, w_ref, tab_ref, q_ref, gate_ref, kk_ref, vt_ref):
    h = _rms(x_ref[...], g_ref[...]).astype(BF16)

    def rope(a, t0):
        return (a * tab_ref[t0] + pltpu.roll(a, LANES - A_ROT // 2, 1) * tab_ref[t0 + 1]
                + pltpu.roll(a, A_ROT // 2, 1) * tab_ref[t0 + 2])

    for c in range(4):
        acc = jnp.dot(h, w_ref[:, c * IN_TN:(c + 1) * IN_TN], preferred_element_type=F32)
        for s in range(4):
            q_ref[4 * c + s] = rope(acc[:, s * LANES:(s + 1) * LANES], 0).astype(BF16)
    for c in range(4):
        acc = jnp.dot(h, w_ref[:, BRANCH + c * IN_TN:BRANCH + (c + 1) * IN_TN],
                      preferred_element_type=F32)
        for s in range(4):
            gate_ref[4 * c + s] = acc[:, s * LANES:(s + 1) * LANES].astype(BF16)
    acc = jnp.dot(h, w_ref[:, 2 * BRANCH:], preferred_element_type=F32)
    nkv = A_KVH * A_HEAD
    lane = lax.broadcasted_iota(jnp.int32, (IN_TM, LANES), 1)
    for p in range(nkv // LANES):
        a = rope(acc[:, p * LANES:(p + 1) * LANES], 3)
        swapped = pltpu.roll(a, A_HEAD, 1)
        kk_ref[2 * p] = jnp.where(lane < A_HEAD, a, swapped).astype(BF16)
        kk_ref[2 * p + 1] = jnp.where(lane >= A_HEAD, a, swapped).astype(BF16)
    vt_ref[0] = acc[:, nkv:].T.astype(BF16)


def _inproj_a(x2d, g, w, tabs, bsz, seq):
    t = x2d.shape[0]
    nblk = seq // IN_TM
    n = w.shape[1]
    return pl.pallas_call(
        _inproj_a_kernel,
        grid=(t // IN_TM,),
        in_specs=[
            pl.BlockSpec((IN_TM, D_MODEL), lambda i: (i, 0)),
            pl.BlockSpec((1, D_MODEL), lambda i: (0, 0)),
            pl.BlockSpec((D_MODEL, n), lambda i: (0, 0)),
            pl.BlockSpec((6, IN_TM, LANES), lambda i: (0, i % nblk, 0)),
        ],
        out_specs=[
            pl.BlockSpec((A_PAIRS, IN_TM, LANES), lambda i: (0, i, 0)),
            pl.BlockSpec((A_PAIRS, IN_TM, LANES), lambda i: (0, i, 0)),
            pl.BlockSpec((A_KVH, IN_TM, LANES), lambda i: (0, i, 0)),
            pl.BlockSpec((1, A_KVH * A_HEAD, IN_TM), lambda i: (i // nblk, 0, i % nblk)),
        ],
        out_shape=[
            jax.ShapeDtypeStruct((A_PAIRS, t, LANES), BF16),
            jax.ShapeDtypeStruct((A_PAIRS, t, LANES), BF16),
            jax.ShapeDtypeStruct((A_KVH, t, LANES), BF16),
            jax.ShapeDtypeStruct((bsz, A_KVH * A_HEAD, seq), BF16),
        ],
        compiler_params=_params(("parallel",)),
        name="inproj_a",
    )(x2d, g, w, tabs)


def _inproj_kernel(x_ref, g_ref, w_ref, *rest, rope_heads):
    if rope_heads:
        tab_ref, o_ref = rest
    else:
        (o_ref,) = rest
    h = _rms(x_ref[...], g_ref[...]).astype(BF16)
    n = w_ref.shape[1]
    for c0 in range(0, n, IN_TN):
        width = min(IN_TN, n - c0)
        acc = jnp.dot(h, w_ref[:, c0:c0 + width], preferred_element_type=F32)
        for lo in range(0, width, B_QK):
            head = (c0 + lo) // B_QK
            if head < rope_heads:
                t0 = 0 if head < rope_heads // 2 else 2
                cos, sin = tab_ref[t0], tab_ref[t0 + 1]
                x1 = acc[:, lo:lo + LANES]
                x2 = acc[:, lo + LANES:lo + B_QK]
                o_ref[:, c0 + lo:c0 + lo + LANES] = (x1 * cos - x2 * sin).astype(BF16)
                o_ref[:, c0 + lo + LANES:c0 + lo + B_QK] = (x2 * cos + x1 * sin).astype(BF16)
            else:
                o_ref[:, c0 + lo:c0 + lo + B_QK] = acc[:, lo:lo + B_QK].astype(BF16)


def _inproj(x2d, g, w, tabs, seq, rope_heads):
    t = x2d.shape[0]
    n = w.shape[1]
    nblk = seq // IN_TM
    in_specs = [
        pl.BlockSpec((IN_TM, D_MODEL), lambda i: (i, 0)),
        pl.BlockSpec((1, D_MODEL), lambda i: (0, 0)),
        pl.BlockSpec((D_MODEL, n), lambda i: (0, 0), pipeline_mode=pl.Buffered(1)),
    ]
    args = [x2d, g, w]
    if rope_heads:
        in_specs.append(pl.BlockSpec((4, IN_TM, LANES), lambda i: (0, i % nblk, 0)))
        args.append(tabs)
    return pl.pallas_call(
        functools.partial(_inproj_kernel, rope_heads=rope_heads),
        grid=(t // IN_TM,),
        in_specs=in_specs,
        out_specs=pl.BlockSpec((IN_TM, n), lambda i: (i, 0)),
        out_shape=jax.ShapeDtypeStruct((t, n), BF16),
        compiler_params=_params(("parallel",)),
        name="inproj",
    )(*args)


def _attn_kernel(sink_ref, q_ref, gate_ref, kp_ref, kc_ref, kn_ref, vp_ref, vc_ref, vn_ref,
                 x_ref, w_ref, fg_ref, o_ref, k_ref, v_ref, bias_ref, *, final):
    i = pl.program_id(1)
    last = pl.num_programs(1) - 1
    rows = 3 * A_WIN
    nslice = A_QBLK // A_WIN
    per_kv = A_PAIRS // A_KVH

    ones_row = (lax.broadcasted_iota(jnp.int32, (A_VPAD, A_SPAN), 0) == 0).astype(BF16)
    for h in range(A_KVH):
        k_ref[h] = jnp.concatenate([kp_ref[h], kc_ref[h], kn_ref[h]], axis=0)
        v_ref[h, :A_HEAD, :] = jnp.concatenate([vp_ref[0, h], vc_ref[0, h], vn_ref[0, h]], axis=1)
        v_ref[h, A_HEAD:, :] = ones_row

    r = lax.broadcasted_iota(jnp.int32, (A_WIN, LANES), 0)
    c = lax.broadcasted_iota(jnp.int32, (A_WIN, LANES), 1)
    neg = jnp.full((A_WIN, LANES), -1e30, F32)
    top = jnp.where(r >= c, 0.0, neg)
    bot = jnp.where(r <= c, 0.0, neg)
    bias_ref[0] = jnp.where(i == 0, neg, top)
    bias_ref[1] = top
    bias_ref[2] = bot
    bias_ref[3] = jnp.where(i == last, neg, bot)
    qlane = lax.broadcasted_iota(jnp.int32, (LANES, LANES), 1)

    units = [(g, jl, jj) for g in range(A_KVH) for jl in range(per_kv) for jj in range(nslice)]

    def scores(n):
        g, jl, jj = units[n]
        q = q_ref[g * per_kv + jl, jj * LANES:(jj + 1) * LANES, :]
        zero = jnp.zeros_like(q)
        rhs = jnp.concatenate([jnp.where(qlane < A_HEAD, q, zero),
                               jnp.where(qlane >= A_HEAD, q, zero)], axis=0)
        return lax.dot_general(k_ref[g, jj * A_WIN:jj * A_WIN + rows, :], rhs,
                               (((1,), (1,)), ((), ())), preferred_element_type=F32)

    def softmax_pv(s_t, sinks, g, jj):
        ps, extra = [], []
        for a in range(2):
            blk = s_t[:, a * LANES:(a + 1) * LANES]
            parts = [blk[:A_WIN] + bias_ref[0 if jj == 0 else 1], blk[A_WIN:2 * A_WIN],
                     blk[2 * A_WIN:] + bias_ref[3 if jj == nslice - 1 else 2]]
            m8 = functools.reduce(jnp.maximum, [part.reshape(-1, 8, LANES).max(axis=0) for part in parts])
            mx = jnp.maximum(jnp.max(m8, axis=0, keepdims=True), sinks[a])
            ps.append(jnp.concatenate([jnp.exp2(part - mx).astype(BF16) for part in parts], axis=0))
            extra.append(jnp.exp2(sinks[a] - mx))
        o_ext = jnp.dot(v_ref[g, :, jj * A_WIN:jj * A_WIN + rows], jnp.concatenate(ps, axis=1),
                        preferred_element_type=F32)
        den = o_ext[A_HEAD:A_HEAD + 1, :] + jnp.concatenate(extra, axis=1)
        return o_ext[:A_HEAD, :] / den

    pending = [scores(n) for n in range(A_LOOKAHEAD)]
    outs, gated = [], []
    y = x_ref[...]
    for n, (g, jl, jj) in enumerate(units):
        if n + A_LOOKAHEAD < len(units):
            pending.append(scores(n + A_LOOKAHEAD))
        j = g * per_kv + jl
        sinks = (sink_ref[2 * j] * LOG2E, sink_ref[2 * j + 1] * LOG2E)
        outs.append(softmax_pv(pending.pop(0), sinks, g, jj))
        if jj == nslice - 1:
            last_outs = outs[-nslice:]
            o_pair = jnp.concatenate(
                [jnp.concatenate([o[:, :LANES] for o in last_outs], axis=1),
                 jnp.concatenate([o[:, LANES:] for o in last_outs], axis=1)], axis=0).T
            gt = gate_ref[j].astype(F32)
            gated.append((o_pair * _silu(gt)).astype(BF16))
            if jl == per_kv - 1 and (g + 1) % A_OUT_GROUPS == 0:
                npair = per_kv * A_OUT_GROUPS
                wg = w_ref[(j + 1 - npair) * LANES:(j + 1) * LANES, :]
                y = y + jnp.dot(jnp.concatenate(gated[-npair:], axis=1), wg,
                                preferred_element_type=F32)
    if final:
        y = _rms(y, fg_ref[...])
    o_ref[...] = y


def _attention(x2d, q3, gate3, kk, vt, sink, w_out, fin_g, bsz, seq, final):
    t = x2d.shape[0]
    nqb = seq // A_QBLK
    nkb = seq // A_WIN
    nsl = A_QBLK // A_WIN
    vt4 = vt.reshape(bsz, A_KVH, A_HEAD, seq)
    grid_spec = pltpu.PrefetchScalarGridSpec(
        num_scalar_prefetch=1,
        grid=(bsz, nqb),
        in_specs=[
            pl.BlockSpec((A_PAIRS, A_QBLK, LANES), lambda b, i, s: (0, b * nqb + i, 0)),
            pl.BlockSpec((A_PAIRS, A_QBLK, LANES), lambda b, i, s: (0, b * nqb + i, 0)),
            pl.BlockSpec((A_KVH, A_WIN, LANES),
                         lambda b, i, s: (0, b * nkb + jnp.maximum(nsl * i - 1, 0), 0)),
            pl.BlockSpec((A_KVH, A_QBLK, LANES), lambda b, i, s: (0, b * nqb + i, 0)),
            pl.BlockSpec((A_KVH, A_WIN, LANES),
                         lambda b, i, s: (0, b * nkb + jnp.minimum(nsl * i + nsl, nkb - 1), 0)),
            pl.BlockSpec((1, A_KVH, A_HEAD, A_WIN),
                         lambda b, i, s: (b, 0, 0, jnp.maximum(nsl * i - 1, 0))),
            pl.BlockSpec((1, A_KVH, A_HEAD, A_QBLK), lambda b, i, s: (b, 0, 0, i)),
            pl.BlockSpec((1, A_KVH, A_HEAD, A_WIN),
                         lambda b, i, s: (b, 0, 0, jnp.minimum(nsl * i + nsl, nkb - 1))),
            pl.BlockSpec((A_QBLK, D_MODEL), lambda b, i, s: (b * nqb + i, 0)),
            pl.BlockSpec((BRANCH, D_MODEL), lambda b, i, s: (0, 0)),
            pl.BlockSpec((1, D_MODEL), lambda b, i, s: (0, 0)),
        ],
        out_specs=pl.BlockSpec((A_QBLK, D_MODEL), lambda b, i, s: (b * nqb + i, 0)),
        scratch_shapes=[
            pltpu.VMEM((A_KVH, A_SPAN, LANES), BF16),
            pltpu.VMEM((A_KVH, A_HEAD + A_VPAD, A_SPAN), BF16),
            pltpu.VMEM((4, A_WIN, LANES), F32),
        ],
    )
    return pl.pallas_call(
        functools.partial(_attn_kernel, final=final),
        grid_spec=grid_spec,
        out_shape=jax.ShapeDtypeStruct((t, D_MODEL), F32),
        compiler_params=_params(("parallel", "parallel")),
        name="attention",
    )(sink, q3, gate3, kk, kk, kk, vt4, vt4, vt4, x2d, w_out, fin_g)


def _retention_start(lg_ref, st_ref, intra_ref, dec_ref, backward):
    n = B_CHUNK
    ii = lax.broadcasted_iota(jnp.int32, (n, n), 0)
    jj = lax.broadcasted_iota(jnp.int32, (n, n), 1)
    col = lax.broadcasted_iota(jnp.int32, (n, LANES), 0).astype(F32)
    st_ref[...] = jnp.zeros_like(st_ref)
    for h in range(B_HEADS):
        lg = lg_ref[1 if backward else 0, h]
        if backward:
            mask = jj > ii
            dist = (jj - ii).astype(F32)
            dec_ref[h, 0] = jnp.exp((n - col) * lg)
            dec_ref[h, 1] = jnp.exp(col * lg)
        else:
            mask = ii >= jj
            dist = (ii - jj).astype(F32)
            dec_ref[h, 0] = jnp.exp((col + 1.0) * lg)
            dec_ref[h, 1] = jnp.exp((n - 1.0 - col) * lg)
        intra_ref[h] = jnp.where(mask, jnp.exp(jnp.where(mask, dist, 0.0) * lg), 0.0)


def _retention_open(q_ref, k_ref, v_ref, row):
    heads = range(B_HEADS)
    qs = [q_ref[row, :, h * B_QK:(h + 1) * B_QK] for h in heads]
    ks = [k_ref[row, :, h * B_QK:(h + 1) * B_QK] for h in heads]
    vs = [v_ref[row, :, h * B_V:(h + 1) * B_V] for h in heads]
    sc = [lax.dot_general(qs[h], ks[h], (((1,), (1,)), ((), ())), preferred_element_type=F32)
          for h in heads]
    return qs, ks, vs, sc


def _retention_heads(lg_ref, opened, st_ref, intra_ref, dec_ref, row, backward, emit):
    qs, ks, vs, sc = opened
    for h in range(B_HEADS):
        q_dec = jnp.concatenate([dec_ref[h, 0]] * (B_QK // LANES), axis=1)
        k_dec = jnp.concatenate([dec_ref[h, 1]] * (B_QK // LANES), axis=1)
        c_dec = jnp.exp(jnp.full((1, B_V), B_CHUNK * lg_ref[1 if backward else 0, h], F32))
        st = st_ref[row, h]
        lhs = jnp.concatenate([(sc[h] * intra_ref[h]).astype(BF16),
                               (qs[h].astype(F32) * q_dec).astype(BF16)], axis=1)
        rhs = jnp.concatenate([vs[h], st.astype(BF16)], axis=0)
        out = jnp.dot(lhs, rhs, preferred_element_type=F32)
        kd_t = (ks[h].astype(F32) * k_dec).T.astype(BF16)
        st_ref[row, h] = st * c_dec + jnp.dot(kd_t, vs[h], preferred_element_type=F32)
        emit(h, out)


def _retention_bwd_kernel(lg_ref, q_ref, k_ref, v_ref, o_ref, st_ref, intra_ref, dec_ref):
    @pl.when(pl.program_id(1) == 0)
    def _():
        _retention_start(lg_ref, st_ref, intra_ref, dec_ref, True)

    rows = range(q_ref.shape[0])
    opened = [_retention_open(q_ref, k_ref, v_ref, row) for row in rows]
    for row in rows:
        def emit(h, o, row=row):
            o_ref[row, :, h * B_V:(h + 1) * B_V] = o.astype(BF16)

        _retention_heads(lg_ref, opened[row], st_ref, intra_ref, dec_ref, row, True, emit)


def _retention_fwd_kernel(lg_ref, q_ref, k_ref, v_ref, gate_ref, ob_ref, x_ref, w_ref, o_ref,
                          st_ref, intra_ref, dec_ref):
    @pl.when(pl.program_id(1) == 0)
    def _():
        _retention_start(lg_ref, st_ref, intra_ref, dec_ref, False)

    rows = range(q_ref.shape[0])
    opened = [_retention_open(q_ref, k_ref, v_ref, row) for row in rows]
    gated = []
    for row in rows:
        parts = []

        def emit(h, o_f, row=row, parts=parts):
            sl = slice(h * B_V, (h + 1) * B_V)
            o = o_f + ob_ref[row, :, sl].astype(F32)
            on = o * lax.rsqrt(jnp.mean(o * o, axis=-1, keepdims=True) + NORM_EPS)
            parts.append((on * _silu(gate_ref[row, :, sl].astype(F32))).astype(BF16))

        _retention_heads(lg_ref, opened[row], st_ref, intra_ref, dec_ref, row, False, emit)
        gated.append(jnp.concatenate(parts, axis=1))
    for row in rows:
        o_ref[row] = x_ref[row] + jnp.dot(gated[row], w_ref[...], preferred_element_type=F32)


def _retention(proj3, log_g, x3, w_out):
    bsz, seq, _ = proj3.shape
    nc = seq // B_CHUNK
    nqk = B_HEADS * B_QK
    rows = B_ROWS if bsz % B_ROWS == 0 else 1
    smem = pl.BlockSpec(memory_space=pltpu.SMEM)
    state = [pltpu.VMEM((rows, B_HEADS, B_QK, B_V), F32),
             pltpu.VMEM((B_HEADS, B_CHUNK, B_CHUNK), F32),
             pltpu.VMEM((B_HEADS, 2, B_CHUNK, LANES), F32)]

    def blk(width, chunk, col):
        return pl.BlockSpec((rows, B_CHUNK, width), lambda b, t: (b, chunk(t), col))

    def rev(t):
        return nc - 1 - t

    def fwd(t):
        return t

    o_b = pl.pallas_call(
        _retention_bwd_kernel,
        grid=(bsz // rows, nc),
        in_specs=[smem, blk(nqk, rev, 0), blk(nqk, rev, 1), blk(BRANCH, rev, 1)],
        out_specs=blk(BRANCH, rev, 0),
        out_shape=jax.ShapeDtypeStruct((bsz, seq, BRANCH), BF16),
        scratch_shapes=state,
        compiler_params=_params(("parallel", "arbitrary")),
        name="retention_bwd",
    )(log_g, proj3, proj3, proj3)
    return pl.pallas_call(
        _retention_fwd_kernel,
        grid=(bsz // rows, nc),
        in_specs=[smem, blk(nqk, fwd, 0), blk(nqk, fwd, 1), blk(BRANCH, fwd, 1),
                  blk(BRANCH, fwd, 2), blk(BRANCH, fwd, 0), blk(D_MODEL, fwd, 0),
                  pl.BlockSpec((BRANCH, D_MODEL), lambda b, t: (0, 0))],
        out_specs=blk(D_MODEL, fwd, 0),
        out_shape=jax.ShapeDtypeStruct((bsz, seq, D_MODEL), F32),
        scratch_shapes=state,
        compiler_params=_params(("parallel", "arbitrary")),
        name="retention_fwd",
    )(log_g, proj3, proj3, proj3, proj3, o_b, x3, w_out)


C_HALF = C_GDIM // 2
C_MAIN = C_GROUPS * C_HALF
C_FOLD = 2 * C_HALF + LANES
C_NYQ_ROWS = 16


def _fold_kernel(w_ref, cs_ref, a_ref, b_ref, sp_ref):
    r = jnp.dot(w_ref[...], cs_ref[0], preferred_element_type=F32,
                precision=lax.Precision.HIGHEST)
    a_ref[...] = r[:, :C_HALF].astype(BF16)
    b_ref[...] = r[:, C_HALF:2 * C_HALF].astype(BF16)

    @pl.when(pl.program_id(0) == 0)
    def _():
        sp_ref[...] = jnp.zeros_like(sp_ref)

    sp_ref[...] += r[:, 2 * C_HALF:]


def _fold_channel_dft(w_u, cs):
    spec = pl.BlockSpec((D_MODEL, C_HALF), lambda g: (0, g))
    return pl.pallas_call(
        _fold_kernel,
        grid=(C_GROUPS,),
        in_specs=[pl.BlockSpec((D_MODEL, C_GDIM), lambda g: (0, g)),
                  pl.BlockSpec((1, C_GDIM, C_FOLD), lambda g: (g, 0, 0))],
        out_specs=[spec, spec, pl.BlockSpec((D_MODEL, LANES), lambda g: (0, 0))],
        out_shape=[jax.ShapeDtypeStruct((D_MODEL, C_MAIN), BF16)] * 2
        + [jax.ShapeDtypeStruct((D_MODEL, LANES), F32)],
        compiler_params=_params(("arbitrary",)),
        name="fold_channel_dft",
    )(w_u, cs)


DFT_ROWS = 512


def _dft_mix_kernel(cm_ref, sm_ref, a_ref, b_ref, gm_ref, gp_ref, sp_ref, gsp_ref,
                    om_ref, op_ref, osp_ref):
    n = cm_ref.shape[0]
    rc = min(n, DFT_ROWS)
    a = a_ref[0]
    b = b_ref[0]
    for r in range(n // rc):
        rows = slice(r * rc, (r + 1) * rc)
        ea = jnp.dot(cm_ref[rows, :], a, preferred_element_type=F32)
        eb = jnp.dot(sm_ref[rows, :], b, preferred_element_type=F32)
        om_ref[0, rows, :] = ((ea - eb) * _silu(gm_ref[0, rows, :].astype(F32))).astype(BF16)
        op_ref[0, rows, :] = ((ea + eb) * _silu(gp_ref[0, rows, :].astype(F32))).astype(BF16)

    @pl.when(pl.program_id(1) == 0)
    def _():
        sp_t = sp_ref[0].astype(F32).T[:C_NYQ_ROWS, :].astype(BF16)
        es_t = jnp.dot(sp_t, cm_ref[...], preferred_element_type=F32)
        es = jnp.concatenate([es_t, jnp.zeros((LANES - C_NYQ_ROWS, n), F32)], axis=0).T
        osp_ref[0] = (es * _silu(gsp_ref[0].astype(F32))).astype(BF16)


def _trig(m, period, scale):
    ang = (2.0 * math.pi / period) * (m % period).astype(F32)
    return jnp.cos(ang) * scale, jnp.sin(ang) * scale


def _fourier_mix(proj3):
    bsz, seq, _ = proj3.shape
    n = jnp.arange(seq, dtype=jnp.int32)[None, None, :]
    hi = jnp.arange(seq // LANES, dtype=jnp.int32)[:, None, None] * LANES
    lo = jnp.arange(LANES, dtype=jnp.int32)[None, :, None]
    ca, sa = _trig(hi * n, seq, seq ** -0.5)
    cb, sb = _trig(lo * n, seq, 1.0)
    cm = (ca * cb - sa * sb).reshape(seq, seq)
    sm = (sa * cb + ca * sb).reshape(seq, seq)
    tc = 512
    nct = C_MAIN // tc
    mspec = pl.BlockSpec((seq, seq), lambda b, c: (0, 0))

    def dspec(width, idx):
        return pl.BlockSpec((1, seq, width), lambda b, c: (b, 0, idx(c)))

    sp_blk = 4 * C_MAIN // LANES
    return pl.pallas_call(
        _dft_mix_kernel,
        grid=(bsz, nct),
        in_specs=[mspec, mspec,
                  dspec(tc, lambda c: c), dspec(tc, lambda c: nct + c),
                  dspec(tc, lambda c: 2 * nct + c), dspec(tc, lambda c: 3 * nct + c),
                  dspec(LANES, lambda c: sp_blk), dspec(LANES, lambda c: sp_blk + 1)],
        out_specs=[dspec(tc, lambda c: c), dspec(tc, lambda c: c), dspec(LANES, lambda c: 0)],
        out_shape=[jax.ShapeDtypeStruct((bsz, seq, C_MAIN), BF16)] * 2
        + [jax.ShapeDtypeStruct((bsz, seq, LANES), BF16)],
        compiler_params=_params(("parallel", "arbitrary")),
        name="dft_mix",
    )(cm.astype(BF16), sm.astype(BF16), proj3, proj3, proj3, proj3, proj3, proj3)


C_GRP = 8
C_SUB = 64
C_ROWS = 256
C_NCHUNK = 1024


def _inproj_perm_kernel(x_ref, g_ref, w_ref, o_ref, xs_ref):
    rh = x_ref.shape[1]
    n = w_ref.shape[1]
    for i in range(C_GRP):
        xs_ref[i * rh:(i + 1) * rh, :] = x_ref[0, :, i, :]
    per_dot = C_ROWS // rh
    for j in range(C_GRP // per_dot):
        h = _rms(xs_ref[j * C_ROWS:(j + 1) * C_ROWS, :], g_ref[...]).astype(BF16)
        for c0 in range(0, n, C_NCHUNK):
            cols = slice(c0, min(c0 + C_NCHUNK, n))
            acc = jnp.dot(h, w_ref[:, cols], preferred_element_type=F32)
            for i in range(per_dot):
                o_ref[0, j * per_dot + i, :, cols] = acc[i * rh:(i + 1) * rh, :].astype(BF16)


def _inproj_perm(x4, g, w):
    bsz, r, _, _ = x4.shape
    n = w.shape[1]
    rh = min(r, C_SUB)
    return pl.pallas_call(
        _inproj_perm_kernel,
        grid=(bsz, r // C_GRP, r // rh),
        in_specs=[
            pl.BlockSpec((1, rh, C_GRP, D_MODEL), lambda b, m, s: (b, s, m, 0)),
            pl.BlockSpec((1, D_MODEL), lambda b, m, s: (0, 0)),
            pl.BlockSpec((D_MODEL, n), lambda b, m, s: (0, 0)),
        ],
        out_specs=pl.BlockSpec((1, C_GRP, rh, n), lambda b, m, s: (b, m, s, 0)),
        out_shape=jax.ShapeDtypeStruct((bsz, r, r, n), BF16),
        scratch_shapes=[pltpu.VMEM((C_GRP * rh, D_MODEL), F32)],
        compiler_params=_params(("parallel", "parallel", "parallel")),
        name="inproj_perm",
    )(x4, g, w)


def _dft_stage1_kernel(d_ref, x_ref, sp_ref, yr_ref, yi_ref, ysr_ref, ysi_ref):
    r = x_ref.shape[2]
    for i in range(C_GRP):
        dc = d_ref[i, :r, :]
        ds = d_ref[i, r:, :]
        yr_ref[0, :, i, :] = jnp.dot(dc, x_ref[0, i], preferred_element_type=F32)
        yi_ref[0, :, i, :] = jnp.dot(ds, x_ref[0, i], preferred_element_type=F32)
        ysr_ref[0, :, i, :] = jnp.dot(dc, sp_ref[0, i], preferred_element_type=F32)
        ysi_ref[0, :, i, :] = jnp.dot(ds, sp_ref[0, i], preferred_element_type=F32)


def _dft_stage1(proj4, dtab):
    bsz, r, _, _ = proj4.shape
    tc = 2 * C_MAIN
    nct = 2 * C_MAIN // tc

    def yspec(width, idx):
        return pl.BlockSpec((1, r, C_GRP, width), lambda b, m, c: (b, 0, m, idx(c)))

    return pl.pallas_call(
        _dft_stage1_kernel,
        grid=(bsz, r // C_GRP, nct),
        in_specs=[
            pl.BlockSpec((C_GRP, 2 * r, r), lambda b, m, c: (m, 0, 0)),
            pl.BlockSpec((1, C_GRP, r, tc), lambda b, m, c: (b, m, 0, c)),
            pl.BlockSpec((1, C_GRP, r, LANES), lambda b, m, c: (b, m, 0, 4 * C_MAIN // LANES)),
        ],
        out_specs=[yspec(tc, lambda c: c), yspec(tc, lambda c: c),
                   yspec(LANES, lambda c: 0), yspec(LANES, lambda c: 0)],
        out_shape=[jax.ShapeDtypeStruct((bsz, r, r, 2 * C_MAIN), F32)] * 2
        + [jax.ShapeDtypeStruct((bsz, r, r, LANES), F32)] * 2,
        compiler_params=_params(("parallel", "parallel", "arbitrary")),
        name="dft_stage1",
    )(dtab, proj4, proj4)


def _dft_stage2_kernel(c1_ref, s1_ref, yra_ref, yia_ref, yrb_ref, yib_ref, gm_ref, gp_ref,
                       wm_ref, wp_ref, ysr_ref, ysi_ref, gsp_ref, wsp_ref, x_ref, o_ref, acc_ref):
    c = pl.program_id(2)
    r = c1_ref.shape[0]
    c1 = c1_ref[...]
    s1 = s1_ref[...]

    def dot(m, y):
        return jnp.dot(m, y.astype(BF16), preferred_element_type=F32)

    @pl.when(c == 0)
    def _():
        parts = []
        for i in range(C_GRP):
            es = dot(c1, ysr_ref[0, i]) - dot(s1, ysi_ref[0, i])
            parts.append((es * _silu(gsp_ref[0, i].astype(F32))).astype(BF16))
        nyq = jnp.dot(jnp.concatenate(parts, axis=0), wsp_ref[...], preferred_element_type=F32)
        for i in range(C_GRP):
            acc_ref[i * r:(i + 1) * r, :] = x_ref[0, :, i, :] + nyq[i * r:(i + 1) * r, :]

    minus, plus = [], []
    for i in range(C_GRP):
        ea = dot(c1, yra_ref[0, i]) - dot(s1, yia_ref[0, i])
        eb = dot(c1, yib_ref[0, i]) + dot(s1, yrb_ref[0, i])
        minus.append(((ea - eb) * _silu(gm_ref[0, i].astype(F32))).astype(BF16))
        plus.append(((ea + eb) * _silu(gp_ref[0, i].astype(F32))).astype(BF16))
    acc_ref[...] += (jnp.dot(jnp.concatenate(minus, axis=0), wm_ref[...], preferred_element_type=F32)
                     + jnp.dot(jnp.concatenate(plus, axis=0), wp_ref[...], preferred_element_type=F32))

    @pl.when(c == pl.num_programs(2) - 1)
    def _():
        for i in range(C_GRP):
            o_ref[0, :, i, :] = acc_ref[i * r:(i + 1) * r, :]


def _dft_stage2(c1, s1, y_r, y_i, y_sr, y_si, proj4, w_out, x4):
    bsz, r, _, _ = x4.shape
    tc = 512
    nct = C_MAIN // tc
    mspec = pl.BlockSpec((r, r), lambda b, m, c: (0, 0))

    def slab(width, idx):
        return pl.BlockSpec((1, C_GRP, r, width), lambda b, m, c: (b, m, 0, idx(c)))

    def wspec(rows, idx):
        return pl.BlockSpec((rows, D_MODEL), lambda b, m, c: (idx(c), 0))

    xspec = pl.BlockSpec((1, r, C_GRP, D_MODEL), lambda b, m, c: (b, 0, m, 0))
    sp_blk = 4 * C_MAIN // LANES
    return pl.pallas_call(
        _dft_stage2_kernel,
        grid=(bsz, r // C_GRP, nct),
        in_specs=[
            mspec, mspec,
            slab(tc, lambda c: c), slab(tc, lambda c: c),
            slab(tc, lambda c: nct + c), slab(tc, lambda c: nct + c),
            slab(tc, lambda c: 2 * nct + c), slab(tc, lambda c: 3 * nct + c),
            wspec(tc, lambda c: c), wspec(tc, lambda c: nct + c),
            slab(LANES, lambda c: 0), slab(LANES, lambda c: 0),
            slab(LANES, lambda c: sp_blk + 1),
            wspec(LANES, lambda c: 2 * C_MAIN // LANES),
            xspec,
        ],
        out_specs=xspec,
        out_shape=jax.ShapeDtypeStruct(x4.shape, F32),
        scratch_shapes=[pltpu.VMEM((C_GRP * r, D_MODEL), F32)],
        compiler_params=_params(("parallel", "parallel", "arbitrary")),
        name="dft_stage2",
    )(c1, s1, y_r, y_i, y_r, y_i, proj4, proj4, w_out, w_out, y_sr, y_si, proj4, w_out, x4)


def _fourier_long(x2d, g, wc, w_out, bsz, seq):
    r = math.isqrt(seq)
    assert r * r == seq and r % C_GRP == 0
    x4 = x2d.reshape(bsz, r, r, D_MODEL)
    proj4 = _inproj_perm(x4, g, wc)
    idx = jnp.arange(r, dtype=jnp.int32)
    k2 = idx[None, :, None]
    ca, sa = _trig(k2 * idx[:, None, None], seq, r ** -0.5)
    cb, sb = _trig(k2 * (r * idx[None, None, :]), seq, 1.0)
    dc, ds = ca * cb - sa * sb, sa * cb + ca * sb
    dtab = jnp.concatenate([dc, ds], axis=1).astype(BF16)
    y_r, y_i, y_sr, y_si = _dft_stage1(proj4, dtab)
    c1, s1 = _trig(idx[:, None] * idx[None, :], r, r ** -0.5)
    out = _dft_stage2(c1.astype(BF16), s1.astype(BF16), y_r, y_i, y_sr, y_si, proj4, w_out, x4)
    return out.reshape(bsz * seq, D_MODEL)


OUT_TM = 512


def _outproj_kernel(mm_ref, mp_ref, ms_ref, x_ref, w_ref, o_ref):
    acc = jnp.dot(mm_ref[...], w_ref[:C_MAIN, :], preferred_element_type=F32)
    acc = acc + jnp.dot(mp_ref[...], w_ref[C_MAIN:2 * C_MAIN, :], preferred_element_type=F32)
    acc = acc + jnp.dot(ms_ref[...], w_ref[2 * C_MAIN:, :], preferred_element_type=F32)
    o_ref[...] = x_ref[...] + acc


def _outproj(minus2d, plus2d, nyq2d, x2d, w_out):
    t = x2d.shape[0]
    return pl.pallas_call(
        _outproj_kernel,
        grid=(t // OUT_TM,),
        in_specs=[
            pl.BlockSpec((OUT_TM, C_MAIN), lambda i: (i, 0)),
            pl.BlockSpec((OUT_TM, C_MAIN), lambda i: (i, 0)),
            pl.BlockSpec((OUT_TM, LANES), lambda i: (i, 0)),
            pl.BlockSpec((OUT_TM, D_MODEL), lambda i: (i, 0)),
            pl.BlockSpec((2 * C_MAIN + LANES, D_MODEL), lambda i: (0, 0)),
        ],
        out_specs=pl.BlockSpec((OUT_TM, D_MODEL), lambda i: (i, 0)),
        out_shape=jax.ShapeDtypeStruct((t, D_MODEL), F32),
        compiler_params=_params(("parallel",)),
        name="outproj",
    )(minus2d, plus2d, nyq2d, x2d, w_out)


def _rope_tables_a(seq):
    half = A_ROT // 2
    inv_freq = jnp.exp(-(jnp.arange(half, dtype=F32) * (2.0 / A_ROT)) * math.log(A_THETA))
    ang = jnp.arange(seq, dtype=F32)[:, None] * inv_freq[None, :]
    cos, sin = jnp.cos(ang), jnp.sin(ang)
    d = jnp.arange(LANES) % A_HEAD
    f = d % half
    cm = jnp.where(d[None, :] < A_ROT, cos[:, f], 1.0)
    s1 = jnp.where(d[None, :] < half, -sin[:, f], 0.0)
    s2 = jnp.where((d[None, :] >= half) & (d[None, :] < A_ROT), sin[:, f], 0.0)
    qs = A_HEAD ** -0.5 * LOG2E
    return jnp.stack([cm * qs, s1 * qs, s2 * qs, cm, s1, s2]).astype(F32)


def _rope_tables_b(seq):
    half = B_QK // 2
    inv_freq = jnp.exp(-(jnp.arange(half, dtype=F32) * (2.0 / B_QK)) * math.log(B_THETA))
    ang = jnp.arange(seq, dtype=F32)[:, None] * inv_freq[None, :]
    cos, sin = jnp.cos(ang), jnp.sin(ang)
    ks = B_QK ** -0.5
    return jnp.stack([cos, sin, cos * ks, sin * ks]).astype(F32)


def _weights_a(w_in):
    nq = A_QH * A_HEAD
    nkv = A_KVH * A_HEAD
    q = w_in[:, :nq]
    kv = w_in[:, nq:nq + 2 * nkv]
    gate = w_in[:, nq + 2 * nkv:]
    return jnp.concatenate([q, gate, kv], axis=1).astype(BF16)


def _c_index():
    g = jnp.arange(C_GROUPS, dtype=jnp.int32)[:, None] * C_GDIM
    m = jnp.arange(C_HALF, dtype=jnp.int32)[None, :]
    minus = (g + m).reshape(-1)
    plus = (g + jnp.where(m == 0, 0, C_GDIM - m)).reshape(-1)
    nyq = (g + C_HALF).reshape(-1)
    return minus, plus, nyq, jnp.tile(m == 0, (C_GROUPS, 1)).reshape(-1)


def _weights_c(w_in):
    ch = jnp.arange(C_GDIM, dtype=jnp.int32)
    m = jnp.arange(C_HALF, dtype=jnp.int32)
    cm, sm = _trig(ch[:, None] * m[None, :], C_GDIM, C_GDIM ** -0.5)
    nyq, _ = _trig(ch * C_HALF, C_GDIM, C_GDIM ** -0.5)
    lane = jnp.arange(LANES)[None, None, :] == jnp.arange(C_GROUPS)[:, None, None]
    nyq = jnp.where(lane, nyq[None, :, None], 0.0)
    cs = jnp.concatenate([jnp.broadcast_to(cm, (C_GROUPS,) + cm.shape),
                          jnp.broadcast_to(sm, (C_GROUPS,) + sm.shape), nyq], axis=2)
    w_a, w_b, w_sp = _fold_channel_dft(w_in[:, :BRANCH], cs)
    minus, plus, nyq_idx, _ = _c_index()
    gate = w_in[:, BRANCH:]
    g_sp = jnp.zeros((D_MODEL, LANES), F32).at[:, :C_GROUPS].set(gate[:, nyq_idx])
    return jnp.concatenate([w_a, w_b, gate[:, minus].astype(BF16), gate[:, plus].astype(BF16),
                            w_sp.astype(BF16), g_sp.astype(BF16)], axis=1)


def _weights_c_out(w_out):
    minus, plus, nyq_idx, dup = _c_index()
    w_plus = jnp.where(dup[:, None], 0.0, w_out[plus])
    w_sp = jnp.zeros((LANES, D_MODEL), F32).at[:C_GROUPS].set(w_out[nyq_idx])
    return jnp.concatenate([w_out[minus], w_plus, w_sp], axis=0).astype(BF16)


def _trunk(x, norm_g, fin_g, wa, a_sink, a_w_out, wb, log_g, b_w_out, wc, c_w_out):
    bsz, seq, _ = x.shape
    t = bsz * seq
    x2d = x.reshape(t, D_MODEL)
    tabs_a = _rope_tables_a(seq)
    tabs_b = _rope_tables_b(seq)
    fin = fin_g.reshape(1, D_MODEL)

    def layer_a(x2d, layer, j, final):
        q3, gate3, kk, vt = _inproj_a(x2d, norm_g[layer].reshape(1, D_MODEL), wa[j], tabs_a, bsz, seq)
        return _attention(x2d, q3, gate3, kk, vt, a_sink[j], a_w_out[j], fin, bsz, seq, final)

    x2d = layer_a(x2d, 0, 0, False)

    proj = _inproj(x2d, norm_g[1].reshape(1, D_MODEL), wb, tabs_b, seq, 2 * B_HEADS)
    x2d = _retention(proj.reshape(bsz, seq, -1), log_g, x2d.reshape(bsz, seq, D_MODEL),
                     b_w_out).reshape(t, D_MODEL)

    if seq <= C_DIRECT_MAX:
        proj = _inproj(x2d, norm_g[2].reshape(1, D_MODEL), wc, None, seq, 0)
        minus, plus, nyq = _fourier_mix(proj.reshape(bsz, seq, -1))
        x2d = _outproj(minus.reshape(t, C_MAIN), plus.reshape(t, C_MAIN), nyq.reshape(t, LANES),
                       x2d, c_w_out)
    else:
        x2d = _fourier_long(x2d, norm_g[2].reshape(1, D_MODEL), wc, c_w_out, bsz, seq)

    x2d = layer_a(x2d, 3, 1, True)
    return x2d.reshape(bsz, seq, D_MODEL)


def kernel(x_prompt, x_sample, norm_g, final_norm_g, a_w_in, a_sink, a_w_out, b_w_in, b_decay,
           b_w_out, c_w_in, c_w_out):
    wa = [_weights_a(a_w_in[j]) for j in range(a_w_in.shape[0])]
    a_out = [a_w_out[j].astype(BF16) for j in range(a_w_out.shape[0])]
    wb = b_w_in[0].astype(BF16)
    log_g = jax.nn.log_sigmoid(b_decay[0].astype(F32))
    wc = _weights_c(c_w_in[0])
    args = (norm_g, final_norm_g, wa, a_sink, a_out, wb, log_g, b_w_out[0].astype(BF16),
            wc, _weights_c_out(c_w_out[0]))
    return (_trunk(x_prompt, *args), _trunk(x_sample, *args))
```

```python
import functools
import math

import jax
import jax.numpy as jnp
from jax import lax
from jax.experimental import pallas as pl
from jax.experimental.pallas import tpu as pltpu

D_MODEL = 1024
BRANCH = 2048
NORM_EPS = 1e-6

A_HEAD = 64
A_QH = 32
A_KVH = 4
A_ROT = 16
A_THETA = 500000.0
A_WIN = 128
A_QBLK = 1024
A_SPAN = A_QBLK + 2 * A_WIN
A_PAIRS = A_QH // 2
A_LOOKAHEAD = 8
A_OUT_GROUPS = 2
A_VPAD = 16
LOG2E = math.log2(math.e)

B_QK = 256
B_HEADS = 4
B_V = 512
B_CHUNK = 256
B_ROWS = 2
B_THETA = 10000.0

C_GROUPS = 4
C_GDIM = 512
C_DIRECT_MAX = 2048

LANES = 128
VMEM_LIMIT = 56 * 1024 * 1024

BF16 = jnp.bfloat16
F32 = jnp.float32


def _params(sem):
    return pltpu.CompilerParams(dimension_semantics=sem, vmem_limit_bytes=VMEM_LIMIT)


def _silu(x):
    half = 0.5 * x
    return half + half * jnp.tanh(half)


def _rms(x, g):
    ms = jnp.mean(x * x, axis=-1, keepdims=True)
    return (x * lax.rsqrt(ms + NORM_EPS)) * g


IN_TM = 1024
IN_TN = 512


def _inproj_a_kernel(x_ref, g_ref, w_ref, tab_ref, q_ref, gate_ref, kk_ref, vt_ref):
    h = _rms(x_ref[...], g_ref[...]).astype(BF16)

    def rope(a, t0):
        return (a * tab_ref[t0] + pltpu.roll(a, LANES - A_ROT // 2, 1) * tab_ref[t0 + 1]
                + pltpu.roll(a, A_ROT // 2, 1) * tab_ref[t0 + 2])

    acc = jnp.dot(h, w_ref[:, 2 * BRANCH:], preferred_element_type=F32)
    nkv = A_KVH * A_HEAD
    lane = lax.broadcasted_iota(jnp.int32, (IN_TM, LANES), 1)
    for p in range(nkv // LANES):
        a = rope(acc[:, p * LANES:(p + 1) * LANES], 3)
        swapped = pltpu.roll(a, A_HEAD, 1)
        kk_ref[2 * p] = jnp.where(lane < A_HEAD, a, swapped).astype(BF16)
        kk_ref[2 * p + 1] = jnp.where(lane >= A_HEAD, a, swapped).astype(BF16)
    vt_ref[0] = acc[:, nkv:].T.astype(BF16)
    for c in range(4):
        acc = jnp.dot(h, w_ref[:, c * IN_TN:(c + 1) * IN_TN], preferred_element_type=F32)
        for s in range(4):
            q_ref[4 * c + s] = rope(acc[:, s * LANES:(s + 1) * LANES], 0).astype(BF16)
    for c in range(4):
        acc = jnp.dot(h, w_ref[:, BRANCH + c * IN_TN:BRANCH + (c + 1) * IN_TN],
                      preferred_element_type=F32)
        for s in range(4):
            gate_ref[4 * c + s] = acc[:, s * LANES:(s + 1) * LANES].astype(BF16)


def _inproj_a(x2d, g, w, tabs, bsz, seq):
    t = x2d.shape[0]
    nblk = seq // IN_TM
    n = w.shape[1]
    return pl.pallas_call(
        _inproj_a_kernel,
        grid=(t // IN_TM,),
        in_specs=[
            pl.BlockSpec((IN_TM, D_MODEL), lambda i: (i, 0)),
            pl.BlockSpec((1, D_MODEL), lambda i: (0, 0)),
            pl.BlockSpec((D_MODEL, n), lambda i: (0, 0)),
            pl.BlockSpec((6, IN_TM, LANES), lambda i: (0, i % nblk, 0)),
        ],
        out_specs=[
            pl.BlockSpec((A_PAIRS, IN_TM, LANES), lambda i: (0, i, 0)),
            pl.BlockSpec((A_PAIRS, IN_TM, LANES), lambda i: (0, i, 0)),
            pl.BlockSpec((A_KVH, IN_TM, LANES), lambda i: (0, i, 0)),
            pl.BlockSpec((1, A_KVH * A_HEAD, IN_TM), lambda i: (i // nblk, 0, i % nblk)),
        ],
        out_shape=[
            jax.ShapeDtypeStruct((A_PAIRS, t, LANES), BF16),
            jax.ShapeDtypeStruct((A_PAIRS, t, LANES), BF16),
            jax.ShapeDtypeStruct((A_KVH, t, LANES), BF16),
            jax.ShapeDtypeStruct((bsz, A_KVH * A_HEAD, seq), BF16),
        ],
        compiler_params=_params(("parallel",)),
        name="inproj_a",
    )(x2d, g, w, tabs)


def _inproj_kernel(x_ref, g_ref, w_ref, *rest, rope_heads):
    if rope_heads:
        tab_ref, o_ref = rest
    else:
        (o_ref,) = rest
    h = _rms(x_ref[...], g_ref[...]).astype(BF16)
    n = w_ref.shape[1]
    for c0 in range(0, n, IN_TN):
        width = min(IN_TN, n - c0)
        acc = jnp.dot(h, w_ref[:, c0:c0 + width], preferred_element_type=F32)
        for lo in range(0, width, B_QK):
            head = (c0 + lo) // B_QK
            if head < rope_heads:
                t0 = 0 if head < rope_heads // 2 else 2
                cos, sin = tab_ref[t0], tab_ref[t0 + 1]
                x1 = acc[:, lo:lo + LANES]
                x2 = acc[:, lo + LANES:lo + B_QK]
                o_ref[:, c0 + lo:c0 + lo + LANES] = (x1 * cos - x2 * sin).astype(BF16)
                o_ref[:, c0 + lo + LANES:c0 + lo + B_QK] = (x2 * cos + x1 * sin).astype(BF16)
            else:
                o_ref[:, c0 + lo:c0 + lo + B_QK] = acc[:, lo:lo + B_QK].astype(BF16)


def _inproj(x2d, g, w, tabs, seq, rope_heads):
    t = x2d.shape[0]
    n = w.shape[1]
    nblk = seq // IN_TM
    in_specs = [
        pl.BlockSpec((IN_TM, D_MODEL), lambda i: (i, 0)),
        pl.BlockSpec((1, D_MODEL), lambda i: (0, 0)),
        pl.BlockSpec((D_MODEL, n), lambda i: (0, 0), pipeline_mode=pl.Buffered(1)),
    ]
    args = [x2d, g, w]
    if rope_heads:
        in_specs.append(pl.BlockSpec((4, IN_TM, LANES), lambda i: (0, i % nblk, 0)))
        args.append(tabs)
    return pl.pallas_call(
        functools.partial(_inproj_kernel, rope_heads=rope_heads),
        grid=(t // IN_TM,),
        in_specs=in_specs,
        out_specs=pl.BlockSpec((IN_TM, n), lambda i: (i, 0)),
        out_shape=jax.ShapeDtypeStruct((t, n), BF16),
        compiler_params=_params(("parallel",)),
        name="inproj",
    )(*args)


def _attn_kernel(sink_ref, q_ref, gate_ref, kp_ref, kc_ref, kn_ref, vp_ref, vc_ref, vn_ref,
                 x_ref, w_ref, fg_ref, o_ref, k_ref, v_ref, bias_ref, *, final):
    i = pl.program_id(1)
    last = pl.num_programs(1) - 1
    rows = 3 * A_WIN
    nslice = A_QBLK // A_WIN
    per_kv = A_PAIRS // A_KVH

    ones_row = (lax.broadcasted_iota(jnp.int32, (A_VPAD, A_SPAN), 0) == 0).astype(BF16)
    for h in range(A_KVH):
        k_ref[h] = jnp.concatenate([kp_ref[h], kc_ref[h], kn_ref[h]], axis=0)
        v_ref[h, :A_HEAD, :] = jnp.concatenate([vp_ref[0, h], vc_ref[0, h], vn_ref[0, h]], axis=1)
        v_ref[h, A_HEAD:, :] = ones_row

    r = lax.broadcasted_iota(jnp.int32, (A_WIN, LANES), 0)
    c = lax.broadcasted_iota(jnp.int32, (A_WIN, LANES), 1)
    neg = jnp.full((A_WIN, LANES), -1e30, F32)
    top = jnp.where(r >= c, 0.0, neg)
    bot = jnp.where(r <= c, 0.0, neg)
    bias_ref[0] = jnp.where(i == 0, neg, top)
    bias_ref[1] = top
    bias_ref[2] = bot
    bias_ref[3] = jnp.where(i == last, neg, bot)
    qlane = lax.broadcasted_iota(jnp.int32, (LANES, LANES), 1)

    units = [(g, jl, jj) for g in range(A_KVH) for jl in range(per_kv) for jj in range(nslice)]

    def scores(n):
        g, jl, jj = units[n]
        q = q_ref[g * per_kv + jl, jj * LANES:(jj + 1) * LANES, :]
        zero = jnp.zeros_like(q)
        rhs = jnp.concatenate([jnp.where(qlane < A_HEAD, q, zero),
                               jnp.where(qlane >= A_HEAD, q, zero)], axis=0)
        return lax.dot_general(k_ref[g, jj * A_WIN:jj * A_WIN + rows, :], rhs,
                               (((1,), (1,)), ((), ())), preferred_element_type=F32)

    def softmax_pv(s_t, sinks, g, jj):
        ps, extra = [], []
        for a in range(2):
            blk = s_t[:, a * LANES:(a + 1) * LANES]
            parts = [blk[:A_WIN] + bias_ref[0 if jj == 0 else 1], blk[A_WIN:2 * A_WIN],
                     blk[2 * A_WIN:] + bias_ref[3 if jj == nslice - 1 else 2]]
            m8 = functools.reduce(jnp.maximum, [part.reshape(-1, 8, LANES).max(axis=0) for part in parts])
            mx = jnp.maximum(jnp.max(m8, axis=0, keepdims=True), sinks[a])
            ps.append(jnp.concatenate([jnp.exp2(part - mx).astype(BF16) for part in parts], axis=0))
            extra.append(jnp.exp2(sinks[a] - mx))
        o_ext = jnp.dot(v_ref[g, :, jj * A_WIN:jj * A_WIN + rows], jnp.concatenate(ps, axis=1),
                        preferred_element_type=F32)
        den = o_ext[A_HEAD:A_HEAD + 1, :] + jnp.concatenate(extra, axis=1)
        return o_ext[:A_HEAD, :] / den

    pending = [scores(n) for n in range(A_LOOKAHEAD)]
    outs, gated = [], []
    y = x_ref[...]
    for n, (g, jl, jj) in enumerate(units):
        if n + A_LOOKAHEAD < len(units):
            pending.append(scores(n + A_LOOKAHEAD))
        j = g * per_kv + jl
        sinks = (sink_ref[2 * j] * LOG2E, sink_ref[2 * j + 1] * LOG2E)
        outs.append(softmax_pv(pending.pop(0), sinks, g, jj))
        if jj == nslice - 1:
            last_outs = outs[-nslice:]
            o_pair = jnp.concatenate(
                [jnp.concatenate([o[:, :LANES] for o in last_outs], axis=1),
                 jnp.concatenate([o[:, LANES:] for o in last_outs], axis=1)], axis=0).T
            gt = gate_ref[j].astype(F32)
            gated.append((o_pair * _silu(gt)).astype(BF16))
            if jl == per_kv - 1 and (g + 1) % A_OUT_GROUPS == 0:
                npair = per_kv * A_OUT_GROUPS
                wg = w_ref[(j + 1 - npair) * LANES:(j + 1) * LANES, :]
                y = y + jnp.dot(jnp.concatenate(gated[-npair:], axis=1), wg,
                                preferred_element_type=F32)
    if final:
        y = _rms(y, fg_ref[...])
    o_ref[...] = y


def _attention(x2d, q3, gate3, kk, vt, sink, w_out, fin_g, bsz, seq, final):
    t = x2d.shape[0]
    nqb = seq // A_QBLK
    nkb = seq // A_WIN
    nsl = A_QBLK // A_WIN
    vt4 = vt.reshape(bsz, A_KVH, A_HEAD, seq)
    grid_spec = pltpu.PrefetchScalarGridSpec(
        num_scalar_prefetch=1,
        grid=(bsz, nqb),
        in_specs=[
            pl.BlockSpec((A_PAIRS, A_QBLK, LANES), lambda b, i, s: (0, b * nqb + i, 0)),
            pl.BlockSpec((A_PAIRS, A_QBLK, LANES), lambda b, i, s: (0, b * nqb + i, 0)),
            pl.BlockSpec((A_KVH, A_WIN, LANES),
                         lambda b, i, s: (0, b * nkb + jnp.maximum(nsl * i - 1, 0), 0)),
            pl.BlockSpec((A_KVH, A_QBLK, LANES), lambda b, i, s: (0, b * nqb + i, 0)),
            pl.BlockSpec((A_KVH, A_WIN, LANES),
                         lambda b, i, s: (0, b * nkb + jnp.minimum(nsl * i + nsl, nkb - 1), 0)),
            pl.BlockSpec((1, A_KVH, A_HEAD, A_WIN),
                         lambda b, i, s: (b, 0, 0, jnp.maximum(nsl * i - 1, 0))),
            pl.BlockSpec((1, A_KVH, A_HEAD, A_QBLK), lambda b, i, s: (b, 0, 0, i)),
            pl.BlockSpec((1, A_KVH, A_HEAD, A_WIN),
                         lambda b, i, s: (b, 0, 0, jnp.minimum(nsl * i + nsl, nkb - 1))),
            pl.BlockSpec((A_QBLK, D_MODEL), lambda b, i, s: (b * nqb + i, 0)),
            pl.BlockSpec((BRANCH, D_MODEL), lambda b, i, s: (0, 0)),
            pl.BlockSpec((1, D_MODEL), lambda b, i, s: (0, 0)),
        ],
        out_specs=pl.BlockSpec((A_QBLK, D_MODEL), lambda b, i, s: (b * nqb + i, 0)),
        scratch_shapes=[
            pltpu.VMEM((A_KVH, A_SPAN, LANES), BF16),
            pltpu.VMEM((A_KVH, A_HEAD + A_VPAD, A_SPAN), BF16),
            pltpu.VMEM((4, A_WIN, LANES), F32),
        ],
    )
    return pl.pallas_call(
        functools.partial(_attn_kernel, final=final),
        grid_spec=grid_spec,
        out_shape=jax.ShapeDtypeStruct((t, D_MODEL), F32),
        compiler_params=_params(("parallel", "parallel")),
        name="attention",
    )(sink, q3, gate3, kk, kk, kk, vt4, vt4, vt4, x2d, w_out, fin_g)


def _retention_start(lg_ref, st_ref, intra_ref, dec_ref, backward):
    n = B_CHUNK
    ii = lax.broadcasted_iota(jnp.int32, (n, n), 0)
    jj = lax.broadcasted_iota(jnp.int32, (n, n), 1)
    col = lax.broadcasted_iota(jnp.int32, (n, LANES), 0).astype(F32)
    st_ref[...] = jnp.zeros_like(st_ref)
    for h in range(B_HEADS):
        lg = lg_ref[1 if backward else 0, h]
        if backward:
            mask = jj > ii
            dist = (jj - ii).astype(F32)
            dec_ref[h, 0] = jnp.exp((n - col) * lg)
            dec_ref[h, 1] = jnp.exp(col * lg)
        else:
            mask = ii >= jj
            dist = (ii - jj).astype(F32)
            dec_ref[h, 0] = jnp.exp((col + 1.0) * lg)
            dec_ref[h, 1] = jnp.exp((n - 1.0 - col) * lg)
        intra_ref[h] = jnp.where(mask, jnp.exp(jnp.where(mask, dist, 0.0) * lg), 0.0)


def _retention_open(q_ref, k_ref, v_ref, row):
    heads = range(B_HEADS)
    qs = [q_ref[row, :, h * B_QK:(h + 1) * B_QK] for h in heads]
    ks = [k_ref[row, :, h * B_QK:(h + 1) * B_QK] for h in heads]
    vs = [v_ref[row, :, h * B_V:(h + 1) * B_V] for h in heads]
    sc = [lax.dot_general(qs[h], ks[h], (((1,), (1,)), ((), ())), preferred_element_type=F32)
          for h in heads]
    return qs, ks, vs, sc


def _retention_heads(lg_ref, opened, st_ref, intra_ref, dec_ref, row, backward, emit):
    qs, ks, vs, sc = opened
    for h in range(B_HEADS):
        q_dec = jnp.concatenate([dec_ref[h, 0]] * (B_QK // LANES), axis=1)
        k_dec = jnp.concatenate([dec_ref[h, 1]] * (B_QK // LANES), axis=1)
        c_dec = jnp.exp(jnp.full((1, B_V), B_CHUNK * lg_ref[1 if backward else 0, h], F32))
        st = st_ref[row, h]
        lhs = jnp.concatenate([(sc[h] * intra_ref[h]).astype(BF16),
                               (qs[h].astype(F32) * q_dec).astype(BF16)], axis=1)
        rhs = jnp.concatenate([vs[h], st.astype(BF16)], axis=0)
        out = jnp.dot(lhs, rhs, preferred_element_type=F32)
        kd_t = (ks[h].astype(F32) * k_dec).T.astype(BF16)
        st_ref[row, h] = st * c_dec + jnp.dot(kd_t, vs[h], preferred_element_type=F32)
        emit(h, out)


def _retention_bwd_kernel(lg_ref, q_ref, k_ref, v_ref, o_ref, st_ref, intra_ref, dec_ref):
    @pl.when(pl.program_id(1) == 0)
    def _():
        _retention_start(lg_ref, st_ref, intra_ref, dec_ref, True)

    rows = range(q_ref.shape[0])
    opened = [_retention_open(q_ref, k_ref, v_ref, row) for row in rows]
    for row in rows:
        def emit(h, o, row=row):
            o_ref[row, :, h * B_V:(h + 1) * B_V] = o.astype(BF16)

        _retention_heads(lg_ref, opened[row], st_ref, intra_ref, dec_ref, row, True, emit)


def _retention_fwd_kernel(lg_ref, q_ref, k_ref, v_ref, gate_ref, ob_ref, x_ref, w_ref, o_ref,
                          st_ref, intra_ref, dec_ref):
    @pl.when(pl.program_id(1) == 0)
    def _():
        _retention_start(lg_ref, st_ref, intra_ref, dec_ref, False)

    rows = range(q_ref.shape[0])
    opened = [_retention_open(q_ref, k_ref, v_ref, row) for row in rows]
    gated = []
    for row in rows:
        parts = []

        def emit(h, o_f, row=row, parts=parts):
            sl = slice(h * B_V, (h + 1) * B_V)
            o = o_f + ob_ref[row, :, sl].astype(F32)
            on = o * lax.rsqrt(jnp.mean(o * o, axis=-1, keepdims=True) + NORM_EPS)
            parts.append((on * _silu(gate_ref[row, :, sl].astype(F32))).astype(BF16))

        _retention_heads(lg_ref, opened[row], st_ref, intra_ref, dec_ref, row, False, emit)
        gated.append(jnp.concatenate(parts, axis=1))
    for row in rows:
        o_ref[row] = x_ref[row] + jnp.dot(gated[row], w_ref[...], preferred_element_type=F32)


def _retention(proj3, log_g, x3, w_out):
    bsz, seq, _ = proj3.shape
    nc = seq // B_CHUNK
    nqk = B_HEADS * B_QK
    rows = B_ROWS if bsz % B_ROWS == 0 else 1
    smem = pl.BlockSpec(memory_space=pltpu.SMEM)
    state = [pltpu.VMEM((rows, B_HEADS, B_QK, B_V), F32),
             pltpu.VMEM((B_HEADS, B_CHUNK, B_CHUNK), F32),
             pltpu.VMEM((B_HEADS, 2, B_CHUNK, LANES), F32)]

    def blk(width, chunk, col):
        return pl.BlockSpec((rows, B_CHUNK, width), lambda b, t: (b, chunk(t), col))

    def rev(t):
        return nc - 1 - t

    def fwd(t):
        return t

    o_b = pl.pallas_call(
        _retention_bwd_kernel,
        grid=(bsz // rows, nc),
        in_specs=[smem, blk(nqk, rev, 0), blk(nqk, rev, 1), blk(BRANCH, rev, 1)],
        out_specs=blk(BRANCH, rev, 0),
        out_shape=jax.ShapeDtypeStruct((bsz, seq, BRANCH), BF16),
        scratch_shapes=state,
        compiler_params=_params(("parallel", "arbitrary")),
        name="retention_bwd",
    )(log_g, proj3, proj3, proj3)
    return pl.pallas_call(
        _retention_fwd_kernel,
        grid=(bsz // rows, nc),
        in_specs=[smem, blk(nqk, fwd, 0), blk(nqk, fwd, 1), blk(BRANCH, fwd, 1),
                  blk(BRANCH, fwd, 2), blk(BRANCH, fwd, 0), blk(D_MODEL, fwd, 0),
                  pl.BlockSpec((BRANCH, D_MODEL), lambda b, t: (0, 0))],
        out_specs=blk(D_MODEL, fwd, 0),
        out_shape=jax.ShapeDtypeStruct((bsz, seq, D_MODEL), F32),
        scratch_shapes=state,
        compiler_params=_params(("parallel", "arbitrary")),
        name="retention_fwd",
    )(log_g, proj3, proj3, proj3, proj3, o_b, x3, w_out)


C_HALF = C_GDIM // 2
C_MAIN = C_GROUPS * C_HALF
C_FOLD = 2 * C_HALF + LANES
C_NYQ_ROWS = 16


def _fold_kernel(w_ref, cs_ref, a_ref, b_ref, sp_ref):
    r = jnp.dot(w_ref[...], cs_ref[0], preferred_element_type=F32,
                precision=lax.Precision.HIGHEST)
    a_ref[...] = r[:, :C_HALF].astype(BF16)
    b_ref[...] = r[:, C_HALF:2 * C_HALF].astype(BF16)

    @pl.when(pl.program_id(0) == 0)
    def _():
        sp_ref[...] = jnp.zeros_like(sp_ref)

    sp_ref[...] += r[:, 2 * C_HALF:]


def _fold_channel_dft(w_u, cs):
    spec = pl.BlockSpec((D_MODEL, C_HALF), lambda g: (0, g))
    return pl.pallas_call(
        _fold_kernel,
        grid=(C_GROUPS,),
        in_specs=[pl.BlockSpec((D_MODEL, C_GDIM), lambda g: (0, g)),
                  pl.BlockSpec((1, C_GDIM, C_FOLD), lambda g: (g, 0, 0))],
        out_specs=[spec, spec, pl.BlockSpec((D_MODEL, LANES), lambda g: (0, 0))],
        out_shape=[jax.ShapeDtypeStruct((D_MODEL, C_MAIN), BF16)] * 2
        + [jax.ShapeDtypeStruct((D_MODEL, LANES), F32)],
        compiler_params=_params(("arbitrary",)),
        name="fold_channel_dft",
    )(w_u, cs)


DFT_ROWS = 512


def _dft_mix_kernel(cm_ref, sm_ref, a_ref, b_ref, gm_ref, gp_ref, sp_ref, gsp_ref,
                    om_ref, op_ref, osp_ref):
    n = cm_ref.shape[0]
    rc = min(n, DFT_ROWS)
    a = a_ref[0]
    b = b_ref[0]
    for r in range(n // rc):
        rows = slice(r * rc, (r + 1) * rc)
        ea = jnp.dot(cm_ref[rows, :], a, preferred_element_type=F32)
        eb = jnp.dot(sm_ref[rows, :], b, preferred_element_type=F32)
        om_ref[0, rows, :] = ((ea - eb) * _silu(gm_ref[0, rows, :].astype(F32))).astype(BF16)
        op_ref[0, rows, :] = ((ea + eb) * _silu(gp_ref[0, rows, :].astype(F32))).astype(BF16)

    @pl.when(pl.program_id(1) == 0)
    def _():
        sp_t = sp_ref[0].astype(F32).T[:C_NYQ_ROWS, :].astype(BF16)
        es_t = jnp.dot(sp_t, cm_ref[...], preferred_element_type=F32)
        es = jnp.concatenate([es_t, jnp.zeros((LANES - C_NYQ_ROWS, n), F32)], axis=0).T
        osp_ref[0] = (es * _silu(gsp_ref[0].astype(F32))).astype(BF16)


def _trig(m, period, scale):
    ang = (2.0 * math.pi / period) * (m % period).astype(F32)
    return jnp.cos(ang) * scale, jnp.sin(ang) * scale


def _fourier_mix(proj3):
    bsz, seq, _ = proj3.shape
    n = jnp.arange(seq, dtype=jnp.int32)[None, None, :]
    hi = jnp.arange(seq // LANES, dtype=jnp.int32)[:, None, None] * LANES
    lo = jnp.arange(LANES, dtype=jnp.int32)[None, :, None]
    ca, sa = _trig(hi * n, seq, seq ** -0.5)
    cb, sb = _trig(lo * n, seq, 1.0)
    cm = (ca * cb - sa * sb).reshape(seq, seq)
    sm = (sa * cb + ca * sb).reshape(seq, seq)
    tc = 512
    nct = C_MAIN // tc
    mspec = pl.BlockSpec((seq, seq), lambda b, c: (0, 0))

    def dspec(width, idx):
        return pl.BlockSpec((1, seq, width), lambda b, c: (b, 0, idx(c)))

    sp_blk = 4 * C_MAIN // LANES
    return pl.pallas_call(
        _dft_mix_kernel,
        grid=(bsz, nct),
        in_specs=[mspec, mspec,
                  dspec(tc, lambda c: c), dspec(tc, lambda c: nct + c),
                  dspec(tc, lambda c: 2 * nct + c), dspec(tc, lambda c: 3 * nct + c),
                  dspec(LANES, lambda c: sp_blk), dspec(LANES, lambda c: sp_blk + 1)],
        out_specs=[dspec(tc, lambda c: c), dspec(tc, lambda c: c), dspec(LANES, lambda c: 0)],
        out_shape=[jax.ShapeDtypeStruct((bsz, seq, C_MAIN), BF16)] * 2
        + [jax.ShapeDtypeStruct((bsz, seq, LANES), BF16)],
        compiler_params=_params(("parallel", "arbitrary")),
        name="dft_mix",
    )(cm.astype(BF16), sm.astype(BF16), proj3, proj3, proj3, proj3, proj3, proj3)


C_GRP = 8
C_SUB = 64
C_ROWS = 256
C_NCHUNK = 1024


def _inproj_perm_kernel(x_ref, g_ref, w_ref, o_ref, xs_ref):
    rh = x_ref.shape[1]
    n = w_ref.shape[1]
    for i in range(C_GRP):
        xs_ref[i * rh:(i + 1) * rh, :] = x_ref[0, :, i, :]
    per_dot = C_ROWS // rh
    for j in range(C_GRP // per_dot):
        h = _rms(xs_ref[j * C_ROWS:(j + 1) * C_ROWS, :], g_ref[...]).astype(BF16)
        for c0 in range(0, n, C_NCHUNK):
            cols = slice(c0, min(c0 + C_NCHUNK, n))
            acc = jnp.dot(h, w_ref[:, cols], preferred_element_type=F32)
            for i in range(per_dot):
                o_ref[0, j * per_dot + i, :, cols] = acc[i * rh:(i + 1) * rh, :].astype(BF16)


def _inproj_perm(x4, g, w):
    bsz, r, _, _ = x4.shape
    n = w.shape[1]
    rh = min(r, C_SUB)
    return pl.pallas_call(
        _inproj_perm_kernel,
        grid=(bsz, r // C_GRP, r // rh),
        in_specs=[
            pl.BlockSpec((1, rh, C_GRP, D_MODEL), lambda b, m, s: (b, s, m, 0)),
            pl.BlockSpec((1, D_MODEL), lambda b, m, s: (0, 0)),
            pl.BlockSpec((D_MODEL, n), lambda b, m, s: (0, 0)),
        ],
        out_specs=pl.BlockSpec((1, C_GRP, rh, n), lambda b, m, s: (b, m, s, 0)),
        out_shape=jax.ShapeDtypeStruct((bsz, r, r, n), BF16),
        scratch_shapes=[pltpu.VMEM((C_GRP * rh, D_MODEL), F32)],
        compiler_params=_params(("parallel", "parallel", "parallel")),
        name="inproj_perm",
    )(x4, g, w)


def _dft_stage1_kernel(d_ref, x_ref, sp_ref, yr_ref, yi_ref, ysr_ref, ysi_ref):
    r = x_ref.shape[2]
    for i in range(C_GRP):
        dc = d_ref[i, :r, :]
        ds = d_ref[i, r:, :]
        yr_ref[0, :, i, :] = jnp.dot(dc, x_ref[0, i], preferred_element_type=F32)
        yi_ref[0, :, i, :] = jnp.dot(ds, x_ref[0, i], preferred_element_type=F32)
        ysr_ref[0, :, i, :] = jnp.dot(dc, sp_ref[0, i], preferred_element_type=F32)
        ysi_ref[0, :, i, :] = jnp.dot(ds, sp_ref[0, i], preferred_element_type=F32)


def _dft_stage1(proj4, dtab):
    bsz, r, _, _ = proj4.shape
    tc = 2 * C_MAIN
    nct = 2 * C_MAIN // tc

    def yspec(width, idx):
        return pl.BlockSpec((1, r, C_GRP, width), lambda b, m, c: (b, 0, m, idx(c)))

    return pl.pallas_call(
        _dft_stage1_kernel,
        grid=(bsz, r // C_GRP, nct),
        in_specs=[
            pl.BlockSpec((C_GRP, 2 * r, r), lambda b, m, c: (m, 0, 0)),
            pl.BlockSpec((1, C_GRP, r, tc), lambda b, m, c: (b, m, 0, c)),
            pl.BlockSpec((1, C_GRP, r, LANES), lambda b, m, c: (b, m, 0, 4 * C_MAIN // LANES)),
        ],
        out_specs=[yspec(tc, lambda c: c), yspec(tc, lambda c: c),
                   yspec(LANES, lambda c: 0), yspec(LANES, lambda c: 0)],
        out_shape=[jax.ShapeDtypeStruct((bsz, r, r, 2 * C_MAIN), F32)] * 2
        + [jax.ShapeDtypeStruct((bsz, r, r, LANES), F32)] * 2,
        compiler_params=_params(("parallel", "parallel", "arbitrary")),
        name="dft_stage1",
    )(dtab, proj4, proj4)


def _dft_stage2_kernel(c1_ref, s1_ref, yra_ref, yia_ref, yrb_ref, yib_ref, gm_ref, gp_ref,
                       wm_ref, wp_ref, ysr_ref, ysi_ref, gsp_ref, wsp_ref, x_ref, o_ref, acc_ref):
    c = pl.program_id(2)
    r = c1_ref.shape[0]
    c1 = c1_ref[...]
    s1 = s1_ref[...]

    def dot(m, y):
        return jnp.dot(m, y.astype(BF16), preferred_element_type=F32)

    @pl.when(c == 0)
    def _():
        parts = []
        for i in range(C_GRP):
            es = dot(c1, ysr_ref[0, i]) - dot(s1, ysi_ref[0, i])
            parts.append((es * _silu(gsp_ref[0, i].astype(F32))).astype(BF16))
        nyq = jnp.dot(jnp.concatenate(parts, axis=0), wsp_ref[...], preferred_element_type=F32)
        for i in range(C_GRP):
            acc_ref[i * r:(i + 1) * r, :] = x_ref[0, :, i, :] + nyq[i * r:(i + 1) * r, :]

    minus, plus = [], []
    for i in range(C_GRP):
        ea = dot(c1, yra_ref[0, i]) - dot(s1, yia_ref[0, i])
        eb = dot(c1, yib_ref[0, i]) + dot(s1, yrb_ref[0, i])
        minus.append(((ea - eb) * _silu(gm_ref[0, i].astype(F32))).astype(BF16))
        plus.append(((ea + eb) * _silu(gp_ref[0, i].astype(F32))).astype(BF16))
    acc_ref[...] += (jnp.dot(jnp.concatenate(minus, axis=0), wm_ref[...], preferred_element_type=F32)
                     + jnp.dot(jnp.concatenate(plus, axis=0), wp_ref[...], preferred_element_type=F32))

    @pl.when(c == pl.num_programs(2) - 1)
    def _():
        for i in range(C_GRP):
            o_ref[0, :, i, :] = acc_ref[i * r:(i + 1) * r, :]


def _dft_stage2(c1, s1, y_r, y_i, y_sr, y_si, proj4, w_out, x4):
    bsz, r, _, _ = x4.shape
    tc = 512
    nct = C_MAIN // tc
    mspec = pl.BlockSpec((r, r), lambda b, m, c: (0, 0))

    def slab(width, idx):
        return pl.BlockSpec((1, C_GRP, r, width), lambda b, m, c: (b, m, 0, idx(c)))

    def wspec(rows, idx):
        return pl.BlockSpec((rows, D_MODEL), lambda b, m, c: (idx(c), 0))

    xspec = pl.BlockSpec((1, r, C_GRP, D_MODEL), lambda b, m, c: (b, 0, m, 0))
    sp_blk = 4 * C_MAIN // LANES
    return pl.pallas_call(
        _dft_stage2_kernel,
        grid=(bsz, r // C_GRP, nct),
        in_specs=[
            mspec, mspec,
            slab(tc, lambda c: c), slab(tc, lambda c: c),
            slab(tc, lambda c: nct + c), slab(tc, lambda c: nct + c),
            slab(tc, lambda c: 2 * nct + c), slab(tc, lambda c: 3 * nct + c),
            wspec(tc, lambda c: c), wspec(tc, lambda c: nct + c),
            slab(LANES, lambda c: 0), slab(LANES, lambda c: 0),
            slab(LANES, lambda c: sp_blk + 1),
            wspec(LANES, lambda c: 2 * C_MAIN // LANES),
            xspec,
        ],
        out_specs=xspec,
        out_shape=jax.ShapeDtypeStruct(x4.shape, F32),
        scratch_shapes=[pltpu.VMEM((C_GRP * r, D_MODEL), F32)],
        compiler_params=_params(("parallel", "parallel", "arbitrary")),
        name="dft_stage2",
    )(c1, s1, y_r, y_i, y_r, y_i, proj4, proj4, w_out, w_out, y_sr, y_si, proj4, w_out, x4)


def _fourier_long(x2d, g, wc, w_out, bsz, seq):
    r = math.isqrt(seq)
    assert r * r == seq and r % C_GRP == 0
    x4 = x2d.reshape(bsz, r, r, D_MODEL)
    proj4 = _inproj_perm(x4, g, wc)
    idx = jnp.arange(r, dtype=jnp.int32)
    k2 = idx[None, :, None]
    ca, sa = _trig(k2 * idx[:, None, None], seq, r ** -0.5)
    cb, sb = _trig(k2 * (r * idx[None, None, :]), seq, 1.0)
    dc, ds = ca * cb - sa * sb, sa * cb + ca * sb
    dtab = jnp.concatenate([dc, ds], axis=1).astype(BF16)
    y_r, y_i, y_sr, y_si = _dft_stage1(proj4, dtab)
    c1, s1 = _trig(idx[:, None] * idx[None, :], r, r ** -0.5)
    out = _dft_stage2(c1.astype(BF16), s1.astype(BF16), y_r, y_i, y_sr, y_si, proj4, w_out, x4)
    return out.reshape(bsz * seq, D_MODEL)


OUT_TM = 512


def _outproj_kernel(mm_ref, mp_ref, ms_ref, x_ref, w_ref, o_ref):
    acc = jnp.dot(mm_ref[...], w_ref[:C_MAIN, :], preferred_element_type=F32)
    acc = acc + jnp.dot(mp_ref[...], w_ref[C_MAIN:2 * C_MAIN, :], preferred_element_type=F32)
    acc = acc + jnp.dot(ms_ref[...], w_ref[2 * C_MAIN:, :], preferred_element_type=F32)
    o_ref[...] = x_ref[...] + acc


def _outproj(minus2d, plus2d, nyq2d, x2d, w_out):
    t = x2d.shape[0]
    return pl.pallas_call(
        _outproj_kernel,
        grid=(t // OUT_TM,),
        in_specs=[
            pl.BlockSpec((OUT_TM, C_MAIN), lambda i: (i, 0)),
            pl.BlockSpec((OUT_TM, C_MAIN), lambda i: (i, 0)),
            pl.BlockSpec((OUT_TM, LANES), lambda i: (i, 0)),
            pl.BlockSpec((OUT_TM, D_MODEL), lambda i: (i, 0)),
            pl.BlockSpec((2 * C_MAIN + LANES, D_MODEL), lambda i: (0, 0)),
        ],
        out_specs=pl.BlockSpec((OUT_TM, D_MODEL), lambda i: (i, 0)),
        out_shape=jax.ShapeDtypeStruct((t, D_MODEL), F32),
        compiler_params=_params(("parallel",)),
        name="outproj",
    )(minus2d, plus2d, nyq2d, x2d, w_out)


def _rope_tables_a(seq):
    half = A_ROT // 2
    inv_freq = jnp.exp(-(jnp.arange(half, dtype=F32) * (2.0 / A_ROT)) * math.log(A_THETA))
    ang = jnp.arange(seq, dtype=F32)[:, None] * inv_freq[None, :]
    cos, sin = jnp.cos(ang), jnp.sin(ang)
    d = jnp.arange(LANES) % A_HEAD
    f = d % half
    cm = jnp.where(d[None, :] < A_ROT, cos[:, f], 1.0)
    s1 = jnp.where(d[None, :] < half, -sin[:, f], 0.0)
    s2 = jnp.where((d[None, :] >= half) & (d[None, :] < A_ROT), sin[:, f], 0.0)
    qs = A_HEAD ** -0.5 * LOG2E
    return jnp.stack([cm * qs, s1 * qs, s2 * qs, cm, s1, s2]).astype(F32)


def _rope_tables_b(seq):
    half = B_QK // 2
    inv_freq = jnp.exp(-(jnp.arange(half, dtype=F32) * (2.0 / B_QK)) * math.log(B_THETA))
    ang = jnp.arange(seq, dtype=F32)[:, None] * inv_freq[None, :]
    cos, sin = jnp.cos(ang), jnp.sin(ang)
    ks = B_QK ** -0.5
    return jnp.stack([cos, sin, cos * ks, sin * ks]).astype(F32)


def _weights_a(w_in):
    nq = A_QH * A_HEAD
    nkv = A_KVH * A_HEAD
    q = w_in[:, :nq]
    kv = w_in[:, nq:nq + 2 * nkv]
    gate = w_in[:, nq + 2 * nkv:]
    return jnp.concatenate([q, gate, kv], axis=1).astype(BF16)


def _c_index():
    g = jnp.arange(C_GROUPS, dtype=jnp.int32)[:, None] * C_GDIM
    m = jnp.arange(C_HALF, dtype=jnp.int32)[None, :]
    minus = (g + m).reshape(-1)
    plus = (g + jnp.where(m == 0, 0, C_GDIM - m)).reshape(-1)
    nyq = (g + C_HALF).reshape(-1)
    return minus, plus, nyq, jnp.tile(m == 0, (C_GROUPS, 1)).reshape(-1)


def _weights_c(w_in):
    ch = jnp.arange(C_GDIM, dtype=jnp.int32)
    m = jnp.arange(C_HALF, dtype=jnp.int32)
    cm, sm = _trig(ch[:, None] * m[None, :], C_GDIM, C_GDIM ** -0.5)
    nyq, _ = _trig(ch * C_HALF, C_GDIM, C_GDIM ** -0.5)
    lane = jnp.arange(LANES)[None, None, :] == jnp.arange(C_GROUPS)[:, None, None]
    nyq = jnp.where(lane, nyq[None, :, None], 0.0)
    cs = jnp.concatenate([jnp.broadcast_to(cm, (C_GROUPS,) + cm.shape),
                          jnp.broadcast_to(sm, (C_GROUPS,) + sm.shape), nyq], axis=2)
    w_a, w_b, w_sp = _fold_channel_dft(w_in[:, :BRANCH], cs)
    minus, plus, nyq_idx, _ = _c_index()
    gate = w_in[:, BRANCH:]
    g_sp = jnp.zeros((D_MODEL, LANES), F32).at[:, :C_GROUPS].set(gate[:, nyq_idx])
    return jnp.concatenate([w_a, w_b, gate[:, minus].astype(BF16), gate[:, plus].astype(BF16),
                            w_sp.astype(BF16), g_sp.astype(BF16)], axis=1)


def _weights_c_out(w_out):
    minus, plus, nyq_idx, dup = _c_index()
    w_plus = jnp.where(dup[:, None], 0.0, w_out[plus])
    w_sp = jnp.zeros((LANES, D_MODEL), F32).at[:C_GROUPS].set(w_out[nyq_idx])
    return jnp.concatenate([w_out[minus], w_plus, w_sp], axis=0).astype(BF16)


def _trunk(x, norm_g, fin_g, wa, a_sink, a_w_out, wb, log_g, b_w_out, wc, c_w_out):
    bsz, seq, _ = x.shape
    t = bsz * seq
    x2d = x.reshape(t, D_MODEL)
    tabs_a = _rope_tables_a(seq)
    tabs_b = _rope_tables_b(seq)
    fin = fin_g.reshape(1, D_MODEL)

    def layer_a(x2d, layer, j, final):
        q3, gate3, kk, vt = _inproj_a(x2d, norm_g[layer].reshape(1, D_MODEL), wa[j], tabs_a, bsz, seq)
        return _attention(x2d, q3, gate3, kk, vt, a_sink[j], a_w_out[j], fin, bsz, seq, final)

    x2d = layer_a(x2d, 0, 0, False)

    proj = _inproj(x2d, norm_g[1].reshape(1, D_MODEL), wb, tabs_b, seq, 2 * B_HEADS)
    x2d = _retention(proj.reshape(bsz, seq, -1), log_g, x2d.reshape(bsz, seq, D_MODEL),
                     b_w_out).reshape(t, D_MODEL)

    if seq <= C_DIRECT_MAX:
        proj = _inproj(x2d, norm_g[2].reshape(1, D_MODEL), wc, None, seq, 0)
        minus, plus, nyq = _fourier_mix(proj.reshape(bsz, seq, -1))
        x2d = _outproj(minus.reshape(t, C_MAIN), plus.reshape(t, C_MAIN), nyq.reshape(t, LANES),
                       x2d, c_w_out)
    else:
        x2d = _fourier_long(x2d, norm_g[2].reshape(1, D_MODEL), wc, c_w_out, bsz, seq)

    x2d = layer_a(x2d, 3, 1, True)
    return x2d.reshape(bsz, seq, D_MODEL)


def kernel(x_prompt, x_sample, norm_g, final_norm_g, a_w_in, a_sink, a_w_out, b_w_in, b_decay,
           b_w_out, c_w_in, c_w_out):
    wa = [_weights_a(a_w_in[j]) for j in range(a_w_in.shape[0])]
    a_out = [a_w_out[j].astype(BF16) for j in range(a_w_out.shape[0])]
    wb = b_w_in[0].astype(BF16)
    log_g = jax.nn.log_sigmoid(b_decay[0].astype(F32))
    wc = _weights_c(c_w_in[0])
    args = (norm_g, final_norm_g, wa, a_sink, a_out, wb, log_g, b_w_out[0].astype(BF16),
            wc, _weights_c_out(c_w_out[0]))
    return (_trunk(x_prompt, *args), _trunk(x_sample, *args))
```

```python
import functools
import math

import jax
import jax.numpy as jnp
from jax import lax
from jax.experimental import pallas as pl
from jax.experimental.pallas import tpu as pltpu

D_MODEL = 1024
BRANCH = 2048
NORM_EPS = 1e-6

A_HEAD = 64
A_QH = 32
A_KVH = 4
A_ROT = 16
A_THETA = 500000.0
A_WIN = 128
A_QBLK = 1024
A_SPAN = A_QBLK + 2 * A_WIN
A_PAIRS = A_QH // 2
A_LOOKAHEAD = 8
A_OUT_GROUPS = 2
A_VPAD = 16
LOG2E = math.log2(math.e)

B_QK = 256
B_HEADS = 4
B_V = 512
B_CHUNK = 256
B_ROWS = 2
B_THETA = 10000.0

C_GROUPS = 4
C_GDIM = 512
C_DIRECT_MAX = 2048

LANES = 128
VMEM_LIMIT = 56 * 1024 * 1024

BF16 = jnp.bfloat16
F32 = jnp.float32


def _params(sem):
    return pltpu.CompilerParams(dimension_semantics=sem, vmem_limit_bytes=VMEM_LIMIT)


def _silu(x):
    half = 0.5 * x
    return half + half * jnp.tanh(half)


def _rms(x, g):
    ms = jnp.mean(x * x, axis=-1, keepdims=True)
    return (x * lax.rsqrt(ms + NORM_EPS)) * g


IN_TM = 1024
IN_TN = 512


def _inproj_a_kernel(x_ref, g_ref, w_ref, tab_ref, q_ref, gate_ref, kk_ref, vt_ref):
    h = _rms(x_ref[...], g_ref[...]).astype(BF16)

    def rope(a, t0):
        return (a * tab_ref[t0] + pltpu.roll(a, LANES - A_ROT // 2, 1) * tab_ref[t0 + 1]
                + pltpu.roll(a, A_ROT // 2, 1) * tab_ref[t0 + 2])

    acc = jnp.dot(h, w_ref[:, 2 * BRANCH:], preferred_element_type=F32)
    nkv = A_KVH * A_HEAD
    lane = lax.broadcasted_iota(jnp.int32, (IN_TM, LANES), 1)
    for p in range(nkv // LANES):
        a = rope(acc[:, p * LANES:(p + 1) * LANES], 3)
        swapped = pltpu.roll(a, A_HEAD, 1)
        kk_ref[2 * p] = jnp.where(lane < A_HEAD, a, swapped).astype(BF16)
        kk_ref[2 * p + 1] = jnp.where(lane >= A_HEAD, a, swapped).astype(BF16)
    vt_ref[0] = acc[:, nkv:].T.astype(BF16)
    for c in range(4):
        acc = jnp.dot(h, w_ref[:, c * IN_TN:(c + 1) * IN_TN], preferred_element_type=F32)
        for s in range(4):
            q_ref[4 * c + s] = rope(acc[:, s * LANES:(s + 1) * LANES], 0).astype(BF16)
    for c in range(4):
        acc = jnp.dot(h, w_ref[:, BRANCH + c * IN_TN:BRANCH + (c + 1) * IN_TN],
                      preferred_element_type=F32)
        for s in range(4):
            gate_ref[4 * c + s] = acc[:, s * LANES:(s + 1) * LANES].astype(BF16)


def _inproj_a(x2d, g, w, tabs, bsz, seq):
    t = x2d.shape[0]
    nblk = seq // IN_TM
    n = w.shape[1]
    return pl.pallas_call(
        _inproj_a_kernel,
        grid=(t // IN_TM,),
        in_specs=[
            pl.BlockSpec((IN_TM, D_MODEL), lambda i: (i, 0)),
            pl.BlockSpec((1, D_MODEL), lambda i: (0, 0), pipeline_mode=pl.Buffered(1)),
            pl.BlockSpec((D_MODEL, n), lambda i: (0, 0), pipeline_mode=pl.Buffered(1)),
            pl.BlockSpec((6, IN_TM, LANES), lambda i: (0, i % nblk, 0)),
        ],
        out_specs=[
            pl.BlockSpec((A_PAIRS, IN_TM, LANES), lambda i: (0, i, 0)),
            pl.BlockSpec((A_PAIRS, IN_TM, LANES), lambda i: (0, i, 0)),
            pl.BlockSpec((A_KVH, IN_TM, LANES), lambda i: (0, i, 0)),
            pl.BlockSpec((1, A_KVH * A_HEAD, IN_TM), lambda i: (i // nblk, 0, i % nblk)),
        ],
        out_shape=[
            jax.ShapeDtypeStruct((A_PAIRS, t, LANES), BF16),
            jax.ShapeDtypeStruct((A_PAIRS, t, LANES), BF16),
            jax.ShapeDtypeStruct((A_KVH, t, LANES), BF16),
            jax.ShapeDtypeStruct((bsz, A_KVH * A_HEAD, seq), BF16),
        ],
        compiler_params=_params(("parallel",)),
        name="inproj_a",
    )(x2d, g, w, tabs)


def _inproj_kernel(x_ref, g_ref, w_ref, *rest, rope_heads):
    if rope_heads:
        tab_ref, o_ref = rest
    else:
        (o_ref,) = rest
    h = _rms(x_ref[...], g_ref[...]).astype(BF16)
    n = w_ref.shape[1]
    for c0 in range(0, n, IN_TN):
        width = min(IN_TN, n - c0)
        acc = jnp.dot(h, w_ref[:, c0:c0 + width], preferred_element_type=F32)
        for lo in range(0, width, B_QK):
            head = (c0 + lo) // B_QK
            if head < rope_heads:
                t0 = 0 if head < rope_heads // 2 else 2
                cos, sin = tab_ref[t0], tab_ref[t0 + 1]
                x1 = acc[:, lo:lo + LANES]
                x2 = acc[:, lo + LANES:lo + B_QK]
                o_ref[:, c0 + lo:c0 + lo + LANES] = (x1 * cos - x2 * sin).astype(BF16)
                o_ref[:, c0 + lo + LANES:c0 + lo + B_QK] = (x2 * cos + x1 * sin).astype(BF16)
            else:
                o_ref[:, c0 + lo:c0 + lo + B_QK] = acc[:, lo:lo + B_QK].astype(BF16)


def _inproj(x2d, g, w, tabs, seq, rope_heads):
    t = x2d.shape[0]
    n = w.shape[1]
    nblk = seq // IN_TM
    in_specs = [
        pl.BlockSpec((IN_TM, D_MODEL), lambda i: (i, 0)),
        pl.BlockSpec((1, D_MODEL), lambda i: (0, 0), pipeline_mode=pl.Buffered(1)),
        pl.BlockSpec((D_MODEL, n), lambda i: (0, 0), pipeline_mode=pl.Buffered(1)),
    ]
    args = [x2d, g, w]
    if rope_heads:
        in_specs.append(pl.BlockSpec((4, IN_TM, LANES), lambda i: (0, i % nblk, 0)))
        args.append(tabs)
    return pl.pallas_call(
        functools.partial(_inproj_kernel, rope_heads=rope_heads),
        grid=(t // IN_TM,),
        in_specs=in_specs,
        out_specs=pl.BlockSpec((IN_TM, n), lambda i: (i, 0)),
        out_shape=jax.ShapeDtypeStruct((t, n), BF16),
        compiler_params=_params(("parallel",)),
        name="inproj",
    )(*args)


def _attn_kernel(sink_ref, q_ref, gate_ref, kp_ref, kc_ref, kn_ref, vp_ref, vc_ref, vn_ref,
                 x_ref, w_ref, fg_ref, o_ref, k_ref, v_ref, bias_ref, *, final):
    i = pl.program_id(1)
    last = pl.num_programs(1) - 1
    rows = 3 * A_WIN
    nslice = A_QBLK // A_WIN
    per_kv = A_PAIRS // A_KVH

    ones_row = (lax.broadcasted_iota(jnp.int32, (A_VPAD, A_SPAN), 0) == 0).astype(BF16)
    for h in range(A_KVH):
        k_ref[h] = jnp.concatenate([kp_ref[h], kc_ref[h], kn_ref[h]], axis=0)
        v_ref[h, :A_HEAD, :] = jnp.concatenate([vp_ref[0, h], vc_ref[0, h], vn_ref[0, h]], axis=1)
        v_ref[h, A_HEAD:, :] = ones_row

    r = lax.broadcasted_iota(jnp.int32, (A_WIN, LANES), 0)
    c = lax.broadcasted_iota(jnp.int32, (A_WIN, LANES), 1)
    neg = jnp.full((A_WIN, LANES), -1e30, F32)
    top = jnp.where(r >= c, 0.0, neg)
    bot = jnp.where(r <= c, 0.0, neg)
    bias_ref[0] = jnp.where(i == 0, neg, top)
    bias_ref[1] = top
    bias_ref[2] = bot
    bias_ref[3] = jnp.where(i == last, neg, bot)
    qlane = lax.broadcasted_iota(jnp.int32, (LANES, LANES), 1)

    units = [(g, jl, jj) for g in range(A_KVH) for jl in range(per_kv) for jj in range(nslice)]

    def scores(n):
        g, jl, jj = units[n]
        q = q_ref[g * per_kv + jl, jj * LANES:(jj + 1) * LANES, :]
        zero = jnp.zeros_like(q)
        rhs = jnp.concatenate([jnp.where(qlane < A_HEAD, q, zero),
                               jnp.where(qlane >= A_HEAD, q, zero)], axis=0)
        return lax.dot_general(k_ref[g, jj * A_WIN:jj * A_WIN + rows, :], rhs,
                               (((1,), (1,)), ((), ())), preferred_element_type=F32)

    def softmax_pv(s_t, sinks, g, jj):
        ps, extra = [], []
        for a in range(2):
            blk = s_t[:, a * LANES:(a + 1) * LANES]
            parts = [blk[:A_WIN] + bias_ref[0 if jj == 0 else 1], blk[A_WIN:2 * A_WIN],
                     blk[2 * A_WIN:] + bias_ref[3 if jj == nslice - 1 else 2]]
            m8 = functools.reduce(jnp.maximum, [part.reshape(-1, 8, LANES).max(axis=0) for part in parts])
            mx = jnp.maximum(jnp.max(m8, axis=0, keepdims=True), sinks[a])
            ps.append(jnp.concatenate([jnp.exp2(part - mx).astype(BF16) for part in parts], axis=0))
            extra.append(jnp.exp2(sinks[a] - mx))
        o_ext = jnp.dot(v_ref[g, :, jj * A_WIN:jj * A_WIN + rows], jnp.concatenate(ps, axis=1),
                        preferred_element_type=F32)
        den = o_ext[A_HEAD:A_HEAD + 1, :] + jnp.concatenate(extra, axis=1)
        return o_ext[:A_HEAD, :] / den

    pending = [scores(n) for n in range(A_LOOKAHEAD)]
    outs, gated = [], []
    y = x_ref[...]
    for n, (g, jl, jj) in enumerate(units):
        if n + A_LOOKAHEAD < len(units):
            pending.append(scores(n + A_LOOKAHEAD))
        j = g * per_kv + jl
        sinks = (sink_ref[2 * j] * LOG2E, sink_ref[2 * j + 1] * LOG2E)
        outs.append(softmax_pv(pending.pop(0), sinks, g, jj))
        if jj == nslice - 1:
            last_outs = outs[-nslice:]
            o_pair = jnp.concatenate(
                [jnp.concatenate([o[:, :LANES] for o in last_outs], axis=1),
                 jnp.concatenate([o[:, LANES:] for o in last_outs], axis=1)], axis=0).T
            gt = gate_ref[j].astype(F32)
            gated.append((o_pair * _silu(gt)).astype(BF16))
            if jl == per_kv - 1 and (g + 1) % A_OUT_GROUPS == 0:
                npair = per_kv * A_OUT_GROUPS
                wg = w_ref[(j + 1 - npair) * LANES:(j + 1) * LANES, :]
                y = y + jnp.dot(jnp.concatenate(gated[-npair:], axis=1), wg,
                                preferred_element_type=F32)
    if final:
        y = _rms(y, fg_ref[...])
    o_ref[...] = y


def _attention(x2d, q3, gate3, kk, vt, sink, w_out, fin_g, bsz, seq, final):
    t = x2d.shape[0]
    nqb = seq // A_QBLK
    nkb = seq // A_WIN
    nsl = A_QBLK // A_WIN
    vt4 = vt.reshape(bsz, A_KVH, A_HEAD, seq)
    grid_spec = pltpu.PrefetchScalarGridSpec(
        num_scalar_prefetch=1,
        grid=(bsz, nqb),
        in_specs=[
            pl.BlockSpec((A_PAIRS, A_QBLK, LANES), lambda b, i, s: (0, b * nqb + i, 0)),
            pl.BlockSpec((A_PAIRS, A_QBLK, LANES), lambda b, i, s: (0, b * nqb + i, 0)),
            pl.BlockSpec((A_KVH, A_WIN, LANES),
                         lambda b, i, s: (0, b * nkb + jnp.maximum(nsl * i - 1, 0), 0)),
            pl.BlockSpec((A_KVH, A_QBLK, LANES), lambda b, i, s: (0, b * nqb + i, 0)),
            pl.BlockSpec((A_KVH, A_WIN, LANES),
                         lambda b, i, s: (0, b * nkb + jnp.minimum(nsl * i + nsl, nkb - 1), 0)),
            pl.BlockSpec((1, A_KVH, A_HEAD, A_WIN),
                         lambda b, i, s: (b, 0, 0, jnp.maximum(nsl * i - 1, 0))),
            pl.BlockSpec((1, A_KVH, A_HEAD, A_QBLK), lambda b, i, s: (b, 0, 0, i)),
            pl.BlockSpec((1, A_KVH, A_HEAD, A_WIN),
                         lambda b, i, s: (b, 0, 0, jnp.minimum(nsl * i + nsl, nkb - 1))),
            pl.BlockSpec((A_QBLK, D_MODEL), lambda b, i, s: (b * nqb + i, 0)),
            pl.BlockSpec((BRANCH, D_MODEL), lambda b, i, s: (0, 0), pipeline_mode=pl.Buffered(1)),
            pl.BlockSpec((1, D_MODEL), lambda b, i, s: (0, 0), pipeline_mode=pl.Buffered(1)),
        ],
        out_specs=pl.BlockSpec((A_QBLK, D_MODEL), lambda b, i, s: (b * nqb + i, 0)),
        scratch_shapes=[
            pltpu.VMEM((A_KVH, A_SPAN, LANES), BF16),
            pltpu.VMEM((A_KVH, A_HEAD + A_VPAD, A_SPAN), BF16),
            pltpu.VMEM((4, A_WIN, LANES), F32),
        ],
    )
    return pl.pallas_call(
        functools.partial(_attn_kernel, final=final),
        grid_spec=grid_spec,
        out_shape=jax.ShapeDtypeStruct((t, D_MODEL), F32),
        compiler_params=_params(("parallel", "parallel")),
        name="attention",
    )(sink, q3, gate3, kk, kk, kk, vt4, vt4, vt4, x2d, w_out, fin_g)


def _retention_start(lg_ref, st_ref, intra_ref, dec_ref, backward):
    n = B_CHUNK
    ii = lax.broadcasted_iota(jnp.int32, (n, n), 0)
    jj = lax.broadcasted_iota(jnp.int32, (n, n), 1)
    col = lax.broadcasted_iota(jnp.int32, (n, LANES), 0).astype(F32)
    st_ref[...] = jnp.zeros_like(st_ref)
    for h in range(B_HEADS):
        lg = lg_ref[1 if backward else 0, h]
        if backward:
            mask = jj > ii
            dist = (jj - ii).astype(F32)
            dec_ref[h, 0] = jnp.exp((n - col) * lg)
            dec_ref[h, 1] = jnp.exp(col * lg)
        else:
            mask = ii >= jj
            dist = (ii - jj).astype(F32)
            dec_ref[h, 0] = jnp.exp((col + 1.0) * lg)
            dec_ref[h, 1] = jnp.exp((n - 1.0 - col) * lg)
        intra_ref[h] = jnp.where(mask, jnp.exp(jnp.where(mask, dist, 0.0) * lg), 0.0)


def _retention_open(q_ref, k_ref, v_ref, row):
    heads = range(B_HEADS)
    qs = [q_ref[row, :, h * B_QK:(h + 1) * B_QK] for h in heads]
    ks = [k_ref[row, :, h * B_QK:(h + 1) * B_QK] for h in heads]
    vs = [v_ref[row, :, h * B_V:(h + 1) * B_V] for h in heads]
    sc = [lax.dot_general(qs[h], ks[h], (((1,), (1,)), ((), ())), preferred_element_type=F32)
          for h in heads]
    return qs, ks, vs, sc


def _retention_heads(lg_ref, opened, st_ref, intra_ref, dec_ref, row, backward, emit):
    qs, ks, vs, sc = opened
    for h in range(B_HEADS):
        q_dec = jnp.concatenate([dec_ref[h, 0]] * (B_QK // LANES), axis=1)
        k_dec = jnp.concatenate([dec_ref[h, 1]] * (B_QK // LANES), axis=1)
        c_dec = jnp.exp(jnp.full((1, B_V), B_CHUNK * lg_ref[1 if backward else 0, h], F32))
        st = st_ref[row, h]
        lhs = jnp.concatenate([(sc[h] * intra_ref[h]).astype(BF16),
                               (qs[h].astype(F32) * q_dec).astype(BF16)], axis=1)
        rhs = jnp.concatenate([vs[h], st.astype(BF16)], axis=0)
        out = jnp.dot(lhs, rhs, preferred_element_type=F32)
        kd_t = (ks[h].astype(F32) * k_dec).T.astype(BF16)
        st_ref[row, h] = st * c_dec + jnp.dot(kd_t, vs[h], preferred_element_type=F32)
        emit(h, out)


def _retention_bwd_kernel(lg_ref, q_ref, k_ref, v_ref, o_ref, st_ref, intra_ref, dec_ref):
    @pl.when(pl.program_id(1) == 0)
    def _():
        _retention_start(lg_ref, st_ref, intra_ref, dec_ref, True)

    rows = range(q_ref.shape[0])
    opened = [_retention_open(q_ref, k_ref, v_ref, row) for row in rows]
    for row in rows:
        def emit(h, o, row=row):
            o_ref[row, :, h * B_V:(h + 1) * B_V] = o.astype(BF16)

        _retention_heads(lg_ref, opened[row], st_ref, intra_ref, dec_ref, row, True, emit)


def _retention_fwd_kernel(lg_ref, q_ref, k_ref, v_ref, gate_ref, ob_ref, x_ref, w_ref, o_ref,
                          st_ref, intra_ref, dec_ref):
    @pl.when(pl.program_id(1) == 0)
    def _():
        _retention_start(lg_ref, st_ref, intra_ref, dec_ref, False)

    rows = range(q_ref.shape[0])
    opened = [_retention_open(q_ref, k_ref, v_ref, row) for row in rows]
    gated = []
    for row in rows:
        parts = []

        def emit(h, o_f, row=row, parts=parts):
            sl = slice(h * B_V, (h + 1) * B_V)
            o = o_f + ob_ref[row, :, sl].astype(F32)
            on = o * lax.rsqrt(jnp.mean(o * o, axis=-1, keepdims=True) + NORM_EPS)
            parts.append((on * _silu(gate_ref[row, :, sl].astype(F32))).astype(BF16))

        _retention_heads(lg_ref, opened[row], st_ref, intra_ref, dec_ref, row, False, emit)
        gated.append(jnp.concatenate(parts, axis=1))
    for row in rows:
        o_ref[row] = x_ref[row] + jnp.dot(gated[row], w_ref[...], preferred_element_type=F32)


def _retention(proj3, log_g, x3, w_out):
    bsz, seq, _ = proj3.shape
    nc = seq // B_CHUNK
    nqk = B_HEADS * B_QK
    rows = B_ROWS if bsz % B_ROWS == 0 else 1
    smem = pl.BlockSpec(memory_space=pltpu.SMEM)
    state = [pltpu.VMEM((rows, B_HEADS, B_QK, B_V), F32),
             pltpu.VMEM((B_HEADS, B_CHUNK, B_CHUNK), F32),
             pltpu.VMEM((B_HEADS, 2, B_CHUNK, LANES), F32)]

    def blk(width, chunk, col):
        return pl.BlockSpec((rows, B_CHUNK, width), lambda b, t: (b, chunk(t), col))

    def rev(t):
        return nc - 1 - t

    def fwd(t):
        return t

    o_b = pl.pallas_call(
        _retention_bwd_kernel,
        grid=(bsz // rows, nc),
        in_specs=[smem, blk(nqk, rev, 0), blk(nqk, rev, 1), blk(BRANCH, rev, 1)],
        out_specs=blk(BRANCH, rev, 0),
        out_shape=jax.ShapeDtypeStruct((bsz, seq, BRANCH), BF16),
        scratch_shapes=state,
        compiler_params=_params(("parallel", "arbitrary")),
        name="retention_bwd",
    )(log_g, proj3, proj3, proj3)
    return pl.pallas_call(
        _retention_fwd_kernel,
        grid=(bsz // rows, nc),
        in_specs=[smem, blk(nqk, fwd, 0), blk(nqk, fwd, 1), blk(BRANCH, fwd, 1),
                  blk(BRANCH, fwd, 2), blk(BRANCH, fwd, 0), blk(D_MODEL, fwd, 0),
                  pl.BlockSpec((BRANCH, D_MODEL), lambda b, t: (0, 0), pipeline_mode=pl.Buffered(1))],
        out_specs=blk(D_MODEL, fwd, 0),
        out_shape=jax.ShapeDtypeStruct((bsz, seq, D_MODEL), F32),
        scratch_shapes=state,
        compiler_params=_params(("parallel", "arbitrary")),
        name="retention_fwd",
    )(log_g, proj3, proj3, proj3, proj3, o_b, x3, w_out)


C_HALF = C_GDIM // 2
C_MAIN = C_GROUPS * C_HALF
C_FOLD = 2 * C_HALF + LANES
C_NYQ_ROWS = 16


def _fold_kernel(w_ref, cs_ref, a_ref, b_ref, sp_ref):
    r = jnp.dot(w_ref[...], cs_ref[0], preferred_element_type=F32,
                precision=lax.Precision.HIGHEST)
    a_ref[...] = r[:, :C_HALF].astype(BF16)
    b_ref[...] = r[:, C_HALF:2 * C_HALF].astype(BF16)

    @pl.when(pl.program_id(0) == 0)
    def _():
        sp_ref[...] = jnp.zeros_like(sp_ref)

    sp_ref[...] += r[:, 2 * C_HALF:]


def _fold_channel_dft(w_u, cs):
    spec = pl.BlockSpec((D_MODEL, C_HALF), lambda g: (0, g))
    return pl.pallas_call(
        _fold_kernel,
        grid=(C_GROUPS,),
        in_specs=[pl.BlockSpec((D_MODEL, C_GDIM), lambda g: (0, g)),
                  pl.BlockSpec((1, C_GDIM, C_FOLD), lambda g: (g, 0, 0))],
        out_specs=[spec, spec, pl.BlockSpec((D_MODEL, LANES), lambda g: (0, 0))],
        out_shape=[jax.ShapeDtypeStruct((D_MODEL, C_MAIN), BF16)] * 2
        + [jax.ShapeDtypeStruct((D_MODEL, LANES), F32)],
        compiler_params=_params(("arbitrary",)),
        name="fold_channel_dft",
    )(w_u, cs)


DFT_ROWS = 512


def _dft_mix_kernel(cm_ref, sm_ref, a_ref, b_ref, gm_ref, gp_ref, sp_ref, gsp_ref,
                    om_ref, op_ref, osp_ref):
    n = cm_ref.shape[0]
    rc = min(n, DFT_ROWS)
    a = a_ref[0]
    b = b_ref[0]
    for r in range(n // rc):
        rows = slice(r * rc, (r + 1) * rc)
        ea = jnp.dot(cm_ref[rows, :], a, preferred_element_type=F32)
        eb = jnp.dot(sm_ref[rows, :], b, preferred_element_type=F32)
        om_ref[0, rows, :] = ((ea - eb) * _silu(gm_ref[0, rows, :].astype(F32))).astype(BF16)
        op_ref[0, rows, :] = ((ea + eb) * _silu(gp_ref[0, rows, :].astype(F32))).astype(BF16)

    @pl.when(pl.program_id(1) == 0)
    def _():
        sp_t = sp_ref[0].astype(F32).T[:C_NYQ_ROWS, :].astype(BF16)
        es_t = jnp.dot(sp_t, cm_ref[...], preferred_element_type=F32)
        es = jnp.concatenate([es_t, jnp.zeros((LANES - C_NYQ_ROWS, n), F32)], axis=0).T
        osp_ref[0] = (es * _silu(gsp_ref[0].astype(F32))).astype(BF16)


def _trig(m, period, scale):
    ang = (2.0 * math.pi / period) * (m % period).astype(F32)
    return jnp.cos(ang) * scale, jnp.sin(ang) * scale


def _fourier_mix(proj3):
    bsz, seq, _ = proj3.shape
    n = jnp.arange(seq, dtype=jnp.int32)[None, None, :]
    hi = jnp.arange(seq // LANES, dtype=jnp.int32)[:, None, None] * LANES
    lo = jnp.arange(LANES, dtype=jnp.int32)[None, :, None]
    ca, sa = _trig(hi * n, seq, seq ** -0.5)
    cb, sb = _trig(lo * n, seq, 1.0)
    cm = (ca * cb - sa * sb).reshape(seq, seq)
    sm = (sa * cb + ca * sb).reshape(seq, seq)
    tc = 512
    nct = C_MAIN // tc
    mspec = pl.BlockSpec((seq, seq), lambda b, c: (0, 0), pipeline_mode=pl.Buffered(1))

    def dspec(width, idx):
        return pl.BlockSpec((1, seq, width), lambda b, c: (b, 0, idx(c)))

    sp_blk = 4 * C_MAIN // LANES
    return pl.pallas_call(
        _dft_mix_kernel,
        grid=(bsz, nct),
        in_specs=[mspec, mspec,
                  dspec(tc, lambda c: c), dspec(tc, lambda c: nct + c),
                  dspec(tc, lambda c: 2 * nct + c), dspec(tc, lambda c: 3 * nct + c),
                  dspec(LANES, lambda c: sp_blk), dspec(LANES, lambda c: sp_blk + 1)],
        out_specs=[dspec(tc, lambda c: c), dspec(tc, lambda c: c), dspec(LANES, lambda c: 0)],
        out_shape=[jax.ShapeDtypeStruct((bsz, seq, C_MAIN), BF16)] * 2
        + [jax.ShapeDtypeStruct((bsz, seq, LANES), BF16)],
        compiler_params=_params(("parallel", "arbitrary")),
        name="dft_mix",
    )(cm.astype(BF16), sm.astype(BF16), proj3, proj3, proj3, proj3, proj3, proj3)


C_GRP = 8
C_SUB = 64
C_ROWS = 256
C_NCHUNK = 1024


def _inproj_perm_kernel(x_ref, g_ref, w_ref, o_ref, xs_ref):
    rh = x_ref.shape[1]
    n = w_ref.shape[1]
    for i in range(C_GRP):
        xs_ref[i * rh:(i + 1) * rh, :] = x_ref[0, :, i, :]
    per_dot = C_ROWS // rh
    for j in range(C_GRP // per_dot):
        h = _rms(xs_ref[j * C_ROWS:(j + 1) * C_ROWS, :], g_ref[...]).astype(BF16)
        for c0 in range(0, n, C_NCHUNK):
            cols = slice(c0, min(c0 + C_NCHUNK, n))
            acc = jnp.dot(h, w_ref[:, cols], preferred_element_type=F32)
            for i in range(per_dot):
                o_ref[0, j * per_dot + i, :, cols] = acc[i * rh:(i + 1) * rh, :].astype(BF16)


def _inproj_perm(x4, g, w):
    bsz, r, _, _ = x4.shape
    n = w.shape[1]
    rh = min(r, C_SUB)
    return pl.pallas_call(
        _inproj_perm_kernel,
        grid=(bsz, r // C_GRP, r // rh),
        in_specs=[
            pl.BlockSpec((1, rh, C_GRP, D_MODEL), lambda b, m, s: (b, s, m, 0)),
            pl.BlockSpec((1, D_MODEL), lambda b, m, s: (0, 0), pipeline_mode=pl.Buffered(1)),
            pl.BlockSpec((D_MODEL, n), lambda b, m, s: (0, 0), pipeline_mode=pl.Buffered(1)),
        ],
        out_specs=pl.BlockSpec((1, C_GRP, rh, n), lambda b, m, s: (b, m, s, 0)),
        out_shape=jax.ShapeDtypeStruct((bsz, r, r, n), BF16),
        scratch_shapes=[pltpu.VMEM((C_GRP * rh, D_MODEL), F32)],
        compiler_params=_params(("parallel", "parallel", "parallel")),
        name="inproj_perm",
    )(x4, g, w)


def _dft_stage1_kernel(d_ref, x_ref, sp_ref, yr_ref, yi_ref, ysr_ref, ysi_ref):
    r = x_ref.shape[2]
    for i in range(C_GRP):
        dc = d_ref[i, :r, :]
        ds = d_ref[i, r:, :]
        yr_ref[0, :, i, :] = jnp.dot(dc, x_ref[0, i], preferred_element_type=F32)
        yi_ref[0, :, i, :] = jnp.dot(ds, x_ref[0, i], preferred_element_type=F32)
        ysr_ref[0, :, i, :] = jnp.dot(dc, sp_ref[0, i], preferred_element_type=F32)
        ysi_ref[0, :, i, :] = jnp.dot(ds, sp_ref[0, i], preferred_element_type=F32)


def _dft_stage1(proj4, dtab):
    bsz, r, _, _ = proj4.shape
    tc = 2 * C_MAIN
    nct = 2 * C_MAIN // tc

    def yspec(width, idx):
        return pl.BlockSpec((1, r, C_GRP, width), lambda b, m, c: (b, 0, m, idx(c)))

    return pl.pallas_call(
        _dft_stage1_kernel,
        grid=(bsz, r // C_GRP, nct),
        in_specs=[
            pl.BlockSpec((C_GRP, 2 * r, r), lambda b, m, c: (m, 0, 0)),
            pl.BlockSpec((1, C_GRP, r, tc), lambda b, m, c: (b, m, 0, c)),
            pl.BlockSpec((1, C_GRP, r, LANES), lambda b, m, c: (b, m, 0, 4 * C_MAIN // LANES)),
        ],
        out_specs=[yspec(tc, lambda c: c), yspec(tc, lambda c: c),
                   yspec(LANES, lambda c: 0), yspec(LANES, lambda c: 0)],
        out_shape=[jax.ShapeDtypeStruct((bsz, r, r, 2 * C_MAIN), F32)] * 2
        + [jax.ShapeDtypeStruct((bsz, r, r, LANES), F32)] * 2,
        compiler_params=_params(("parallel", "parallel", "arbitrary")),
        name="dft_stage1",
    )(dtab, proj4, proj4)


def _dft_stage2_kernel(c1_ref, s1_ref, yra_ref, yia_ref, yrb_ref, yib_ref, gm_ref, gp_ref,
                       wm_ref, wp_ref, ysr_ref, ysi_ref, gsp_ref, wsp_ref, x_ref, o_ref, acc_ref):
    c = pl.program_id(2)
    r = c1_ref.shape[0]
    c1 = c1_ref[...]
    s1 = s1_ref[...]

    def dot(m, y):
        return jnp.dot(m, y.astype(BF16), preferred_element_type=F32)

    @pl.when(c == 0)
    def _():
        parts = []
        for i in range(C_GRP):
            es = dot(c1, ysr_ref[0, i]) - dot(s1, ysi_ref[0, i])
            parts.append((es * _silu(gsp_ref[0, i].astype(F32))).astype(BF16))
        nyq = jnp.dot(jnp.concatenate(parts, axis=0), wsp_ref[...], preferred_element_type=F32)
        for i in range(C_GRP):
            acc_ref[i * r:(i + 1) * r, :] = x_ref[0, :, i, :] + nyq[i * r:(i + 1) * r, :]

    minus, plus = [], []
    for i in range(C_GRP):
        ea = dot(c1, yra_ref[0, i]) - dot(s1, yia_ref[0, i])
        eb = dot(c1, yib_ref[0, i]) + dot(s1, yrb_ref[0, i])
        minus.append(((ea - eb) * _silu(gm_ref[0, i].astype(F32))).astype(BF16))
        plus.append(((ea + eb) * _silu(gp_ref[0, i].astype(F32))).astype(BF16))
    acc_ref[...] += (jnp.dot(jnp.concatenate(minus, axis=0), wm_ref[...], preferred_element_type=F32)
                     + jnp.dot(jnp.concatenate(plus, axis=0), wp_ref[...], preferred_element_type=F32))

    @pl.when(c == pl.num_programs(2) - 1)
    def _():
        for i in range(C_GRP):
            o_ref[0, :, i, :] = acc_ref[i * r:(i + 1) * r, :]


def _dft_stage2(c1, s1, y_r, y_i, y_sr, y_si, proj4, w_out, x4):
    bsz, r, _, _ = x4.shape
    tc = 512
    nct = C_MAIN // tc
    mspec = pl.BlockSpec((r, r), lambda b, m, c: (0, 0), pipeline_mode=pl.Buffered(1))

    def slab(width, idx):
        return pl.BlockSpec((1, C_GRP, r, width), lambda b, m, c: (b, m, 0, idx(c)))

    def wspec(rows, idx):
        return pl.BlockSpec((rows, D_MODEL), lambda b, m, c: (idx(c), 0))

    xspec = pl.BlockSpec((1, r, C_GRP, D_MODEL), lambda b, m, c: (b, 0, m, 0))
    sp_blk = 4 * C_MAIN // LANES
    return pl.pallas_call(
        _dft_stage2_kernel,
        grid=(bsz, r // C_GRP, nct),
        in_specs=[
            mspec, mspec,
            slab(tc, lambda c: c), slab(tc, lambda c: c),
            slab(tc, lambda c: nct + c), slab(tc, lambda c: nct + c),
            slab(tc, lambda c: 2 * nct + c), slab(tc, lambda c: 3 * nct + c),
            wspec(tc, lambda c: c), wspec(tc, lambda c: nct + c),
            slab(LANES, lambda c: 0), slab(LANES, lambda c: 0),
            slab(LANES, lambda c: sp_blk + 1),
            wspec(LANES, lambda c: 2 * C_MAIN // LANES),
            xspec,
        ],
        out_specs=xspec,
        out_shape=jax.ShapeDtypeStruct(x4.shape, F32),
        scratch_shapes=[pltpu.VMEM((C_GRP * r, D_MODEL), F32)],
        compiler_params=_params(("parallel", "parallel", "arbitrary")),
        name="dft_stage2",
    )(c1, s1, y_r, y_i, y_r, y_i, proj4, proj4, w_out, w_out, y_sr, y_si, proj4, w_out, x4)


def _fourier_long(x2d, g, wc, w_out, bsz, seq):
    r = math.isqrt(seq)
    assert r * r == seq and r % C_GRP == 0
    x4 = x2d.reshape(bsz, r, r, D_MODEL)
    proj4 = _inproj_perm(x4, g, wc)
    idx = jnp.arange(r, dtype=jnp.int32)
    k2 = idx[None, :, None]
    ca, sa = _trig(k2 * idx[:, None, None], seq, r ** -0.5)
    cb, sb = _trig(k2 * (r * idx[None, None, :]), seq, 1.0)
    dc, ds = ca * cb - sa * sb, sa * cb + ca * sb
    dtab = jnp.concatenate([dc, ds], axis=1).astype(BF16)
    y_r, y_i, y_sr, y_si = _dft_stage1(proj4, dtab)
    c1, s1 = _trig(idx[:, None] * idx[None, :], r, r ** -0.5)
    out = _dft_stage2(c1.astype(BF16), s1.astype(BF16), y_r, y_i, y_sr, y_si, proj4, w_out, x4)
    return out.reshape(bsz * seq, D_MODEL)


OUT_TM = 512


def _outproj_kernel(mm_ref, mp_ref, ms_ref, x_ref, w_ref, o_ref):
    acc = jnp.dot(mm_ref[...], w_ref[:C_MAIN, :], preferred_element_type=F32)
    acc = acc + jnp.dot(mp_ref[...], w_ref[C_MAIN:2 * C_MAIN, :], preferred_element_type=F32)
    acc = acc + jnp.dot(ms_ref[...], w_ref[2 * C_MAIN:, :], preferred_element_type=F32)
    o_ref[...] = x_ref[...] + acc


def _outproj(minus2d, plus2d, nyq2d, x2d, w_out):
    t = x2d.shape[0]
    return pl.pallas_call(
        _outproj_kernel,
        grid=(t // OUT_TM,),
        in_specs=[
            pl.BlockSpec((OUT_TM, C_MAIN), lambda i: (i, 0)),
            pl.BlockSpec((OUT_TM, C_MAIN), lambda i: (i, 0)),
            pl.BlockSpec((OUT_TM, LANES), lambda i: (i, 0)),
            pl.BlockSpec((OUT_TM, D_MODEL), lambda i: (i, 0)),
            pl.BlockSpec((2 * C_MAIN + LANES, D_MODEL), lambda i: (0, 0), pipeline_mode=pl.Buffered(1)),
        ],
        out_specs=pl.BlockSpec((OUT_TM, D_MODEL), lambda i: (i, 0)),
        out_shape=jax.ShapeDtypeStruct((t, D_MODEL), F32),
        compiler_params=_params(("parallel",)),
        name="outproj",
    )(minus2d, plus2d, nyq2d, x2d, w_out)


def _rope_tables_a(seq):
    half = A_ROT // 2
    inv_freq = jnp.exp(-(jnp.arange(half, dtype=F32) * (2.0 / A_ROT)) * math.log(A_THETA))
    ang = jnp.arange(seq, dtype=F32)[:, None] * inv_freq[None, :]
    cos, sin = jnp.cos(ang), jnp.sin(ang)
    d = jnp.arange(LANES) % A_HEAD
    f = d % half
    cm = jnp.where(d[None, :] < A_ROT, cos[:, f], 1.0)
    s1 = jnp.where(d[None, :] < half, -sin[:, f], 0.0)
    s2 = jnp.where((d[None, :] >= half) & (d[None, :] < A_ROT), sin[:, f], 0.0)
    qs = A_HEAD ** -0.5 * LOG2E
    return jnp.stack([cm * qs, s1 * qs, s2 * qs, cm, s1, s2]).astype(F32)


def _rope_tables_b(seq):
    half = B_QK // 2
    inv_freq = jnp.exp(-(jnp.arange(half, dtype=F32) * (2.0 / B_QK)) * math.log(B_THETA))
    ang = jnp.arange(seq, dtype=F32)[:, None] * inv_freq[None, :]
    cos, sin = jnp.cos(ang), jnp.sin(ang)
    ks = B_QK ** -0.5
    return jnp.stack([cos, sin, cos * ks, sin * ks]).astype(F32)


def _weights_a(w_in):
    nq = A_QH * A_HEAD
    nkv = A_KVH * A_HEAD
    q = w_in[:, :nq]
    kv = w_in[:, nq:nq + 2 * nkv]
    gate = w_in[:, nq + 2 * nkv:]
    return jnp.concatenate([q, gate, kv], axis=1).astype(BF16)


def _c_index():
    g = jnp.arange(C_GROUPS, dtype=jnp.int32)[:, None] * C_GDIM
    m = jnp.arange(C_HALF, dtype=jnp.int32)[None, :]
    minus = (g + m).reshape(-1)
    plus = (g + jnp.where(m == 0, 0, C_GDIM - m)).reshape(-1)
    nyq = (g + C_HALF).reshape(-1)
    return minus, plus, nyq, jnp.tile(m == 0, (C_GROUPS, 1)).reshape(-1)


def _weights_c(w_in):
    ch = jnp.arange(C_GDIM, dtype=jnp.int32)
    m = jnp.arange(C_HALF, dtype=jnp.int32)
    cm, sm = _trig(ch[:, None] * m[None, :], C_GDIM, C_GDIM ** -0.5)
    nyq, _ = _trig(ch * C_HALF, C_GDIM, C_GDIM ** -0.5)
    lane = jnp.arange(LANES)[None, None, :] == jnp.arange(C_GROUPS)[:, None, None]
    nyq = jnp.where(lane, nyq[None, :, None], 0.0)
    cs = jnp.concatenate([jnp.broadcast_to(cm, (C_GROUPS,) + cm.shape),
                          jnp.broadcast_to(sm, (C_GROUPS,) + sm.shape), nyq], axis=2)
    w_a, w_b, w_sp = _fold_channel_dft(w_in[:, :BRANCH], cs)
    minus, plus, nyq_idx, _ = _c_index()
    gate = w_in[:, BRANCH:]
    g_sp = jnp.zeros((D_MODEL, LANES), F32).at[:, :C_GROUPS].set(gate[:, nyq_idx])
    return jnp.concatenate([w_a, w_b, gate[:, minus].astype(BF16), gate[:, plus].astype(BF16),
                            w_sp.astype(BF16), g_sp.astype(BF16)], axis=1)


def _weights_c_out(w_out):
    minus, plus, nyq_idx, dup = _c_index()
    w_plus = jnp.where(dup[:, None], 0.0, w_out[plus])
    w_sp = jnp.zeros((LANES, D_MODEL), F32).at[:C_GROUPS].set(w_out[nyq_idx])
    return jnp.concatenate([w_out[minus], w_plus, w_sp], axis=0).astype(BF16)


def _trunk(x, norm_g, fin_g, wa, a_sink, a_w_out, wb, log_g, b_w_out, wc, c_w_out):
    bsz, seq, _ = x.shape
    t = bsz * seq
    x2d = x.reshape(t, D_MODEL)
    tabs_a = _rope_tables_a(seq)
    tabs_b = _rope_tables_b(seq)
    fin = fin_g.reshape(1, D_MODEL)

    def layer_a(x2d, layer, j, final):
        q3, gate3, kk, vt = _inproj_a(x2d, norm_g[layer].reshape(1, D_MODEL), wa[j], tabs_a, bsz, seq)
        return _attention(x2d, q3, gate3, kk, vt, a_sink[j], a_w_out[j], fin, bsz, seq, final)

    x2d = layer_a(x2d, 0, 0, False)

    proj = _inproj(x2d, norm_g[1].reshape(1, D_MODEL), wb, tabs_b, seq, 2 * B_HEADS)
    x2d = _retention(proj.reshape(bsz, seq, -1), log_g, x2d.reshape(bsz, seq, D_MODEL),
                     b_w_out).reshape(t, D_MODEL)

    if seq <= C_DIRECT_MAX:
        proj = _inproj(x2d, norm_g[2].reshape(1, D_MODEL), wc, None, seq, 0)
        minus, plus, nyq = _fourier_mix(proj.reshape(bsz, seq, -1))
        x2d = _outproj(minus.reshape(t, C_MAIN), plus.reshape(t, C_MAIN), nyq.reshape(t, LANES),
                       x2d, c_w_out)
    else:
        x2d = _fourier_long(x2d, norm_g[2].reshape(1, D_MODEL), wc, c_w_out, bsz, seq)

    x2d = layer_a(x2d, 3, 1, True)
    return x2d.reshape(bsz, seq, D_MODEL)


def kernel(x_prompt, x_sample, norm_g, final_norm_g, a_w_in, a_sink, a_w_out, b_w_in, b_decay,
           b_w_out, c_w_in, c_w_out):
    wa = [_weights_a(a_w_in[j]) for j in range(a_w_in.shape[0])]
    a_out = [a_w_out[j].astype(BF16) for j in range(a_w_out.shape[0])]
    wb = b_w_in[0].astype(BF16)
    log_g = jax.nn.log_sigmoid(b_decay[0].astype(F32))
    wc = _weights_c(c_w_in[0])
    args = (norm_g, final_norm_g, wa, a_sink, a_out, wb, log_g, b_w_out[0].astype(BF16),
            wc, _weights_c_out(c_w_out[0]))
    return (_trunk(x_prompt, *args), _trunk(x_sample, *args))
```
